```python
import jax
import jax.numpy as jnp
from jax import lax
import numpy as np

D_MODEL = 1024
BATCH = 2
SEQ = 8192
DEPTH = 4
DEC_BATCH = 32
DEC_SEQ = 8
PAST_LEN = 8192
PAGE_SIZE = 128

HEAD_DIM = 64
N_HEADS = D_MODEL // HEAD_DIM
N_KV_HEADS = 4
GROUP = N_HEADS // N_KV_HEADS
Q_DIM = N_HEADS * HEAD_DIM
KV_DIM = N_KV_HEADS * HEAD_DIM
ROT_DIM = HEAD_DIM // 4
ROPE_THETA = 500000.0
ATTN_SCALE = HEAD_DIM ** -0.5
Q_BLOCK = 128
MOBA_BLOCK = 256
MOBA_TOPK = 3
CMP_LEN = 32
CMP_STRIDE = 16
CMP_HIDDEN = 64
SLC_BLOCK = 64
SLC_TOPK = 16
WINDOW = 512
IDX_HEADS = 8
IDX_DIM = 64
DSA_TOPK = 256
N_EXPERTS = 16
N_GROUPS = 4
EXPERTS_PER_GROUP = N_EXPERTS // N_GROUPS
TOP_K = 2
D_EXPERT = 256
MOE_BLOCK = 128
EPS = 1e-6
NEG_INF = -1e30
MIXERS = ('fox', 'moba', 'nsa', 'dsa')

kernel_name = 'hybrid_fox_moba_nsa_dsa_moe_step'


def rmsnorm(x, g):
    xf = x.astype(jnp.float32)
    y = xf * lax.rsqrt(jnp.mean(xf * xf, axis=-1, keepdims=True) + EPS)
    return (y * g.astype(jnp.float32)).astype(x.dtype)


def rope(x, pos):
    half = ROT_DIM // 2
    inv = ROPE_THETA ** (-jnp.arange(half, dtype=jnp.float32) / half)
    ang = pos.astype(jnp.float32)[:, None] * inv[None, :]
    cos, sin = jnp.cos(ang)[:, None, :], jnp.sin(ang)[:, None, :]
    xr = x[..., :ROT_DIM].astype(jnp.float32)
    x1, x2 = xr[..., :half], xr[..., half:]
    rot = jnp.concatenate([x1 * cos - x2 * sin, x2 * cos + x1 * sin], axis=-1)
    return jnp.concatenate([rot.astype(x.dtype), x[..., ROT_DIM:]], axis=-1)


def split_cols(y, sizes):
    return jnp.split(y, np.cumsum(sizes)[:-1].tolist(), axis=-1)


def masked_probs(s, mask):
    s = jnp.where(mask, s.astype(jnp.float32), NEG_INF)
    return jax.nn.softmax(s, axis=-1) * mask


def gather_pages(pool, page_table):
    g = pool[page_table]
    return g.reshape((page_table.shape[0], -1) + pool.shape[2:])


def pad_seq(a, mult):
    pad = (-a.shape[1]) % mult
    return jnp.pad(a, [(0, 0), (0, pad)] + [(0, 0)] * (a.ndim - 2))


def to_blocks(a, blk):
    b, L, hk, dh = a.shape
    return a.reshape(b, L // blk, blk, hk, dh).transpose(0, 3, 1, 2, 4)


def sweep_queries(fn, *q_arrays):
    T = q_arrays[0].shape[1]
    def body(start):
        chunks = [lax.dynamic_slice_in_dim(a, start, Q_BLOCK, axis=1) for a in q_arrays]
        return fn(start + jnp.arange(Q_BLOCK, dtype=jnp.int32), *chunks)
    out = lax.map(body, jnp.arange(T // Q_BLOCK, dtype=jnp.int32) * Q_BLOCK)
    out = jnp.moveaxis(out, 0, 1)
    return out.reshape((out.shape[0], T) + out.shape[3:])


def gather_blocks(kb, sel):
    b_i = jnp.arange(kb.shape[0])[:, None, None, None]
    h_i = jnp.arange(kb.shape[1])[None, None, :, None]
    return kb[b_i, h_i, sel]


def select_blocks(score, qpos, blk, n_top):
    nb = score.shape[-1]
    own = qpos // blk
    past = jnp.arange(nb)[None, :] < own[:, None]
    score = jnp.where(past[None, :, None, :], score, -jnp.inf)
    _, top = lax.top_k(score, min(n_top, nb))
    own_b = jnp.broadcast_to(own[None, :, None, None], top.shape[:-1] + (1,)).astype(jnp.int32)
    sel = jnp.concatenate([top.astype(jnp.int32), own_b], axis=-1)
    valid = jnp.concatenate([top < own_b, jnp.ones(own_b.shape, dtype=bool)], axis=-1)
    return sel, valid


def gathered_block_attn(q, qpos, kb, vb, sel, valid):
    blk = kb.shape[3]
    b, nq, hk, n = sel.shape
    kg, vg = gather_blocks(kb, sel), gather_blocks(vb, sel)
    kpos = sel[..., None] * blk + jnp.arange(blk)
    mask = valid[..., None] & (kpos <= qpos[None, :, None, None, None])
    s = jnp.einsum('bqhgd,bqhnkd->bqhgnk', q, kg).reshape(b, nq, hk, GROUP, n * blk) * ATTN_SCALE
    p = masked_probs(s, mask.reshape(b, nq, hk, 1, n * blk))
    return jnp.einsum('bqhgl,bqhld->bqhgd', p.astype(vg.dtype), vg.reshape(b, nq, hk, n * blk, HEAD_DIM))


def q_heads(q, pos, q_norm, use_rope=True):
    b, T, _ = q.shape
    q = rmsnorm(q.reshape(b, T, N_HEADS, HEAD_DIM), q_norm)
    if use_rope:
        q = rope(q, pos)
    return q.reshape(b, T, N_KV_HEADS, GROUP, HEAD_DIM)


def kv_heads(a):
    return a.reshape(a.shape[0], a.shape[1], N_KV_HEADS, HEAD_DIM)


def fox_project(h, w_in, b_f, q_norm, k_norm):
    q, k, v, fl = split_cols(h @ w_in, [Q_DIM, KV_DIM, KV_DIM, N_HEADS])
    q = q_heads(q, None, q_norm, use_rope=False)
    k = rmsnorm(kv_heads(k), k_norm)
    logf = jax.nn.log_sigmoid((fl + b_f).astype(jnp.float32))
    return q, k, kv_heads(v), logf


def fox_attend(q, qpos, fq, k, v, fcum):
    b, nq = q.shape[:2]
    L = k.shape[1]
    s = jnp.einsum('bqhgd,blhd->bqhgl', q, k).astype(jnp.float32) * ATTN_SCALE
    decay = fq.reshape(b, nq, N_KV_HEADS, GROUP, 1) - jnp.transpose(fcum.reshape(b, L, N_KV_HEADS, GROUP), (0, 2, 3, 1))[:, None]
    mask = (jnp.arange(L)[None, :] <= qpos[:, None])[None, :, None, None, :]
    p = masked_probs(s + decay, mask)
    return jnp.einsum('bqhgl,blhd->bqhgd', p.astype(v.dtype), v)


def fox_prompt(h, w_in, b_f, q_norm, k_norm, w_o):
    b, T, _ = h.shape
    q, k, v, logf = fox_project(h, w_in, b_f, q_norm, k_norm)
    fcum = jnp.cumsum(logf, axis=1)
    o = sweep_queries(lambda qpos, qc, fc: fox_attend(qc, qpos, fc, k, v, fcum), q, fcum)
    return o.reshape(b, T, Q_DIM) @ w_o, k, v, logf


def fox_sample(h, cache_k, cache_v, cache_logf, page_table, w_in, b_f, q_norm, k_norm, w_o):
    b, T, _ = h.shape
    past = page_table.shape[1] * cache_k.shape[1]
    q, k, v, logf = fox_project(h, w_in, b_f, q_norm, k_norm)
    K = jnp.concatenate([gather_pages(cache_k, page_table), k], axis=1)
    V = jnp.concatenate([gather_pages(cache_v, page_table), v], axis=1)
    fcum = jnp.cumsum(jnp.concatenate([gather_pages(cache_logf, page_table).astype(jnp.float32), logf], axis=1), axis=1)
    o = fox_attend(q, past + jnp.arange(T), fcum[:, past:], K, V, fcum)
    return o.reshape(b, T, Q_DIM) @ w_o, k, v, logf


def moba_project(h, pos, w_in, q_norm, k_norm):
    q, k, v = split_cols(h @ w_in, [Q_DIM, KV_DIM, KV_DIM])
    return q_heads(q, pos, q_norm), rope(rmsnorm(kv_heads(k), k_norm), pos), kv_heads(v)


def moba_prep(k, v):
    kb = to_blocks(pad_seq(k, MOBA_BLOCK), MOBA_BLOCK)
    vb = to_blocks(pad_seq(v, MOBA_BLOCK), MOBA_BLOCK)
    kbar = jnp.mean(kb.astype(jnp.float32), axis=3)
    return kb, vb, kbar


def moba_attend(q, qpos, kb, vb, kbar):
    gate = jnp.einsum('bqhgd,bhnd->bqhn', q.astype(jnp.float32), kbar)
    sel, valid = select_blocks(gate, qpos, MOBA_BLOCK, MOBA_TOPK)
    return gathered_block_attn(q, qpos, kb, vb, sel, valid)


def moba_prompt(h, w_in, q_norm, k_norm, w_o):
    b, T, _ = h.shape
    q, k, v = moba_project(h, jnp.arange(T), w_in, q_norm, k_norm)
    kb, vb, kbar = moba_prep(k, v)
    o = sweep_queries(lambda qpos, qc: moba_attend(qc, qpos, kb, vb, kbar), q)
    return o.reshape(b, T, Q_DIM) @ w_o, k, v


def moba_sample(h, cache_k, cache_v, page_table, w_in, q_norm, k_norm, w_o):
    b, T, _ = h.shape
    past = page_table.shape[1] * cache_k.shape[1]
    qpos = past + jnp.arange(T)
    q, k, v = moba_project(h, qpos, w_in, q_norm, k_norm)
    K = jnp.concatenate([gather_pages(cache_k, page_table), k], axis=1)
    V = jnp.concatenate([gather_pages(cache_v, page_table), v], axis=1)
    kb, vb, kbar = moba_prep(K, V)
    o = moba_attend(q, qpos, kb, vb, kbar)
    return o.reshape(b, T, Q_DIM) @ w_o, k, v


def nsa_project(h, pos, w_in, q_norm, ks_norm, kw_norm):
    b, T, _ = h.shape
    q, kc, vc, ks, vs, kw, vw, gl = split_cols(h @ w_in, [Q_DIM] + [KV_DIM] * 6 + [3 * N_HEADS])
    q = q_heads(q, pos, q_norm)
    ks = rope(rmsnorm(kv_heads(ks), ks_norm), pos)
    kw = rope(rmsnorm(kv_heads(kw), kw_norm), pos)
    gates = jax.nn.sigmoid(gl.astype(jnp.float32)).reshape(b, T, N_KV_HEADS, GROUP, 3)
    return q, kv_heads(kc), kv_heads(vc), ks, kv_heads(vs), kw, kv_heads(vw), gates


def compress(raw, pe, w1, w2):
    b, L, hk, dh = raw.shape
    r = raw.reshape(b, L // CMP_STRIDE, CMP_STRIDE, hk, dh)
    def flat(a):
        return jnp.transpose(a, (0, 1, 3, 2, 4)).reshape(b, L // CMP_STRIDE, hk, CMP_STRIDE * dh)
    lo = flat(r + pe[:CMP_STRIDE][:, None, :]) @ w1[:CMP_STRIDE * dh]
    hi = flat(r + pe[CMP_STRIDE:][:, None, :]) @ w1[CMP_STRIDE * dh:]
    return jax.nn.silu(lo[:, :-1] + hi[:, 1:]) @ w2


def nsa_prep(kc, vc, ks, vs, kc_norm, pe_k, w1_k, w2_k, pe_v, w1_v, w2_v):
    kcc = compress(pad_seq(kc, SLC_BLOCK), pe_k, w1_k, w2_k)
    nc = kcc.shape[1]
    cend = jnp.arange(nc) * CMP_STRIDE + CMP_LEN - 1
    kcc = rope(rmsnorm(kcc, kc_norm), cend)
    vcc = compress(pad_seq(vc, SLC_BLOCK), pe_v, w1_v, w2_v)
    ksb = to_blocks(pad_seq(ks, SLC_BLOCK), SLC_BLOCK)
    vsb = to_blocks(pad_seq(vs, SLC_BLOCK), SLC_BLOCK)
    ci = jnp.arange(nc)[:, None]
    sj = jnp.arange(ksb.shape[2])[None, :]
    cover = ((ci * CMP_STRIDE < (sj + 1) * SLC_BLOCK) & (ci * CMP_STRIDE + CMP_LEN > sj * SLC_BLOCK)).astype(jnp.float32)
    return kcc, vcc, cend, cover, ksb, vsb


def nsa_attend(q, qpos, gates, kcc, vcc, cend, cover, ksb, vsb, kw, vw, kwpos):
    sc = jnp.einsum('bqhgd,bchd->bqhgc', q, kcc) * ATTN_SCALE
    pc = masked_probs(sc, (cend[None, :] <= qpos[:, None])[None, :, None, None, :])
    oc = jnp.einsum('bqhgc,bchd->bqhgd', pc.astype(vcc.dtype), vcc)
    imp = jnp.einsum('bqhgc,cn->bqhn', pc, cover)
    sel, valid = select_blocks(imp, qpos, SLC_BLOCK, SLC_TOPK - 1)
    osl = gathered_block_attn(q, qpos, ksb, vsb, sel, valid)
    sw = jnp.einsum('bqhgd,blhd->bqhgl', q, kw) * ATTN_SCALE
    d = qpos[:, None] - kwpos[None, :]
    wmask = (d >= 0) & (d < WINDOW) & (kwpos[None, :] >= 0)
    pw = masked_probs(sw, wmask[None, :, None, None, :])
    ow = jnp.einsum('bqhgl,blhd->bqhgd', pw.astype(vw.dtype), vw)
    g = gates.astype(q.dtype)
    return g[..., 0:1] * oc + g[..., 1:2] * osl + g[..., 2:3] * ow


def nsa_prompt(h, w_in, q_norm, kc_norm, ks_norm, kw_norm, pe_k, w1_k, w2_k, pe_v, w1_v, w2_v, w_o):
    b, T, _ = h.shape
    q, kc, vc, ks, vs, kw, vw, gates = nsa_project(h, jnp.arange(T), w_in, q_norm, ks_norm, kw_norm)
    kcc, vcc, cend, cover, ksb, vsb = nsa_prep(kc, vc, ks, vs, kc_norm, pe_k, w1_k, w2_k, pe_v, w1_v, w2_v)
    lpad = ((0, 0), (WINDOW, 0), (0, 0), (0, 0))
    kwp, vwp = jnp.pad(kw, lpad), jnp.pad(vw, lpad)
    def chunk(qpos, qc, gc):
        start = qpos[0]
        kwc = lax.dynamic_slice_in_dim(kwp, start, WINDOW + Q_BLOCK, axis=1)
        vwc = lax.dynamic_slice_in_dim(vwp, start, WINDOW + Q_BLOCK, axis=1)
        kwpos = start - WINDOW + jnp.arange(WINDOW + Q_BLOCK)
        return nsa_attend(qc, qpos, gc, kcc, vcc, cend, cover, ksb, vsb, kwc, vwc, kwpos)
    o = sweep_queries(chunk, q, gates)
    wb = min(WINDOW, T)
    return o.reshape(b, T, Q_DIM) @ w_o, kc, vc, ks, vs, kw[:, T - wb:], vw[:, T - wb:]


def nsa_sample(h, cache_cmp_k, cache_cmp_v, cache_slc_k, cache_slc_v, win_k, win_v, page_table,
               w_in, q_norm, kc_norm, ks_norm, kw_norm, pe_k, w1_k, w2_k, pe_v, w1_v, w2_v, w_o):
    b, T, _ = h.shape
    past = page_table.shape[1] * cache_cmp_k.shape[1]
    qpos = past + jnp.arange(T)
    q, kc, vc, ks, vs, kw, vw, gates = nsa_project(h, qpos, w_in, q_norm, ks_norm, kw_norm)
    def cat(cache, new):
        return jnp.concatenate([gather_pages(cache, page_table), new], axis=1)
    kcc, vcc, cend, cover, ksb, vsb = nsa_prep(cat(cache_cmp_k, kc), cat(cache_cmp_v, vc), cat(cache_slc_k, ks),
                                               cat(cache_slc_v, vs), kc_norm, pe_k, w1_k, w2_k, pe_v, w1_v, w2_v)
    KW = jnp.concatenate([win_k, kw], axis=1)
    VW = jnp.concatenate([win_v, vw], axis=1)
    kwpos = past - win_k.shape[1] + jnp.arange(KW.shape[1])
    o = nsa_attend(q, qpos, gates, kcc, vcc, cend, cover, ksb, vsb, KW, VW, kwpos)
    lw = KW.shape[1]
    wb = min(WINDOW, lw)
    return o.reshape(b, T, Q_DIM) @ w_o, kc, vc, ks, vs, KW[:, lw - wb:], VW[:, lw - wb:]


def dsa_project(h, pos, w_in, q_norm, k_norm):
    b, T, _ = h.shape
    q, k, v, qi, ki, wi = split_cols(h @ w_in, [Q_DIM, KV_DIM, KV_DIM, IDX_HEADS * IDX_DIM, IDX_DIM, IDX_HEADS])
    q = q_heads(q, pos, q_norm)
    k = rope(rmsnorm(kv_heads(k), k_norm), pos)
    qi = rope(qi.reshape(b, T, IDX_HEADS, IDX_DIM), pos)
    ki = rope(ki[:, :, None, :], pos)[:, :, 0]
    wi = wi.astype(jnp.float32) * IDX_HEADS ** -0.5
    return q, k, kv_heads(v), qi, ki, wi


def dsa_attend(q, qpos, qi, wi, k, v, ki, n_sel):
    L = k.shape[1]
    rel = jax.nn.relu(jnp.einsum('bqie,ble->bqil', qi, ki).astype(jnp.float32))
    score = jnp.einsum('bqi,bqil->bql', wi, rel)
    score = jnp.where((jnp.arange(L)[None, :] <= qpos[:, None])[None], score, -jnp.inf)
    _, idx = lax.top_k(score, n_sel)
    valid = idx <= qpos[None, :, None]
    b_i = jnp.arange(k.shape[0])[:, None, None]
    kg, vg = k[b_i, idx], v[b_i, idx]
    s = jnp.einsum('bqhgd,bqnhd->bqhgn', q, kg) * ATTN_SCALE
    p = masked_probs(s, valid[:, :, None, None, :])
    return jnp.einsum('bqhgn,bqnhd->bqhgd', p.astype(vg.dtype), vg)


def dsa_prompt(h, w_in, q_norm, k_norm, w_o):
    b, T, _ = h.shape
    q, k, v, qi, ki, wi = dsa_project(h, jnp.arange(T), w_in, q_norm, k_norm)
    n_sel = min(DSA_TOPK, T // 4)
    o = sweep_queries(lambda qpos, qc, qic, wic: dsa_attend(qc, qpos, qic, wic, k, v, ki, n_sel), q, qi, wi)
    return o.reshape(b, T, Q_DIM) @ w_o, k, v, ki


def dsa_sample(h, cache_k, cache_v, cache_idx_k, page_table, w_in, q_norm, k_norm, w_o):
    b, T, _ = h.shape
    past = page_table.shape[1] * cache_k.shape[1]
    qpos = past + jnp.arange(T)
    q, k, v, qi, ki, wi = dsa_project(h, qpos, w_in, q_norm, k_norm)
    K = jnp.concatenate([gather_pages(cache_k, page_table), k], axis=1)
    V = jnp.concatenate([gather_pages(cache_v, page_table), v], axis=1)
    KI = jnp.concatenate([gather_pages(cache_idx_k, page_table), ki], axis=1)
    n_sel = min(DSA_TOPK, K.shape[1] // 4)
    o = dsa_attend(q, qpos, qi, wi, K, V, KI, n_sel)
    return o.reshape(b, T, Q_DIM) @ w_o, k, v, ki


def routed_experts(x, eidx, gate, w1, w3, w2):
    n, d = x.shape
    npair = n * TOP_K
    flat_e = eidx.reshape(-1).astype(jnp.int32)
    flat_t = jnp.repeat(jnp.arange(n, dtype=jnp.int32), TOP_K)
    flat_g = gate.reshape(-1)
    order = jnp.argsort(flat_e)
    se, st, sg = flat_e[order], flat_t[order], flat_g[order]
    counts = jnp.zeros((N_EXPERTS,), jnp.int32).at[flat_e].add(1)
    padded = (counts + MOE_BLOCK - 1) // MOE_BLOCK * MOE_BLOCK
    start = jnp.cumsum(counts) - counts
    ends = jnp.cumsum(padded)
    pstart = ends - padded
    dest = pstart[se] + jnp.arange(npair, dtype=jnp.int32) - start[se]
    n_blocks = -(-npair // MOE_BLOCK) + N_EXPERTS
    slot_tok = jnp.full((n_blocks * MOE_BLOCK,), n, jnp.int32).at[dest].set(st)
    xpad = jnp.concatenate([x, jnp.zeros((1, d), x.dtype)], axis=0)
    xb = xpad[slot_tok].reshape(n_blocks, MOE_BLOCK, d)
    block_e = jnp.minimum(jnp.searchsorted(ends, jnp.arange(n_blocks) * MOE_BLOCK, side='right'), N_EXPERTS - 1)
    def ffn(args):
        xblk, e = args
        return (jax.nn.silu(xblk @ w1[e]) * (xblk @ w3[e])) @ w2[e]
    yb = lax.map(ffn, (xb, block_e)).reshape(-1, d)
    return jnp.zeros((n, d), x.dtype).at[st].add(yb[dest] * sg[:, None].astype(yb.dtype))


def moe(h, router_w, router_b, w1, w3, w2):
    b, T, d = h.shape
    x = h.reshape(-1, d)
    n = x.shape[0]
    aff = jax.nn.sigmoid((x @ router_w).astype(jnp.float32))
    grp = (aff + router_b.astype(jnp.float32)).reshape(n, N_GROUPS, EXPERTS_PER_GROUP)
    grp_score = lax.top_k(grp, TOP_K)[0].sum(-1)
    g = jnp.argmax(grp_score, axis=-1)
    rows = jnp.arange(n)
    _, loc = lax.top_k(grp[rows, g], TOP_K)
    eidx = g[:, None] * EXPERTS_PER_GROUP + loc
    wsel = aff[rows[:, None], eidx]
    wsel = wsel / jnp.sum(wsel, axis=-1, keepdims=True)
    return routed_experts(x, eidx, wsel, w1, w3, w2).reshape(b, T, d)


def modulation(c, ada_w, ada_b):
    m = jax.nn.silu(c) @ ada_w + ada_b
    return [a[:, None, :] for a in jnp.split(m, 6, axis=-1)]


def adanorm(x, g, shift, scale):
    return rmsnorm(x, g) * (1 + scale) + shift


def setup_inputs(seed: int = 0) -> dict:
    keys = iter(jax.random.split(jax.random.key(seed), 80))
    def nrm(shape, scale=1.0):
        return jax.random.normal(next(keys), shape, jnp.float32) * scale
    def gain(shape):
        return 1.0 + nrm(shape, 0.05)
    n_pages = PAST_LEN // PAGE_SIZE
    n_pool = (5 * DEC_BATCH * n_pages + 3) // 4
    wb = min(WINDOW, PAST_LEN)
    kv_pool = (n_pool, PAGE_SIZE, N_KV_HEADS, HEAD_DIM)
    win = (DEC_BATCH, wb, N_KV_HEADS, HEAD_DIM)
    perm = jax.random.permutation(next(keys), n_pool)
    page_table = perm[:DEC_BATCH * n_pages].reshape(DEC_BATCH, n_pages).astype(jnp.int32)
    sd = D_MODEL ** -0.5
    so = Q_DIM ** -0.5
    return {
        'x_prompt': nrm((BATCH, SEQ, D_MODEL)),
        'x_sample': nrm((DEC_BATCH, DEC_SEQ, D_MODEL)),
        'cache_fox_k': nrm(kv_pool),
        'cache_fox_v': nrm(kv_pool),
        'cache_fox_logf': jax.nn.log_sigmoid(4.0 + nrm((n_pool, PAGE_SIZE, N_HEADS), 1.0)),
        'cache_moba_k': nrm(kv_pool),
        'cache_moba_v': nrm(kv_pool),
        'cache_nsa_cmp_k': nrm(kv_pool),
        'cache_nsa_cmp_v': nrm(kv_pool),
        'cache_nsa_slc_k': nrm(kv_pool),
        'cache_nsa_slc_v': nrm(kv_pool),
        'state_nsa_win_k': nrm(win),
        'state_nsa_win_v': nrm(win),
        'cache_dsa_k': nrm(kv_pool),
        'cache_dsa_v': nrm(kv_pool),
        'cache_dsa_idx_k': nrm((n_pool, PAGE_SIZE, IDX_DIM)),
        'page_table': page_table,
        'c_prompt': nrm((BATCH, D_MODEL)),
        'c_sample': nrm((DEC_BATCH, D_MODEL)),
        'ada_w': nrm((DEPTH, D_MODEL, 6 * D_MODEL), 0.5 * sd),
        'ada_b': nrm((DEPTH, 6 * D_MODEL), 0.02),
        'norm_mix': gain((DEPTH, D_MODEL)),
        'norm_ffn': gain((DEPTH, D_MODEL)),
        'fox_w_in': nrm((D_MODEL, Q_DIM + 2 * KV_DIM + N_HEADS), sd),
        'fox_b_f': jax.random.uniform(next(keys), (N_HEADS,), jnp.float32, 2.0, 7.0),
        'fox_q_norm': gain((HEAD_DIM,)),
        'fox_k_norm': gain((HEAD_DIM,)),
        'fox_w_o': nrm((Q_DIM, D_MODEL), so),
        'moba_w_in': nrm((D_MODEL, Q_DIM + 2 * KV_DIM), sd),
        'moba_q_norm': gain((HEAD_DIM,)),
        'moba_k_norm': gain((HEAD_DIM,)),
        'moba_w_o': nrm((Q_DIM, D_MODEL), so),
        'nsa_w_in': nrm((D_MODEL, Q_DIM + 6 * KV_DIM + 3 * N_HEADS), sd),
        'nsa_q_norm': gain((HEAD_DIM,)),
        'nsa_kc_norm': gain((HEAD_DIM,)),
        'nsa_ks_norm': gain((HEAD_DIM,)),
        'nsa_kw_norm': gain((HEAD_DIM,)),
        'nsa_pe_k': nrm((CMP_LEN, HEAD_DIM), 0.1),
        'nsa_w1_k': nrm((CMP_LEN * HEAD_DIM, CMP_HIDDEN), (CMP_LEN * HEAD_DIM) ** -0.5),
        'nsa_w2_k': nrm((CMP_HIDDEN, HEAD_DIM), CMP_HIDDEN ** -0.5),
        'nsa_pe_v': nrm((CMP_LEN, HEAD_DIM), 0.1),
        'nsa_w1_v': nrm((CMP_LEN * HEAD_DIM, CMP_HIDDEN), (CMP_LEN * HEAD_DIM) ** -0.5),
        'nsa_w2_v': nrm((CMP_HIDDEN, HEAD_DIM), CMP_HIDDEN ** -0.5),
        'nsa_w_o': nrm((Q_DIM, D_MODEL), so),
        'dsa_w_in': nrm((D_MODEL, Q_DIM + 2 * KV_DIM + IDX_HEADS * IDX_DIM + IDX_DIM + IDX_HEADS), sd),
        'dsa_q_norm': gain((HEAD_DIM,)),
        'dsa_k_norm': gain((HEAD_DIM,)),
        'dsa_w_o': nrm((Q_DIM, D_MODEL), so),
        'router_w': nrm((D_MODEL, N_EXPERTS), sd),
        'router_b': nrm((N_EXPERTS,), 0.01),
        'moe_w1': nrm((DEPTH, N_EXPERTS, D_MODEL, D_EXPERT), sd),
        'moe_w3': nrm((DEPTH, N_EXPERTS, D_MODEL, D_EXPERT), sd),
        'moe_w2': nrm((DEPTH, N_EXPERTS, D_EXPERT, D_MODEL), D_EXPERT ** -0.5),
    }


def reference(x_prompt, x_sample, cache_fox_k, cache_fox_v, cache_fox_logf, cache_moba_k, cache_moba_v,
              cache_nsa_cmp_k, cache_nsa_cmp_v, cache_nsa_slc_k, cache_nsa_slc_v, state_nsa_win_k, state_nsa_win_v,
              cache_dsa_k, cache_dsa_v, cache_dsa_idx_k, page_table, c_prompt, c_sample,
              ada_w, ada_b, norm_mix, norm_ffn,
              fox_w_in, fox_b_f, fox_q_norm, fox_k_norm, fox_w_o,
              moba_w_in, moba_q_norm, moba_k_norm, moba_w_o,
              nsa_w_in, nsa_q_norm, nsa_kc_norm, nsa_ks_norm, nsa_kw_norm, nsa_pe_k, nsa_w1_k, nsa_w2_k,
              nsa_pe_v, nsa_w1_v, nsa_w2_v, nsa_w_o,
              dsa_w_in, dsa_q_norm, dsa_k_norm, dsa_w_o,
              router_w, router_b, moe_w1, moe_w3, moe_w2):
    xp, xs = x_prompt, x_sample
    for i in range(DEPTH):
        kind = MIXERS[i % len(MIXERS)]
        p_sh1, p_sc1, p_g1, p_sh2, p_sc2, p_g2 = modulation(c_prompt, ada_w[i], ada_b[i])
        s_sh1, s_sc1, s_g1, s_sh2, s_sc2, s_g2 = modulation(c_sample, ada_w[i], ada_b[i])
        hp = adanorm(xp, norm_mix[i], p_sh1, p_sc1)
        hs = adanorm(xs, norm_mix[i], s_sh1, s_sc1)
        if kind == 'fox':
            op, fox_k_p, fox_v_p, fox_logf_p = fox_prompt(hp, fox_w_in, fox_b_f, fox_q_norm, fox_k_norm, fox_w_o)
            osm, fox_k_s, fox_v_s, fox_logf_s = fox_sample(hs, cache_fox_k, cache_fox_v, cache_fox_logf, page_table,
                                                          fox_w_in, fox_b_f, fox_q_norm, fox_k_norm, fox_w_o)
        elif kind == 'moba':
            op, moba_k_p, moba_v_p = moba_prompt(hp, moba_w_in, moba_q_norm, moba_k_norm, moba_w_o)
            osm, moba_k_s, moba_v_s = moba_sample(hs, cache_moba_k, cache_moba_v, page_table,
                                                  moba_w_in, moba_q_norm, moba_k_norm, moba_w_o)
        elif kind == 'nsa':
            (op, nsa_cmp_k_p, nsa_cmp_v_p, nsa_slc_k_p, nsa_slc_v_p, nsa_win_k_p, nsa_win_v_p) = nsa_prompt(
                hp, nsa_w_in, nsa_q_norm, nsa_kc_norm, nsa_ks_norm, nsa_kw_norm, nsa_pe_k, nsa_w1_k, nsa_w2_k,
                nsa_pe_v, nsa_w1_v, nsa_w2_v, nsa_w_o)
            (osm, nsa_cmp_k_s, nsa_cmp_v_s, nsa_slc_k_s, nsa_slc_v_s, nsa_win_k_s, nsa_win_v_s) = nsa_sample(
                hs, cache_nsa_cmp_k, cache_nsa_cmp_v, cache_nsa_slc_k, cache_nsa_slc_v, state_nsa_win_k,
                state_nsa_win_v, page_table, nsa_w_in, nsa_q_norm, nsa_kc_norm, nsa_ks_norm, nsa_kw_norm,
                nsa_pe_k, nsa_w1_k, nsa_w2_k, nsa_pe_v, nsa_w1_v, nsa_w2_v, nsa_w_o)
        else:
            op, dsa_k_p, dsa_v_p, dsa_idx_p = dsa_prompt(hp, dsa_w_in, dsa_q_norm, dsa_k_norm, dsa_w_o)
            osm, dsa_k_s, dsa_v_s, dsa_idx_s = dsa_sample(hs, cache_dsa_k, cache_dsa_v, cache_dsa_idx_k, page_table,
                                                          dsa_w_in, dsa_q_norm, dsa_k_norm, dsa_w_o)
        xp = xp + p_g1 * op
        xs = xs + s_g1 * osm
        hp = adanorm(xp, norm_ffn[i], p_sh2, p_sc2)
        hs = adanorm(xs, norm_ffn[i], s_sh2, s_sc2)
        xp = xp + p_g2 * moe(hp, router_w, router_b, moe_w1[i], moe_w3[i], moe_w2[i])
        xs = xs + s_g2 * moe(hs, router_w, router_b, moe_w1[i], moe_w3[i], moe_w2[i])
    return (xp, xs,
            fox_k_p, fox_k_s, fox_v_p, fox_v_s, fox_logf_p, fox_logf_s,
            moba_k_p, moba_k_s, moba_v_p, moba_v_s,
            nsa_cmp_k_p, nsa_cmp_k_s, nsa_cmp_v_p, nsa_cmp_v_s, nsa_slc_k_p, nsa_slc_k_s, nsa_slc_v_p, nsa_slc_v_s,
            nsa_win_k_p, nsa_win_k_s, nsa_win_v_p, nsa_win_v_s,
            dsa_k_p, dsa_k_s, dsa_v_p, dsa_v_s, dsa_idx_p, dsa_idx_s)
```

```python
import functools

import numpy as np
import jax
import jax.numpy as jnp
from jax import lax
from jax.experimental import pallas as pl
from jax.experimental.pallas import tpu as pltpu

HEAD_DIM = 64
N_KV_HEADS = 4
GROUP = 4
N_HEADS = N_KV_HEADS * GROUP
KV_DIM = N_KV_HEADS * HEAD_DIM
ROT_DIM = HEAD_DIM // 4
ROPE_THETA = 500000.0
ATTN_SCALE = HEAD_DIM ** -0.5
MOBA_BLOCK = 256
MOBA_TOPK = 3
CMP_LEN = 32
CMP_STRIDE = 16
CMP_HIDDEN = 64
SLC_BLOCK = 64
SLC_TOPK = 16
WINDOW = 512
IDX_HEADS = 8
IDX_DIM = 64
DSA_TOPK = 256
N_EXPERTS = 16
N_GROUPS = 4
EXPERTS_PER_GROUP = N_EXPERTS // N_GROUPS
TOP_K = 2
EPS = 1e-6
NEG_INF = -1e30

LANES = 128
VMEM_LIMIT = 56 * 1024 * 1024
BF16 = jnp.bfloat16
F32 = jnp.float32
INT_MIN = -2 ** 31


def _round_up(a, m):
    return (a + m - 1) // m * m


def _params(sem):
    return pltpu.CompilerParams(dimension_semantics=sem, vmem_limit_bytes=VMEM_LIMIT)


def _gather_kernel(pt_ref, *refs, pps, n_main):
    page_refs, tail_ref, o_ref = refs[:pps], refs[pps], refs[pps + 1]
    j = pl.program_id(1)

    @pl.when(j < n_main)
    def _():
        for p in range(pps):
            o_ref[p] = page_refs[p][...]

    @pl.when(j == n_main)
    def _():
        o_ref[0] = tail_ref[...]
        for p in range(1, pps):
            o_ref[p] = jnp.zeros(o_ref.shape[1:], o_ref.dtype)


def gather_pages(pool, page_table, tail):
    _, R, C = pool.shape
    B, n_pages = page_table.shape
    pps = max(d for d in (8, 4, 2, 1) if n_pages % d == 0)
    n_main = n_pages // pps

    def page_map(b, j, pt, p):
        return (pt[b, jnp.minimum(j * pps + p, n_pages - 1)], 0, 0)

    in_specs = [pl.BlockSpec((None, R, C), functools.partial(page_map, p=p)) for p in range(pps)]
    in_specs.append(pl.BlockSpec((None, R, C), lambda b, j, pt: (b, 0, 0)))
    out = pl.pallas_call(
        functools.partial(_gather_kernel, pps=pps, n_main=n_main),
        grid_spec=pltpu.PrefetchScalarGridSpec(
            num_scalar_prefetch=1, grid=(B, n_main + 1), in_specs=in_specs,
            out_specs=pl.BlockSpec((None, pps, R, C), lambda b, j, pt: (b, j, 0, 0))),
        out_shape=jax.ShapeDtypeStruct((B, (n_main + 1) * pps, R, C), pool.dtype),
        compiler_params=_params(("arbitrary", "arbitrary")),
    )(page_table, *([pool] * pps), tail)
    return out.reshape(B, (n_main + 1) * pps * R, C)


def gather_rows(pool, page_table, new_rows):
    P, page = pool.shape[:2]
    C = int(np.prod(pool.shape[2:]))
    B, T = new_rows.shape[:2]
    fold = KV_DIM // C if C < KV_DIM else 1
    tail = jnp.pad(new_rows.reshape(B, T, C), ((0, 0), (0, page - T), (0, 0)))
    g = gather_pages(pool.reshape(P, page // fold, C * fold), page_table,
                     tail.reshape(B, page // fold, C * fold))
    return g.reshape(B, -1, C)


def _adanorm(x, g, sc, sh):
    y = x * lax.rsqrt(jnp.mean(x * x, axis=-1, keepdims=True) + EPS)
    return (y * g) * (1.0 + sc) + sh


def _matmul_kernel(*refs, has_norm, has_epi):
    it = iter(refs)
    x_ref, w_ref = next(it), next(it)
    x = x_ref[...]
    if has_norm:
        g_ref, sc_ref, sh_ref = next(it), next(it), next(it)
        x = _adanorm(x, g_ref[...], sc_ref[...], sh_ref[...])
    acc = jnp.dot(x.astype(BF16), w_ref[...], preferred_element_type=F32)
    if has_epi:
        res_ref, gate_ref = next(it), next(it)
        acc = res_ref[...] + gate_ref[...] * acc
    o_ref = next(it)
    o_ref[...] = acc


def _mod_spec(mod, tm, seq_len):
    D = mod.shape[-1]
    if seq_len % tm == 0:
        tps = seq_len // tm
        return mod, pl.BlockSpec((None, 1, D), lambda i: (i // tps, 0, 0))
    rows = jnp.broadcast_to(mod, (mod.shape[0], seq_len, D)).reshape(-1, D)
    return rows, pl.BlockSpec((tm, D), lambda i: (i, 0))


def fused_matmul(x, w, *, seq_len, tm, norm=None, epi=None):
    N, K = x.shape
    M = w.shape[1]
    tm = min(tm, N)
    args = [x, w]
    specs = [pl.BlockSpec((tm, K), lambda i: (i, 0)), pl.BlockSpec((K, M), lambda i: (0, 0))]
    if norm is not None:
        gain, sc, sh = norm
        args.append(gain)
        specs.append(pl.BlockSpec((1, K), lambda i: (0, 0)))
        for m in (sc, sh):
            a, s = _mod_spec(m, tm, seq_len)
            args.append(a)
            specs.append(s)
    if epi is not None:
        res, gate = epi
        args.append(res)
        specs.append(pl.BlockSpec((tm, M), lambda i: (i, 0)))
        a, s = _mod_spec(gate, tm, seq_len)
        args.append(a)
        specs.append(s)
    return pl.pallas_call(
        functools.partial(_matmul_kernel, has_norm=norm is not None, has_epi=epi is not None),
        grid=(N // tm,), in_specs=specs,
        out_specs=pl.BlockSpec((tm, M), lambda i: (i, 0)),
        out_shape=jax.ShapeDtypeStruct((N, M), F32),
        compiler_params=_params(("arbitrary",)),
    )(*args)


def _cumsum_kernel(x_ref, o_ref, carry_ref, *, tc):
    @pl.when(pl.program_id(1) == 0)
    def _():
        carry_ref[...] = jnp.zeros_like(carry_ref)

    r = lax.broadcasted_iota(jnp.int32, (tc, tc), 0)
    c = lax.broadcasted_iota(jnp.int32, (tc, tc), 1)
    tri = (c <= r).astype(F32)
    out = jnp.dot(tri, x_ref[...], preferred_element_type=F32,
                  precision=lax.Precision.HIGHEST) + carry_ref[...]
    o_ref[...] = out
    carry_ref[...] = out[tc - 1:tc, :]


def cumsum_rows(x, tc=256):
    B, L, H = x.shape
    tc = min(tc, L)
    return pl.pallas_call(
        functools.partial(_cumsum_kernel, tc=tc),
        grid=(B, L // tc),
        in_specs=[pl.BlockSpec((None, tc, H), lambda b, j: (b, j, 0))],
        out_specs=pl.BlockSpec((None, tc, H), lambda b, j: (b, j, 0)),
        out_shape=jax.ShapeDtypeStruct((B, L, H), F32),
        scratch_shapes=[pltpu.VMEM((1, H), F32)],
        compiler_params=_params(("arbitrary", "arbitrary")),
    )(x)


def _select_blocks(score, own, n_top):
    col = lax.broadcasted_iota(jnp.int32, score.shape, 1)
    colf = col.astype(F32)
    sc = jnp.where(col < own, score, -jnp.inf)
    sel = col == own
    for _ in range(n_top):
        m = jnp.max(sc, axis=-1, keepdims=True)
        idx = jnp.min(jnp.where(sc == m, colf, 1e9), axis=-1, keepdims=True)
        hit = colf == idx
        sel = sel | (hit & (m > -jnp.inf))
        sc = jnp.where(hit, -jnp.inf, sc)
    return sel


def _row_pos(tq, rows, base):
    assert tq & (tq - 1) == 0
    r = lax.broadcasted_iota(jnp.int32, (rows, 1), 0)
    return base + (r & (tq - 1))


def _moba_select_kernel(q_ref, kb_ref, o_ref, *, tq, qpos_base):
    qpos = _row_pos(tq, tq, qpos_base + pl.program_id(1) * tq)
    own = lax.shift_right_logical(qpos, int(np.log2(MOBA_BLOCK)))
    for h in range(N_KV_HEADS):
        gate = jnp.dot(q_ref[:, h * KV_DIM:(h + 1) * KV_DIM], kb_ref[h], preferred_element_type=F32)
        o_ref[h] = _select_blocks(gate, own, MOBA_TOPK).astype(F32)


def moba_select(q_nat, kbt, *, tq, qpos_base):
    B, T, D = q_nat.shape
    NBP = kbt.shape[-1]
    return pl.pallas_call(
        functools.partial(_moba_select_kernel, tq=tq, qpos_base=qpos_base),
        grid=(B, T // tq),
        in_specs=[pl.BlockSpec((None, tq, D), lambda b, i: (b, i, 0)),
                  pl.BlockSpec((None, N_KV_HEADS, KV_DIM, NBP), lambda b, i: (b, 0, 0, 0))],
        out_specs=pl.BlockSpec((None, N_KV_HEADS, tq, NBP), lambda b, i: (b, 0, i, 0)),
        out_shape=jax.ShapeDtypeStruct((B, N_KV_HEADS, T, NBP), F32),
        compiler_params=_params(("arbitrary", "arbitrary")),
    )(q_nat, kbt)


def _block_mean_kernel(k_ref, o_ref, *, nb, blk):
    x = k_ref[...].reshape(nb, blk, KV_DIM)
    o_ref[...] = jnp.sum(x, axis=1) * (1.0 / blk)


def block_means(k, blk):
    B, L, C = k.shape
    NB = L // blk
    nb = max(d for d in (8, 4, 2, 1) if NB % d == 0)
    if NB // nb > 1 and nb % 8:
        nb = NB
    return pl.pallas_call(
        functools.partial(_block_mean_kernel, nb=nb, blk=blk),
        grid=(B, NB // nb),
        in_specs=[pl.BlockSpec((None, nb * blk, C), lambda b, j: (b, j, 0))],
        out_specs=pl.BlockSpec((None, nb, C), lambda b, j: (b, j, 0)),
        out_shape=jax.ShapeDtypeStruct((B, NB, C), F32),
        compiler_params=_params(("arbitrary", "arbitrary")),
    )(k)


def _compress_lohi_kernel(x_ref, pe_lo_ref, pe_hi_ref, wlo_ref, whi_ref, lo_ref, hi_ref, *, nblk):
    lo = jnp.zeros((nblk, KV_DIM), F32)
    hi = jnp.zeros((nblk, KV_DIM), F32)
    for t in range(CMP_STRIDE):
        x = jnp.concatenate([x_ref[pl.ds(2 * t, nblk, stride=2 * CMP_STRIDE), :],
                             x_ref[pl.ds(2 * t + 1, nblk, stride=2 * CMP_STRIDE), :]], axis=-1)
        lo += jnp.dot((x + pe_lo_ref[t:t + 1, :]).astype(BF16), wlo_ref[t], preferred_element_type=F32)
        hi += jnp.dot((x + pe_hi_ref[t:t + 1, :]).astype(BF16), whi_ref[t], preferred_element_type=F32)
    lo_ref[...] = lo
    hi_ref[...] = hi


def compress_lohi(raw, pe, w1, tr=2048):
    N = raw.shape[0]
    tr = max(d for d in (tr, tr // 2, tr // 4, tr // 8, LANES) if N % d == 0)
    nblk = tr // CMP_STRIDE
    eye = jnp.eye(N_KV_HEADS, dtype=F32)

    def blockdiag(w):
        w = w.reshape(CMP_STRIDE, HEAD_DIM, CMP_HIDDEN)
        return jnp.einsum('tdc,hk->thdkc', w, eye).reshape(CMP_STRIDE, KV_DIM, KV_DIM).astype(BF16)

    half = CMP_STRIDE * HEAD_DIM
    pe_lo = jnp.tile(pe[:CMP_STRIDE], (1, N_KV_HEADS))
    pe_hi = jnp.tile(pe[CMP_STRIDE:], (1, N_KV_HEADS))
    const = lambda i: (0, 0)
    const3 = lambda i: (0, 0, 0)
    return pl.pallas_call(
        functools.partial(_compress_lohi_kernel, nblk=nblk),
        grid=(N // tr,),
        in_specs=[pl.BlockSpec((2 * tr, LANES), lambda i: (i, 0)),
                  pl.BlockSpec((CMP_STRIDE, KV_DIM), const), pl.BlockSpec((CMP_STRIDE, KV_DIM), const),
                  pl.BlockSpec((CMP_STRIDE, KV_DIM, KV_DIM), const3),
                  pl.BlockSpec((CMP_STRIDE, KV_DIM, KV_DIM), const3)],
        out_specs=[pl.BlockSpec((nblk, KV_DIM), lambda i: (i, 0))] * 2,
        out_shape=[jax.ShapeDtypeStruct((N // CMP_STRIDE, KV_DIM), F32)] * 2,
        compiler_params=_params(("arbitrary",)),
    )(raw.reshape(2 * N, LANES), pe_lo, pe_hi, blockdiag(w1[:half]), blockdiag(w1[half:]))


def _compress_out_kernel(lo_ref, hi_ref, w_ref, o_ref):
    z = lo_ref[...] + hi_ref[...]
    z = z * jax.nn.sigmoid(z)
    o_ref[...] = jnp.dot(z.astype(BF16), w_ref[...], preferred_element_type=F32)


def compress_out(lo, hi_next, w2):
    N = lo.shape[0]
    tm = max(d for d in (512, 256, 128, 64, 32, 16, 8) if N % d == 0)
    w = jnp.einsum('cd,hk->hckd', w2, jnp.eye(N_KV_HEADS, dtype=F32)).reshape(KV_DIM, KV_DIM).astype(BF16)
    return pl.pallas_call(
        _compress_out_kernel, grid=(N // tm,),
        in_specs=[pl.BlockSpec((tm, KV_DIM), lambda i: (i, 0)), pl.BlockSpec((tm, KV_DIM), lambda i: (i, 0)),
                  pl.BlockSpec((KV_DIM, KV_DIM), lambda i: (0, 0))],
        out_specs=pl.BlockSpec((tm, KV_DIM), lambda i: (i, 0)),
        out_shape=jax.ShapeDtypeStruct((N, KV_DIM), F32),
        compiler_params=_params(("arbitrary",)),
    )(lo, hi_next, w)


def _lane_head_mask(rows, h):
    lane = lax.broadcasted_iota(jnp.int32, (rows, KV_DIM), 1)
    return (lane >= h * HEAD_DIM) & (lane < (h + 1) * HEAD_DIM)


def _nsa_cmp_kernel(q_ref, kc_ref, vc_ref, cover_ref, o_ref, sel_ref, *, tq, qpos_base, nc):
    R = GROUP * tq
    NCP = kc_ref.shape[0]
    q0 = qpos_base + pl.program_id(1) * tq
    qpos = _row_pos(tq, R, q0)
    c = lax.broadcasted_iota(jnp.int32, (1, NCP), 1)
    mask = (c * CMP_STRIDE + (CMP_LEN - 1) <= qpos) & (c < nc)
    kc = kc_ref[...]
    vc = vc_ref[...]
    own = lax.shift_right_logical(_row_pos(tq, tq, q0), int(np.log2(SLC_BLOCK)))
    out = jnp.zeros((R, KV_DIM), F32)
    for h in range(N_KV_HEADS):
        s = lax.dot_general(q_ref[h], kc, (((1,), (1,)), ((), ())), preferred_element_type=F32)
        s = jnp.where(mask, s, NEG_INF)
        m = jnp.max(s, axis=-1, keepdims=True)
        e = jnp.where(mask, jnp.exp(s - m), 0.0)
        l = jnp.sum(e, axis=-1, keepdims=True)
        p = (e / jnp.where(l > 0.0, l, 1.0)).astype(BF16)
        out = jnp.where(_lane_head_mask(R, h), jnp.dot(p, vc, preferred_element_type=F32), out)
        imp = jnp.dot(p, cover_ref[...], preferred_element_type=F32)
        imp = jnp.sum(imp.reshape(GROUP, tq, imp.shape[-1]), axis=0)
        sel_ref[h] = _select_blocks(imp, own, SLC_TOPK - 1).astype(F32)
    o_ref[...] = out


def nsa_compressed(qpad, kcc, vcc, cover, *, tq, qpos_base, nc):
    B, nQ, _, R, _ = qpad.shape
    NCP, NBP = cover.shape
    return pl.pallas_call(
        functools.partial(_nsa_cmp_kernel, tq=tq, qpos_base=qpos_base, nc=nc),
        grid=(B, nQ),
        in_specs=[pl.BlockSpec((None, None, N_KV_HEADS, R, KV_DIM), lambda b, i: (b, i, 0, 0, 0)),
                  pl.BlockSpec((None, NCP, KV_DIM), lambda b, i: (b, 0, 0)),
                  pl.BlockSpec((None, NCP, KV_DIM), lambda b, i: (b, 0, 0)),
                  pl.BlockSpec((NCP, NBP), lambda b, i: (0, 0))],
        out_specs=[pl.BlockSpec((None, None, R, KV_DIM), lambda b, i: (b, i, 0, 0)),
                   pl.BlockSpec((None, N_KV_HEADS, tq, NBP), lambda b, i: (b, 0, i, 0))],
        out_shape=[jax.ShapeDtypeStruct((B, nQ, R, KV_DIM), F32),
                   jax.ShapeDtypeStruct((B, N_KV_HEADS, nQ * tq, NBP), F32)],
        compiler_params=_params(("arbitrary", "arbitrary")),
    )(qpad, kcc, vcc, cover)


def _index_keys(qi, wi, ki, kpos, qpos, tq):
    rel = jnp.maximum(lax.dot_general(qi, ki, (((1,), (1,)), ((), ())), preferred_element_type=F32), 0.0)
    score = jnp.zeros((tq, rel.shape[1]), F32)
    for i in range(IDX_HEADS):
        score = score + wi[:, i:i + 1] * rel[i * tq:(i + 1) * tq]
    score = jnp.where(kpos <= qpos, score + 0.0, -jnp.inf)
    bits = lax.bitcast_convert_type(score, jnp.int32)
    return bits ^ (lax.shift_right_arithmetic(bits, 31) & 0x7FFFFFFF)


def _dsa_thresh_kernel(qi_ref, wi_ref, ki_ref, thr_ref, need_ref, key_ref, *, tq, kt, qpos_base, n_sel, nkt):
    q0 = qpos_base + pl.program_id(1) * tq
    qpos = _row_pos(tq, tq, q0)
    last = jnp.minimum((q0 + tq - 1) // kt, nkt - 1)
    qi = qi_ref[...]
    wi = wi_ref[...]
    lane = lax.broadcasted_iota(jnp.int32, (1, kt), 1)

    def fill(jt, carry):
        ki = ki_ref[pl.ds(pl.multiple_of(jt * kt, kt), kt), :].astype(BF16)
        key_ref[jt] = _index_keys(qi, wi, ki, jt * kt + lane, qpos, tq)
        return carry

    lax.fori_loop(0, last + 1, fill, 0)

    def count_ge(t):
        def body(jt, acc):
            return acc + jnp.sum((key_ref[jt] >= t).astype(F32), axis=-1, keepdims=True)
        return lax.fori_loop(0, last + 1, body, jnp.zeros((tq, 1), F32))

    def bit_step(it, t_u):
        cand = t_u | lax.shift_left(jnp.int32(1), 31 - it)
        ok = count_ge(cand ^ INT_MIN) >= n_sel
        return jnp.where(ok, cand, t_u)

    t_u = lax.fori_loop(0, 32, bit_step, jnp.zeros((tq, 1), jnp.int32))
    thr = t_u ^ INT_MIN

    def body_gt(jt, acc):
        return acc + jnp.sum((key_ref[jt] > thr).astype(F32), axis=-1, keepdims=True)

    n_gt = lax.fori_loop(0, last + 1, body_gt, jnp.zeros((tq, 1), F32))
    thr_ref[...] = thr
    need_ref[...] = n_sel - n_gt


def dsa_threshold(qi, wi, ki, *, tq, kt, qpos_base, n_sel):
    B, nQ = qi.shape[:2]
    Lk = ki.shape[1]
    nkt = Lk // kt
    return pl.pallas_call(
        functools.partial(_dsa_thresh_kernel, tq=tq, kt=kt, qpos_base=qpos_base, n_sel=n_sel, nkt=nkt),
        grid=(B, nQ),
        in_specs=[pl.BlockSpec((None, None, IDX_HEADS * tq, IDX_DIM), lambda b, i: (b, i, 0, 0)),
                  pl.BlockSpec((None, None, tq, IDX_HEADS), lambda b, i: (b, i, 0, 0)),
                  pl.BlockSpec((None, Lk, IDX_DIM), lambda b, i: (b, 0, 0))],
        out_specs=[pl.BlockSpec((None, None, tq, 1), lambda b, i: (b, i, 0, 0))] * 2,
        out_shape=[jax.ShapeDtypeStruct((B, nQ, tq, 1), jnp.int32),
                   jax.ShapeDtypeStruct((B, nQ, tq, 1), F32)],
        scratch_shapes=[pltpu.VMEM((nkt, tq, kt), jnp.int32)],
        compiler_params=_params(("arbitrary", "arbitrary")),
    )(qi, wi, ki)


def _flash_kernel(*refs, mode, tq, kt, nk, qpos_base, kpos_base, window, blk, kfirst, klast):
    it = iter(refs)
    q_ref, k_ref, v_ref = next(it), next(it), next(it)
    if mode == 'fox':
        fq_ref, fk_ref = next(it), next(it)
    elif mode == 'blocks':
        sel_ref = next(it)
    elif mode == 'dsa':
        qi_ref, wi_ref, ki_ref, thr_ref, need_ref = (next(it) for _ in range(5))
    o_ref, m_ref, l_ref, acc_ref = next(it), next(it), next(it), next(it)
    if mode == 'dsa':
        eq_ref = next(it)

    R = GROUP * tq
    i, j = pl.program_id(1), pl.program_id(2)
    q0 = qpos_base + i * tq
    jt = kfirst(i) + j

    @pl.when(j == 0)
    def _():
        m_ref[...] = jnp.full(m_ref.shape, NEG_INF, F32)
        l_ref[...] = jnp.zeros(l_ref.shape, F32)
        acc_ref[...] = jnp.zeros(acc_ref.shape, F32)
        if mode == 'dsa':
            eq_ref[...] = jnp.zeros(eq_ref.shape, F32)

    @pl.when(jt <= klast(i))
    def _():
        kb = k_ref[...].astype(BF16)
        vb = v_ref[...].astype(BF16)
        kpos = kpos_base + jt * kt + lax.broadcasted_iota(jnp.int32, (1, kt), 1)
        qpos = _row_pos(tq, tq, q0)
        mask_t = kpos <= qpos
        if window is not None:
            mask_t = mask_t & (qpos - kpos < window) & (kpos >= 0)
        if mode == 'dsa':
            key = _index_keys(qi_ref[...], wi_ref[...], ki_ref[...].astype(BF16), kpos, qpos, tq)
            thr = thr_ref[...]
            eq = key == thr
            r = lax.broadcasted_iota(jnp.int32, (kt, kt), 0)
            c = lax.broadcasted_iota(jnp.int32, (kt, kt), 1)
            rank = eq_ref[...] + jnp.dot(eq.astype(F32), (r <= c).astype(F32), preferred_element_type=F32)
            eq_ref[...] = rank[:, kt - 1:kt]
            mask_t = mask_t & ((key > thr) | (eq & (rank <= need_ref[...])))
        for h in range(N_KV_HEADS):
            mask = mask_t
            if mode == 'blocks':
                nbp = sel_ref.shape[-1]
                kblk = lax.shift_right_logical(
                    kpos_base + jt * kt + lax.broadcasted_iota(jnp.int32, (nbp, kt), 1), int(np.log2(blk)))
                expand = (kblk == lax.broadcasted_iota(jnp.int32, (nbp, kt), 0)).astype(F32)
                mask = mask & (jnp.dot(sel_ref[h], expand, preferred_element_type=F32) > 0.5)
            s = lax.dot_general(q_ref[h], kb, (((1,), (1,)), ((), ())), preferred_element_type=F32)
            s = s.reshape(GROUP, tq, kt)
            if mode == 'fox':
                s = s + fq_ref[h].reshape(GROUP, tq, 1) - fk_ref[h * GROUP:(h + 1) * GROUP, :][:, None, :]
            s = jnp.where(mask[None], s, NEG_INF).reshape(R, kt)
            m_prev = m_ref[h]
            m_new = jnp.maximum(m_prev, jnp.max(s, axis=-1, keepdims=True))
            alpha = jnp.exp(m_prev - m_new)
            p = jnp.where(mask[None], jnp.exp(s - m_new).reshape(GROUP, tq, kt), 0.0).reshape(R, kt)
            l_ref[h] = alpha * l_ref[h] + jnp.sum(p, axis=-1, keepdims=True)
            acc_ref[h] = alpha * acc_ref[h] + jnp.dot(p.astype(BF16), vb, preferred_element_type=F32)
            m_ref[h] = m_new

    @pl.when(j == nk - 1)
    def _():
        out = jnp.zeros((R, KV_DIM), F32)
        for h in range(N_KV_HEADS):
            l = l_ref[h]
            out = jnp.where(_lane_head_mask(R, h), acc_ref[h] / jnp.where(l > 0.0, l, 1.0), out)
        o_ref[...] = out


def flash_attention(qpad, k, v, *, tq, kt, qpos_base, kpos_base=0, kv_len, mode='plain', window=None,
                    fq=None, fkT=None, sel=None, blk=None, dsa=None):
    B, nQ, _, R, _ = qpad.shape
    Lk = k.shape[1]
    nkt = Lk // kt
    last_tile = (kv_len - 1) // kt

    def klast(i):
        return jnp.minimum((qpos_base + (i + 1) * tq - 1 - kpos_base) // kt, last_tile)

    def kfirst(i):
        if window is None:
            return 0
        return jnp.maximum(qpos_base + i * tq - (window - 1) - kpos_base, 0) // kt

    def span(i):
        lo = 0 if window is None else max(qpos_base + i * tq - (window - 1) - kpos_base, 0) // kt
        return min((qpos_base + (i + 1) * tq - 1 - kpos_base) // kt, last_tile) - lo + 1

    nk = max(span(i) for i in range(nQ))

    def kv_map(b, i, j):
        return (b, jnp.minimum(kfirst(i) + j, klast(i)), 0)

    args = [qpad, k, v]
    specs = [pl.BlockSpec((None, None, N_KV_HEADS, R, KV_DIM), lambda b, i, j: (b, i, 0, 0, 0)),
             pl.BlockSpec((None, kt, KV_DIM), kv_map), pl.BlockSpec((None, kt, KV_DIM), kv_map)]
    scratch = [pltpu.VMEM((N_KV_HEADS, R, 1), F32), pltpu.VMEM((N_KV_HEADS, R, 1), F32),
               pltpu.VMEM((N_KV_HEADS, R, KV_DIM), F32)]
    if mode == 'fox':
        args += [fq, fkT]
        specs += [pl.BlockSpec((None, None, N_KV_HEADS, R, 1), lambda b, i, j: (b, i, 0, 0, 0)),
                  pl.BlockSpec((None, N_HEADS, kt), lambda b, i, j: kv_map(b, i, j)[:1] + (0, kv_map(b, i, j)[1]))]
    elif mode == 'blocks':
        NBP = sel.shape[-1]
        args += [sel]
        specs += [pl.BlockSpec((None, N_KV_HEADS, tq, NBP), lambda b, i, j: (b, 0, i, 0))]
    elif mode == 'dsa':
        qi, wi, ki, thr, need = dsa
        args += [qi, wi, ki, thr, need]
        specs += [pl.BlockSpec((None, None, IDX_HEADS * tq, IDX_DIM), lambda b, i, j: (b, i, 0, 0)),
                  pl.BlockSpec((None, None, tq, IDX_HEADS), lambda b, i, j: (b, i, 0, 0)),
                  pl.BlockSpec((None, kt, IDX_DIM), kv_map),
                  pl.BlockSpec((None, None, tq, 1), lambda b, i, j: (b, i, 0, 0)),
                  pl.BlockSpec((None, None, tq, 1), lambda b, i, j: (b, i, 0, 0))]
        scratch.append(pltpu.VMEM((tq, 1), F32))
    return pl.pallas_call(
        functools.partial(_flash_kernel, mode=mode, tq=tq, kt=kt, nk=nk, qpos_base=qpos_base,
                          kpos_base=kpos_base, window=window, blk=blk, kfirst=kfirst, klast=klast),
        grid=(B, nQ, nk), in_specs=specs,
        out_specs=pl.BlockSpec((None, None, R, KV_DIM), lambda b, i, j: (b, i, 0, 0)),
        out_shape=jax.ShapeDtypeStruct((B, nQ, R, KV_DIM), F32),
        scratch_shapes=scratch,
        compiler_params=_params(("arbitrary", "arbitrary", "arbitrary")),
    )(*args)


def _moe_kernel(x_ref, g_ref, sc_ref, sh_ref, gate_ref, c_ref, w1_ref, w3_ref, w2_ref, o_ref, h_ref, acc_ref):
    e = pl.program_id(1)

    @pl.when(e == 0)
    def _():
        h_ref[...] = _adanorm(x_ref[...], g_ref[...], sc_ref[...], sh_ref[...]).astype(BF16)
        acc_ref[...] = jnp.zeros_like(acc_ref)

    h = h_ref[...]
    a = jnp.dot(h, w1_ref[...], preferred_element_type=F32)
    a = (a * jax.nn.sigmoid(a)) * jnp.dot(h, w3_ref[...], preferred_element_type=F32)
    y = jnp.dot(a.astype(BF16), w2_ref[...], preferred_element_type=F32)
    c = c_ref[...]
    lane = lax.broadcasted_iota(jnp.int32, c.shape, 1)
    ce = jnp.sum(jnp.where(lane == e, c, 0.0), axis=-1, keepdims=True)
    acc_ref[...] += jnp.where(ce != 0.0, ce * y, 0.0)

    @pl.when(e == pl.num_programs(1) - 1)
    def _():
        o_ref[...] = x_ref[...] + gate_ref[...] * acc_ref[...]


def moe_dense(x, coef, gain, sc, sh, gate, w1, w3, w2, *, seq_len, tm):
    N, D = x.shape
    E, _, DE = w1.shape
    tm = min(tm, N)
    row = lambda i, e: (i, 0)
    specs = [pl.BlockSpec((tm, D), row), pl.BlockSpec((1, D), lambda i, e: (0, 0))]
    args = [x, gain]
    for m in (sc, sh, gate):
        a, s = _mod_spec(m, tm, seq_len)
        args.append(a)
        imap = s.index_map
        specs.append(pl.BlockSpec(s.block_shape, lambda i, e, imap=imap: imap(i)))
    args += [coef, w1, w3, w2]
    specs += [pl.BlockSpec((tm, E), row),
              pl.BlockSpec((None, D, DE), lambda i, e: (e, 0, 0)),
              pl.BlockSpec((None, D, DE), lambda i, e: (e, 0, 0)),
              pl.BlockSpec((None, DE, D), lambda i, e: (e, 0, 0))]
    return pl.pallas_call(
        _moe_kernel, grid=(N // tm, E), in_specs=specs,
        out_specs=pl.BlockSpec((tm, D), row),
        out_shape=jax.ShapeDtypeStruct((N, D), F32),
        scratch_shapes=[pltpu.VMEM((tm, D), BF16), pltpu.VMEM((tm, D), F32)],
        compiler_params=_params(("arbitrary", "arbitrary")),
    )(*args)


def _head_norm(x, g):
    return x * lax.rsqrt(jnp.mean(x * x, axis=-1, keepdims=True) + EPS) * g


def _rope(x, pos):
    half = ROT_DIM // 2
    inv = ROPE_THETA ** (-jnp.arange(half, dtype=F32) / half)
    ang = pos.astype(F32)[:, None] * inv[None, :]
    cos, sin = jnp.cos(ang)[:, None, :], jnp.sin(ang)[:, None, :]
    x1, x2 = x[..., :half], x[..., half:ROT_DIM]
    return jnp.concatenate([x1 * cos - x2 * sin, x2 * cos + x1 * sin, x[..., ROT_DIM:]], axis=-1)


def _split(y, sizes):
    return jnp.split(y[..., :sum(sizes)], np.cumsum(sizes)[:-1].tolist(), axis=-1)


def _pad_cols(w, mult=256):
    return jnp.pad(w, ((0, 0), (0, _round_up(w.shape[1], mult) - w.shape[1])))


def _qpad(q, tq):
    B, T = q.shape[:2]
    nQ = T // tq
    q = (q * ATTN_SCALE).astype(BF16).reshape(B, nQ, tq, N_KV_HEADS, GROUP, HEAD_DIM)
    q = q.transpose(0, 1, 3, 4, 2, 5)
    eye = jnp.eye(N_KV_HEADS, dtype=BF16)
    q = q[:, :, :, :, :, None, :] * eye[None, None, :, None, None, :, None]
    return q.reshape(B, nQ, N_KV_HEADS, GROUP * tq, KV_DIM)


def _unpad_out(o, tq):
    B, nQ = o.shape[:2]
    o = o.reshape(B, nQ, GROUP, tq, N_KV_HEADS, HEAD_DIM).transpose(0, 1, 3, 4, 2, 5)
    return o.reshape(B, nQ * tq, N_HEADS, HEAD_DIM)


def _rows_gt(a, tq):
    B, T, _ = a.shape
    a = a.reshape(B, T // tq, tq, N_KV_HEADS, GROUP).transpose(0, 1, 3, 4, 2)
    return a.reshape(B, T // tq, N_KV_HEADS, GROUP * tq, 1)


def _pad_len(a, L):
    return jnp.pad(a, ((0, 0), (0, L - a.shape[1])) + ((0, 0),) * (a.ndim - 2))


def _modulation(c, ada_w, ada_b):
    m = jax.nn.silu(c) @ ada_w + ada_b
    return [a[:, None, :] for a in jnp.split(m, 6, axis=-1)]


def _routing(logits, router_b):
    n = logits.shape[0]
    aff = jax.nn.sigmoid(logits)
    grp = (aff + router_b).reshape(n, N_GROUPS, EXPERTS_PER_GROUP)
    g = jnp.argmax(lax.top_k(grp, TOP_K)[0].sum(-1), axis=-1)
    rows = jnp.arange(n)
    _, loc = lax.top_k(grp[rows, g], TOP_K)
    eidx = g[:, None] * EXPERTS_PER_GROUP + loc
    wsel = aff[rows[:, None], eidx]
    wsel = wsel / jnp.sum(wsel, axis=-1, keepdims=True)
    return jnp.sum(jax.nn.one_hot(eidx, N_EXPERTS, dtype=F32) * wsel[..., None], axis=1)


class _Stream:
    def __init__(self, B, T, past):
        self.B, self.T, self.past = B, T, past
        self.tq = min(T, 128)
        self.tm = min(B * T, 512)
        self.pos = past + jnp.arange(T)


def _attend(st, q, k, v, kv_len, kt, **kw):
    o = flash_attention(_qpad(q, st.tq), k, v, tq=st.tq, kt=kt, qpos_base=st.past, kv_len=kv_len, **kw)
    return o


def _kv_tile(L):
    return max(d for d in (512, 256, 128) if L % d == 0)


def kernel(x_prompt, x_sample, cache_fox_k, cache_fox_v, cache_fox_logf, cache_moba_k, cache_moba_v,
           cache_nsa_cmp_k, cache_nsa_cmp_v, cache_nsa_slc_k, cache_nsa_slc_v, state_nsa_win_k, state_nsa_win_v,
           cache_dsa_k, cache_dsa_v, cache_dsa_idx_k, page_table, c_prompt, c_sample,
           ada_w, ada_b, norm_mix, norm_ffn,
           fox_w_in, fox_b_f, fox_q_norm, fox_k_norm, fox_w_o,
           moba_w_in, moba_q_norm, moba_k_norm, moba_w_o,
           nsa_w_in, nsa_q_norm, nsa_kc_norm, nsa_ks_norm, nsa_kw_norm, nsa_pe_k, nsa_w1_k, nsa_w2_k,
           nsa_pe_v, nsa_w1_v, nsa_w2_v, nsa_w_o,
           dsa_w_in, dsa_q_norm, dsa_k_norm, dsa_w_o,
           router_w, router_b, moe_w1, moe_w3, moe_w2):
    D = x_prompt.shape[-1]
    page = cache_fox_k.shape[1]
    past = page_table.shape[1] * page
    streams = (_Stream(x_prompt.shape[0], x_prompt.shape[1], 0),
               _Stream(x_sample.shape[0], x_sample.shape[1], past))
    conds = (c_prompt, c_sample)
    xs = [x_prompt.reshape(-1, D), x_sample.reshape(-1, D)]
    outs = [dict(), dict()]

    w_in = {'fox': fox_w_in, 'moba': moba_w_in, 'nsa': nsa_w_in, 'dsa': dsa_w_in}
    w_out = {'fox': fox_w_o, 'moba': moba_w_o, 'nsa': nsa_w_o, 'dsa': dsa_w_o}
    w_in = {n: _pad_cols(w).astype(BF16) for n, w in w_in.items()}
    w_out = {n: w.astype(BF16) for n, w in w_out.items()}
    w_router = _pad_cols(router_w, LANES).astype(BF16)
    moe_w1b, moe_w3b, moe_w2b = moe_w1.astype(BF16), moe_w3.astype(BF16), moe_w2.astype(BF16)

    def keys_of(st, new, cache):
        B, T = new.shape[:2]
        if st.past == 0:
            return new.reshape(B, T, -1), T
        return gather_rows(cache, page_table, new), st.past + T

    def fox(st, y, o):
        B, T = st.B, st.T
        q, k, v, fl = _split(y, [N_HEADS * HEAD_DIM, KV_DIM, KV_DIM, N_HEADS])
        q = _head_norm(q.reshape(B, T, N_HEADS, HEAD_DIM), fox_q_norm)
        k = _head_norm(k.reshape(B, T, N_KV_HEADS, HEAD_DIM), fox_k_norm)
        v = v.reshape(B, T, N_KV_HEADS, HEAD_DIM)
        logf = jax.nn.log_sigmoid(fl.reshape(B, T, N_HEADS) + fox_b_f)
        o.update(fox_k=k, fox_v=v, fox_logf=logf)
        K, L = keys_of(st, k, cache_fox_k)
        V, _ = keys_of(st, v, cache_fox_v)
        LF, _ = keys_of(st, logf, cache_fox_logf)
        F = cumsum_rows(LF)
        fq = _rows_gt(lax.slice_in_dim(F, st.past, st.past + T, axis=1), st.tq)
        return _unpad_out(_attend(st, q, K, V, L, _kv_tile(K.shape[1]), mode='fox', fq=fq,
                                  fkT=F.transpose(0, 2, 1)), st.tq)

    def moba(st, y, o):
        B, T = st.B, st.T
        q, k, v = _split(y, [N_HEADS * HEAD_DIM, KV_DIM, KV_DIM])
        q = _rope(_head_norm(q.reshape(B, T, N_HEADS, HEAD_DIM), moba_q_norm), st.pos)
        k = _rope(_head_norm(k.reshape(B, T, N_KV_HEADS, HEAD_DIM), moba_k_norm), st.pos)
        v = v.reshape(B, T, N_KV_HEADS, HEAD_DIM)
        o.update(moba_k=k, moba_v=v)
        K, L = keys_of(st, k, cache_moba_k)
        V, _ = keys_of(st, v, cache_moba_v)
        kbar = block_means(K, MOBA_BLOCK)
        NB = kbar.shape[1]
        NBP = _round_up(NB, LANES)
        kbt = jnp.tile(kbar.reshape(B, NB, N_KV_HEADS, 1, HEAD_DIM), (1, 1, 1, GROUP, 1))
        kbt = kbt.reshape(B, NB, N_KV_HEADS, KV_DIM).transpose(0, 2, 3, 1)
        kbt = jnp.pad(kbt, ((0, 0), (0, 0), (0, 0), (0, NBP - NB)))
        sel = moba_select(q.reshape(B, T, -1), kbt, tq=st.tq, qpos_base=st.past)
        return _unpad_out(_attend(st, q, K, V, L, _kv_tile(K.shape[1]), mode='blocks', sel=sel,
                                  blk=MOBA_BLOCK), st.tq)

    def compress(st, raw_new, cache, pe, w1, w2):
        B, T = st.B, st.T
        if st.past == 0:
            lo, hi = compress_lohi(raw_new.reshape(B * T, KV_DIM), pe, w1)
            nblk = T // CMP_STRIDE
            lo, hi = lo.reshape(B, nblk, KV_DIM), hi.reshape(B, nblk, KV_DIM)
            L = T
        else:
            P = cache.shape[0]
            lo_p, hi_p = compress_lohi(cache.reshape(P * page, KV_DIM), pe, w1)
            tail = jnp.pad(raw_new.reshape(B, T, KV_DIM), ((0, 0), (0, page - T), (0, 0)))
            lo_t, hi_t = compress_lohi(tail.reshape(B * page, KV_DIM), pe, w1)
            bpp = page // CMP_STRIDE
            lo = gather_pages(lo_p.reshape(P, bpp, KV_DIM), page_table, lo_t.reshape(B, bpp, KV_DIM))
            hi = gather_pages(hi_p.reshape(P, bpp, KV_DIM), page_table, hi_t.reshape(B, bpp, KV_DIM))
            L = st.past + T
        nc = _round_up(L, SLC_BLOCK) // CMP_STRIDE - 1
        NCP = _round_up(nc, LANES)
        lo = _pad_len(lo, NCP + 1)
        hi = _pad_len(hi, NCP + 1)
        out = compress_out(lo[:, :NCP].reshape(B * NCP, KV_DIM), hi[:, 1:NCP + 1].reshape(B * NCP, KV_DIM), w2)
        return out.reshape(B, NCP, KV_DIM), nc

    def nsa(st, y, o, win_k, win_v):
        B, T = st.B, st.T
        q, kc, vc, ks, vs, kw, vw, gl = _split(y, [N_HEADS * HEAD_DIM] + [KV_DIM] * 6 + [3 * N_HEADS])
        heads = lambda a: a.reshape(B, T, N_KV_HEADS, HEAD_DIM)
        q = _rope(_head_norm(q.reshape(B, T, N_HEADS, HEAD_DIM), nsa_q_norm), st.pos)
        ks = _rope(_head_norm(heads(ks), nsa_ks_norm), st.pos)
        kw = _rope(_head_norm(heads(kw), nsa_kw_norm), st.pos)
        kc, vc, vs, vw = heads(kc), heads(vc), heads(vs), heads(vw)
        gates = jax.nn.sigmoid(gl).reshape(B, T, N_HEADS, 3)
        o.update(nsa_cmp_k=kc, nsa_cmp_v=vc, nsa_slc_k=ks, nsa_slc_v=vs)
        qp = _qpad(q, st.tq)
        kcc, nc = compress(st, kc, cache_nsa_cmp_k, nsa_pe_k, nsa_w1_k, nsa_w2_k)
        vcc, _ = compress(st, vc, cache_nsa_cmp_v, nsa_pe_v, nsa_w1_v, nsa_w2_v)
        NCP = kcc.shape[1]
        cend = jnp.arange(NCP) * CMP_STRIDE + CMP_LEN - 1
        kcc = _rope(_head_norm(kcc.reshape(B, NCP, N_KV_HEADS, HEAD_DIM), nsa_kc_norm), cend)
        KS, L = keys_of(st, ks, cache_nsa_slc_k)
        VS, _ = keys_of(st, vs, cache_nsa_slc_v)
        nbs = _round_up(L, SLC_BLOCK) // SLC_BLOCK
        NBP = _round_up(nbs, LANES)
        ci = np.arange(NCP)[:, None]
        sj = np.arange(NBP)[None, :]
        cover = ((ci * CMP_STRIDE < (sj + 1) * SLC_BLOCK) & (ci * CMP_STRIDE + CMP_LEN > sj * SLC_BLOCK)
                 & (ci < nc) & (sj < nbs))
        oc, sel = nsa_compressed(qp, kcc.reshape(B, NCP, KV_DIM).astype(BF16), vcc.astype(BF16),
                                 jnp.asarray(cover, BF16), tq=st.tq, qpos_base=st.past, nc=nc)
        osl = flash_attention(qp, KS, VS, tq=st.tq, kt=_kv_tile(KS.shape[1]), qpos_base=st.past, kv_len=L,
                              mode='blocks', sel=sel, blk=SLC_BLOCK)
        if st.past == 0:
            KW, VW, kbase = kw.reshape(B, T, KV_DIM), vw.reshape(B, T, KV_DIM), 0
            wb = min(WINDOW, T)
            o.update(nsa_win_k=kw[:, T - wb:], nsa_win_v=vw[:, T - wb:])
        else:
            KWf = jnp.concatenate([win_k, kw], axis=1)
            VWf = jnp.concatenate([win_v, vw], axis=1)
            lw = KWf.shape[1]
            wb = min(WINDOW, lw)
            o.update(nsa_win_k=KWf[:, lw - wb:], nsa_win_v=VWf[:, lw - wb:])
            kbase = st.past - win_k.shape[1]
            Lp = _round_up(lw, LANES)
            KW = _pad_len(KWf.reshape(B, lw, KV_DIM), Lp)
            VW = _pad_len(VWf.reshape(B, lw, KV_DIM), Lp)
        ow = flash_attention(qp, KW, VW, tq=st.tq, kt=_kv_tile(KW.shape[1]), qpos_base=st.past,
                             kpos_base=kbase, kv_len=KW.shape[1], mode='plain', window=WINDOW)
        oc, osl, ow = (_unpad_out(a, st.tq) for a in (oc, osl, ow))
        return gates[..., 0:1] * oc + gates[..., 1:2] * osl + gates[..., 2:3] * ow

    def dsa(st, y, o):
        B, T, tq = st.B, st.T, st.tq
        q, k, v, qi, ki, wi = _split(y, [N_HEADS * HEAD_DIM, KV_DIM, KV_DIM, IDX_HEADS * IDX_DIM, IDX_DIM, IDX_HEADS])
        q = _rope(_head_norm(q.reshape(B, T, N_HEADS, HEAD_DIM), dsa_q_norm), st.pos)
        k = _rope(_head_norm(k.reshape(B, T, N_KV_HEADS, HEAD_DIM), dsa_k_norm), st.pos)
        v = v.reshape(B, T, N_KV_HEADS, HEAD_DIM)
        qi = _rope(qi.reshape(B, T, IDX_HEADS, IDX_DIM), st.pos)
        ki = _rope(ki.reshape(B, T, 1, IDX_DIM), st.pos)[:, :, 0]
        wi = wi * IDX_HEADS ** -0.5
        o.update(dsa_k=k, dsa_v=v, dsa_idx=ki)
        K, L = keys_of(st, k, cache_dsa_k)
        V, _ = keys_of(st, v, cache_dsa_v)
        KI, _ = keys_of(st, ki, cache_dsa_idx_k)
        kt = _kv_tile(K.shape[1])
        nQ = T // tq
        qi_r = qi.astype(BF16).reshape(B, nQ, tq, IDX_HEADS, IDX_DIM).transpose(0, 1, 3, 2, 4)
        qi_r = qi_r.reshape(B, nQ, IDX_HEADS * tq, IDX_DIM)
        wi_r = wi.reshape(B, nQ, tq, IDX_HEADS)
        n_sel = min(DSA_TOPK, L // 4)
        thr, need = dsa_threshold(qi_r, wi_r, KI, tq=tq, kt=kt, qpos_base=st.past, n_sel=n_sel)
        return _unpad_out(_attend(st, q, K, V, L, kt, mode='dsa', dsa=(qi_r, wi_r, KI, thr, need)), tq)

    for layer in range(ada_w.shape[0]):
        kind = ('fox', 'moba', 'nsa', 'dsa')[layer % 4]
        for s, st in enumerate(streams):
            x = xs[s]
            sh1, sc1, g1, sh2, sc2, g2 = _modulation(conds[s], ada_w[layer], ada_b[layer])
            y = fused_matmul(x, w_in[kind], seq_len=st.T, tm=st.tm,
                             norm=(norm_mix[layer][None, :], sc1, sh1)).reshape(st.B, st.T, -1)
            if kind == 'fox':
                att = fox(st, y, outs[s])
            elif kind == 'moba':
                att = moba(st, y, outs[s])
            elif kind == 'nsa':
                att = nsa(st, y, outs[s], state_nsa_win_k, state_nsa_win_v)
            else:
                att = dsa(st, y, outs[s])
            x = fused_matmul(att.reshape(-1, N_HEADS * HEAD_DIM), w_out[kind], seq_len=st.T, tm=st.tm,
                             epi=(x, g1))
            ffn_norm = (norm_ffn[layer][None, :], sc2, sh2)
            logits = fused_matmul(x, w_router, seq_len=st.T, tm=st.tm, norm=ffn_norm)[:, :N_EXPERTS]
            coef = _routing(logits, router_b)
            xs[s] = moe_dense(x, coef, *ffn_norm, g2, moe_w1b[layer], moe_w3b[layer], moe_w2b[layer],
                              seq_len=st.T, tm=2 * st.tm)

    names = ('fox_k', 'fox_v', 'fox_logf', 'moba_k', 'moba_v', 'nsa_cmp_k', 'nsa_cmp_v', 'nsa_slc_k', 'nsa_slc_v',
             'nsa_win_k', 'nsa_win_v', 'dsa_k', 'dsa_v', 'dsa_idx')
    res = [xs[0].reshape(x_prompt.shape), xs[1].reshape(x_sample.shape)]
    for n in names:
        res += [outs[0][n], outs[1][n]]
    return tuple(res)
```

```python
import functools

import numpy as np
import jax
import jax.numpy as jnp
from jax import lax
from jax.experimental import pallas as pl
from jax.experimental.pallas import tpu as pltpu

HEAD_DIM = 64
N_KV_HEADS = 4
GROUP = 4
N_HEADS = N_KV_HEADS * GROUP
KV_DIM = N_KV_HEADS * HEAD_DIM
ROT_DIM = HEAD_DIM // 4
ROPE_THETA = 500000.0
ATTN_SCALE = HEAD_DIM ** -0.5
LOG2E = 1.4426950408889634
MOBA_BLOCK = 256
MOBA_TOPK = 3
CMP_LEN = 32
CMP_STRIDE = 16
CMP_HIDDEN = 64
SLC_BLOCK = 64
SLC_TOPK = 16
WINDOW = 512
IDX_HEADS = 8
IDX_DIM = 64
DSA_TOPK = 256
N_EXPERTS = 16
N_GROUPS = 4
EXPERTS_PER_GROUP = N_EXPERTS // N_GROUPS
TOP_K = 2
EPS = 1e-6
NEG_INF = -1e30

LANES = 128
VMEM_LIMIT = 56 * 1024 * 1024
BF16 = jnp.bfloat16
F32 = jnp.float32
INT_MIN = -2 ** 31


def _round_up(a, m):
    return (a + m - 1) // m * m


def _params(sem):
    return pltpu.CompilerParams(dimension_semantics=sem, vmem_limit_bytes=VMEM_LIMIT)


def _gather_kernel(pt_ref, *refs, pps, n_main):
    page_refs, tail_ref, o_ref = refs[:pps], refs[pps], refs[pps + 1]
    j = pl.program_id(1)

    @pl.when(j < n_main)
    def _():
        for p in range(pps):
            o_ref[p] = page_refs[p][...]

    @pl.when(j == n_main)
    def _():
        o_ref[0] = tail_ref[...]
        for p in range(1, pps):
            o_ref[p] = jnp.zeros(o_ref.shape[1:], o_ref.dtype)


def gather_pages(pool, page_table, tail):
    _, R, C = pool.shape
    B, n_pages = page_table.shape
    pps = max(d for d in (8, 4, 2, 1) if n_pages % d == 0)
    n_main = n_pages // pps

    def page_map(b, j, pt, p):
        return (pt[b, jnp.minimum(j * pps + p, n_pages - 1)], 0, 0)

    in_specs = [pl.BlockSpec((None, R, C), functools.partial(page_map, p=p)) for p in range(pps)]
    in_specs.append(pl.BlockSpec((None, R, C), lambda b, j, pt: (b, 0, 0)))
    out = pl.pallas_call(
        functools.partial(_gather_kernel, pps=pps, n_main=n_main), name='gather_pages',
        grid_spec=pltpu.PrefetchScalarGridSpec(
            num_scalar_prefetch=1, grid=(B, n_main + 1), in_specs=in_specs,
            out_specs=pl.BlockSpec((None, pps, R, C), lambda b, j, pt: (b, j, 0, 0))),
        out_shape=jax.ShapeDtypeStruct((B, (n_main + 1) * pps, R, C), pool.dtype),
        compiler_params=_params(("arbitrary", "arbitrary")),
    )(page_table, *([pool] * pps), tail)
    return out.reshape(B, (n_main + 1) * pps * R, C)


def gather_rows(pool, page_table, new_rows):
    P, page = pool.shape[:2]
    C = int(np.prod(pool.shape[2:]))
    B, T = new_rows.shape[:2]
    fold = KV_DIM // C if C < KV_DIM else 1
    tail = jnp.pad(new_rows.reshape(B, T, C), ((0, 0), (0, page - T), (0, 0)))
    g = gather_pages(pool.reshape(P, page // fold, C * fold), page_table,
                     tail.reshape(B, page // fold, C * fold))
    return g.reshape(B, -1, C)


def _adanorm(x, g, sc, sh):
    y = x * lax.rsqrt(jnp.mean(x * x, axis=-1, keepdims=True) + EPS)
    return (y * g) * (1.0 + sc) + sh


def _matmul_kernel(*refs, has_norm, has_epi):
    it = iter(refs)
    x_ref, w_ref = next(it), next(it)
    x = x_ref[...]
    if has_norm:
        g_ref, sc_ref, sh_ref = next(it), next(it), next(it)
        x = _adanorm(x, g_ref[...], sc_ref[...], sh_ref[...])
    acc = jnp.dot(x.astype(BF16), w_ref[...], preferred_element_type=F32)
    if has_epi:
        res_ref, gate_ref = next(it), next(it)
        acc = res_ref[...] + gate_ref[...] * acc
    o_ref = next(it)
    o_ref[...] = acc


def _mod_spec(mod, tm, seq_len):
    D = mod.shape[-1]
    if seq_len % tm == 0:
        tps = seq_len // tm
        return mod, pl.BlockSpec((None, 1, D), lambda i: (i // tps, 0, 0))
    rows = jnp.broadcast_to(mod, (mod.shape[0], seq_len, D)).reshape(-1, D)
    return rows, pl.BlockSpec((tm, D), lambda i: (i, 0))


def fused_matmul(x, w, *, seq_len, tm, norm=None, epi=None):
    N, K = x.shape
    M = w.shape[1]
    tm = min(tm, N)
    args = [x, w]
    specs = [pl.BlockSpec((tm, K), lambda i: (i, 0)), pl.BlockSpec((K, M), lambda i: (0, 0))]
    if norm is not None:
        gain, sc, sh = norm
        args.append(gain)
        specs.append(pl.BlockSpec((1, K), lambda i: (0, 0)))
        for m in (sc, sh):
            a, s = _mod_spec(m, tm, seq_len)
            args.append(a)
            specs.append(s)
    if epi is not None:
        res, gate = epi
        args.append(res)
        specs.append(pl.BlockSpec((tm, M), lambda i: (i, 0)))
        a, s = _mod_spec(gate, tm, seq_len)
        args.append(a)
        specs.append(s)
    return pl.pallas_call(
        functools.partial(_matmul_kernel, has_norm=norm is not None, has_epi=epi is not None),
        name='proj_in' if norm is not None else 'proj_out',
        grid=(N // tm,), in_specs=specs,
        out_specs=pl.BlockSpec((tm, M), lambda i: (i, 0)),
        out_shape=jax.ShapeDtypeStruct((N, M), F32),
        compiler_params=_params(("arbitrary",)),
    )(*args)


ROUTE_ROWS = 8


def _first_match(vals, target):
    out, seen = [], None
    for v in vals:
        hit = v == target
        out.append(hit if seen is None else hit & jnp.logical_not(seen))
        seen = hit if seen is None else seen | hit
    return out


def _router_kernel(x_ref, g_ref, sc_ref, sh_ref, wt_ref, b_ref, o_ref):
    h = _adanorm(x_ref[...], g_ref[...], sc_ref[...], sh_ref[...]).astype(BF16)
    logit = lax.dot_general(wt_ref[...], h, (((1,), (1,)), ((), ())), preferred_element_type=F32)
    aff = jax.nn.sigmoid(logit)
    grp = aff + b_ref[...]
    n = EXPERTS_PER_GROUP
    a = [grp[k * ROUTE_ROWS:(k + 1) * ROUTE_ROWS] for k in range(n)]
    f = [aff[k * ROUTE_ROWS:(k + 1) * ROUTE_ROWS] for k in range(n)]
    top1 = functools.reduce(jnp.maximum, a)
    s1 = _first_match(a, top1)
    rest = [jnp.where(s, -jnp.inf, v) for s, v in zip(s1, a)]
    top2 = functools.reduce(jnp.maximum, rest)
    s2 = _first_match(rest, top2)
    row = lax.broadcasted_iota(jnp.int32, top1.shape, 0).astype(F32)
    score = jnp.where(row < N_GROUPS, top1 + top2, -jnp.inf)
    best = jnp.max(score, axis=0, keepdims=True)
    gsel = row == jnp.min(jnp.where(score == best, row, 1e9), axis=0, keepdims=True)
    w1 = sum(jnp.where(s, v, 0.0) for s, v in zip(s1, f))
    w2 = sum(jnp.where(s, v, 0.0) for s, v in zip(s2, f))
    den = w1 + w2
    for k in range(n):
        o_ref[k * ROUTE_ROWS:(k + 1) * ROUTE_ROWS, :] = jnp.where(
            gsel & s1[k], w1 / den, jnp.where(gsel & s2[k], w2 / den, 0.0))


def route(x, router_w, router_b, norm, *, seq_len, tm):
    assert TOP_K == 2
    N, D = x.shape
    tm = min(tm, N)
    wt = router_w.T.reshape(N_GROUPS, EXPERTS_PER_GROUP, D).transpose(1, 0, 2)
    wt = jnp.pad(wt, ((0, 0), (0, ROUTE_ROWS - N_GROUPS), (0, 0))).reshape(-1, D).astype(BF16)
    bt = router_b.reshape(N_GROUPS, EXPERTS_PER_GROUP).T
    bt = jnp.pad(bt, ((0, 0), (0, ROUTE_ROWS - N_GROUPS))).reshape(-1, 1)
    rows = wt.shape[0]
    gain, sc, sh = norm
    args, specs = [x, gain], [pl.BlockSpec((tm, D), lambda i: (i, 0)), pl.BlockSpec((1, D), lambda i: (0, 0))]
    for m in (sc, sh):
        a, s = _mod_spec(m, tm, seq_len)
        args.append(a)
        specs.append(s)
    args += [wt, bt]
    specs += [pl.BlockSpec((rows, D), lambda i: (0, 0)), pl.BlockSpec((rows, 1), lambda i: (0, 0))]
    coef_t = pl.pallas_call(
        _router_kernel, name='router', grid=(N // tm,), in_specs=specs,
        out_specs=pl.BlockSpec((rows, tm), lambda i: (0, i)),
        out_shape=jax.ShapeDtypeStruct((rows, N), F32),
        compiler_params=_params(("arbitrary",)),
    )(*args)
    coef_t = coef_t.reshape(EXPERTS_PER_GROUP, ROUTE_ROWS, N)[:, :N_GROUPS]
    return coef_t.transpose(2, 1, 0).reshape(N, N_EXPERTS)


def _cumsum_kernel(x_ref, o_ref, carry_ref, *, tc):
    @pl.when(pl.program_id(1) == 0)
    def _():
        carry_ref[...] = jnp.zeros_like(carry_ref)

    r = lax.broadcasted_iota(jnp.int32, (tc, tc), 0)
    c = lax.broadcasted_iota(jnp.int32, (tc, tc), 1)
    tri = (c <= r).astype(F32)
    out = jnp.dot(tri, x_ref[...], preferred_element_type=F32,
                  precision=lax.Precision.HIGHEST) + carry_ref[...]
    o_ref[...] = out
    carry_ref[...] = out[tc - 1:tc, :]


def cumsum_rows(x, tc=256):
    B, L, H = x.shape
    tc = min(tc, L)
    return pl.pallas_call(
        functools.partial(_cumsum_kernel, tc=tc), name='cumsum',
        grid=(B, L // tc),
        in_specs=[pl.BlockSpec((None, tc, H), lambda b, j: (b, j, 0))],
        out_specs=pl.BlockSpec((None, tc, H), lambda b, j: (b, j, 0)),
        out_shape=jax.ShapeDtypeStruct((B, L, H), F32),
        scratch_shapes=[pltpu.VMEM((1, H), F32)],
        compiler_params=_params(("arbitrary", "arbitrary")),
    )(x)


def _select_blocks(score, own, n_top):
    col = lax.broadcasted_iota(jnp.int32, score.shape, 1)
    colf = col.astype(F32)
    sc = jnp.where(col < own, score, -jnp.inf)
    sel = col == own
    for _ in range(n_top):
        m = jnp.max(sc, axis=-1, keepdims=True)
        idx = jnp.min(jnp.where(sc == m, colf, 1e9), axis=-1, keepdims=True)
        hit = colf == idx
        sel = sel | (hit & (m > -jnp.inf))
        sc = jnp.where(hit, -jnp.inf, sc)
    return sel


def _row_pos(tq, rows, base):
    assert tq & (tq - 1) == 0
    r = lax.broadcasted_iota(jnp.int32, (rows, 1), 0)
    return base + (r & (tq - 1))


def _moba_select_kernel(q_ref, kb_ref, o_ref, *, tq, qpos_base):
    qpos = _row_pos(tq, tq, qpos_base + pl.program_id(1) * tq)
    own = lax.shift_right_logical(qpos, int(np.log2(MOBA_BLOCK)))
    for h in range(N_KV_HEADS):
        gate = jnp.dot(q_ref[:, h * KV_DIM:(h + 1) * KV_DIM], kb_ref[h], preferred_element_type=F32)
        o_ref[h] = _select_blocks(gate, own, MOBA_TOPK).astype(F32)


def moba_select(q_nat, kbt, *, tq, qpos_base):
    B, T, D = q_nat.shape
    NBP = kbt.shape[-1]
    return pl.pallas_call(
        functools.partial(_moba_select_kernel, tq=tq, qpos_base=qpos_base), name='moba_select',
        grid=(B, T // tq),
        in_specs=[pl.BlockSpec((None, tq, D), lambda b, i: (b, i, 0)),
                  pl.BlockSpec((None, N_KV_HEADS, KV_DIM, NBP), lambda b, i: (b, 0, 0, 0))],
        out_specs=pl.BlockSpec((None, N_KV_HEADS, tq, NBP), lambda b, i: (b, 0, i, 0)),
        out_shape=jax.ShapeDtypeStruct((B, N_KV_HEADS, T, NBP), F32),
        compiler_params=_params(("arbitrary", "arbitrary")),
    )(q_nat, kbt)


def _block_mean_kernel(k_ref, o_ref, *, nb, blk):
    x = k_ref[...].reshape(nb, blk, KV_DIM)
    o_ref[...] = jnp.sum(x, axis=1) * (1.0 / blk)


def block_means(k, blk):
    B, L, C = k.shape
    NB = L // blk
    nb = max(d for d in (8, 4, 2, 1) if NB % d == 0)
    if NB // nb > 1 and nb % 8:
        nb = NB
    return pl.pallas_call(
        functools.partial(_block_mean_kernel, nb=nb, blk=blk), name='block_means',
        grid=(B, NB // nb),
        in_specs=[pl.BlockSpec((None, nb * blk, C), lambda b, j: (b, j, 0))],
        out_specs=pl.BlockSpec((None, nb, C), lambda b, j: (b, j, 0)),
        out_shape=jax.ShapeDtypeStruct((B, NB, C), F32),
        compiler_params=_params(("arbitrary", "arbitrary")),
    )(k)


def _compress_lohi_kernel(x_ref, pe_lo_ref, pe_hi_ref, wlo_ref, whi_ref, lo_ref, hi_ref, *, nblk):
    lo = jnp.zeros((nblk, KV_DIM), F32)
    hi = jnp.zeros((nblk, KV_DIM), F32)
    for t in range(CMP_STRIDE):
        x = jnp.concatenate([x_ref[pl.ds(2 * t, nblk, stride=2 * CMP_STRIDE), :],
                             x_ref[pl.ds(2 * t + 1, nblk, stride=2 * CMP_STRIDE), :]], axis=-1)
        lo += jnp.dot((x + pe_lo_ref[t:t + 1, :]).astype(BF16), wlo_ref[t], preferred_element_type=F32)
        hi += jnp.dot((x + pe_hi_ref[t:t + 1, :]).astype(BF16), whi_ref[t], preferred_element_type=F32)
    lo_ref[...] = lo
    hi_ref[...] = hi


def compress_lohi(raw, pe, w1, tr=2048):
    N = raw.shape[0]
    tr = max(d for d in (tr, tr // 2, tr // 4, tr // 8, LANES) if N % d == 0)
    nblk = tr // CMP_STRIDE
    eye = jnp.eye(N_KV_HEADS, dtype=F32)

    def blockdiag(w):
        w = w.reshape(CMP_STRIDE, HEAD_DIM, CMP_HIDDEN)
        return jnp.einsum('tdc,hk->thdkc', w, eye).reshape(CMP_STRIDE, KV_DIM, KV_DIM).astype(BF16)

    half = CMP_STRIDE * HEAD_DIM
    pe_lo = jnp.tile(pe[:CMP_STRIDE], (1, N_KV_HEADS))
    pe_hi = jnp.tile(pe[CMP_STRIDE:], (1, N_KV_HEADS))
    const = lambda i: (0, 0)
    const3 = lambda i: (0, 0, 0)
    return pl.pallas_call(
        functools.partial(_compress_lohi_kernel, nblk=nblk), name='compress_lohi',
        grid=(N // tr,),
        in_specs=[pl.BlockSpec((2 * tr, LANES), lambda i: (i, 0)),
                  pl.BlockSpec((CMP_STRIDE, KV_DIM), const), pl.BlockSpec((CMP_STRIDE, KV_DIM), const),
                  pl.BlockSpec((CMP_STRIDE, KV_DIM, KV_DIM), const3),
                  pl.BlockSpec((CMP_STRIDE, KV_DIM, KV_DIM), const3)],
        out_specs=[pl.BlockSpec((nblk, KV_DIM), lambda i: (i, 0))] * 2,
        out_shape=[jax.ShapeDtypeStruct((N // CMP_STRIDE, KV_DIM), F32)] * 2,
        compiler_params=_params(("arbitrary",)),
    )(raw.reshape(2 * N, LANES), pe_lo, pe_hi, blockdiag(w1[:half]), blockdiag(w1[half:]))


def _compress_out_kernel(lo_ref, hi_ref, w_ref, o_ref):
    z = lo_ref[...] + hi_ref[...]
    z = z * jax.nn.sigmoid(z)
    o_ref[...] = jnp.dot(z.astype(BF16), w_ref[...], preferred_element_type=F32)


def compress_out(lo, hi_next, w2):
    N = lo.shape[0]
    tm = max(d for d in (512, 256, 128, 64, 32, 16, 8) if N % d == 0)
    w = jnp.einsum('cd,hk->hckd', w2, jnp.eye(N_KV_HEADS, dtype=F32)).reshape(KV_DIM, KV_DIM).astype(BF16)
    return pl.pallas_call(
        _compress_out_kernel, name='compress_out', grid=(N // tm,),
        in_specs=[pl.BlockSpec((tm, KV_DIM), lambda i: (i, 0)), pl.BlockSpec((tm, KV_DIM), lambda i: (i, 0)),
                  pl.BlockSpec((KV_DIM, KV_DIM), lambda i: (0, 0))],
        out_specs=pl.BlockSpec((tm, KV_DIM), lambda i: (i, 0)),
        out_shape=jax.ShapeDtypeStruct((N, KV_DIM), F32),
        compiler_params=_params(("arbitrary",)),
    )(lo, hi_next, w)


def _lane_head_mask(rows, h):
    lane = lax.broadcasted_iota(jnp.int32, (rows, KV_DIM), 1)
    return (lane >= h * HEAD_DIM) & (lane < (h + 1) * HEAD_DIM)


def _nsa_cmp_kernel(q_ref, kc_ref, vc_ref, cover_ref, o_ref, sel_ref, *, tq, qpos_base, nc):
    R = GROUP * tq
    NCP = kc_ref.shape[0]
    q0 = qpos_base + pl.program_id(1) * tq
    qpos = _row_pos(tq, R, q0)
    c = lax.broadcasted_iota(jnp.int32, (1, NCP), 1)
    mask = (c * CMP_STRIDE + (CMP_LEN - 1) <= qpos) & (c < nc)
    kc = kc_ref[...]
    vc = vc_ref[...]
    own = lax.shift_right_logical(_row_pos(tq, tq, q0), int(np.log2(SLC_BLOCK)))
    out = jnp.zeros((R, KV_DIM), F32)
    for h in range(N_KV_HEADS):
        s = lax.dot_general(q_ref[h], kc, (((1,), (1,)), ((), ())), preferred_element_type=F32)
        s = jnp.where(mask, s, NEG_INF)
        m = jnp.max(s, axis=-1, keepdims=True)
        e = jnp.where(mask, jnp.exp2(s - m), 0.0)
        l = jnp.sum(e, axis=-1, keepdims=True)
        p = (e / jnp.where(l > 0.0, l, 1.0)).astype(BF16)
        out = jnp.where(_lane_head_mask(R, h), jnp.dot(p, vc, preferred_element_type=F32), out)
        imp = jnp.dot(p, cover_ref[...], preferred_element_type=F32)
        imp = jnp.sum(imp.reshape(GROUP, tq, imp.shape[-1]), axis=0)
        sel_ref[h] = _select_blocks(imp, own, SLC_TOPK - 1).astype(F32)
    o_ref[...] = out


def nsa_compressed(qpad, kcc, vcc, cover, *, tq, qpos_base, nc):
    B, nQ, _, R, _ = qpad.shape
    NCP, NBP = cover.shape
    return pl.pallas_call(
        functools.partial(_nsa_cmp_kernel, tq=tq, qpos_base=qpos_base, nc=nc), name='nsa_compressed',
        grid=(B, nQ),
        in_specs=[pl.BlockSpec((None, None, N_KV_HEADS, R, KV_DIM), lambda b, i: (b, i, 0, 0, 0)),
                  pl.BlockSpec((None, NCP, KV_DIM), lambda b, i: (b, 0, 0)),
                  pl.BlockSpec((None, NCP, KV_DIM), lambda b, i: (b, 0, 0)),
                  pl.BlockSpec((NCP, NBP), lambda b, i: (0, 0))],
        out_specs=[pl.BlockSpec((None, None, R, KV_DIM), lambda b, i: (b, i, 0, 0)),
                   pl.BlockSpec((None, N_KV_HEADS, tq, NBP), lambda b, i: (b, 0, i, 0))],
        out_shape=[jax.ShapeDtypeStruct((B, nQ, R, KV_DIM), F32),
                   jax.ShapeDtypeStruct((B, N_KV_HEADS, nQ * tq, NBP), F32)],
        compiler_params=_params(("arbitrary", "arbitrary")),
    )(qpad, kcc, vcc, cover)


def _index_keys(qi, wi, ki, kpos, qpos, tq):
    rel = jnp.maximum(lax.dot_general(qi, ki, (((1,), (1,)), ((), ())), preferred_element_type=F32), 0.0)
    score = jnp.zeros((tq, rel.shape[1]), F32)
    for i in range(IDX_HEADS):
        score = score + wi[:, i:i + 1] * rel[i * tq:(i + 1) * tq]
    score = jnp.where(kpos <= qpos, score + 0.0, -jnp.inf)
    bits = lax.bitcast_convert_type(score, jnp.int32)
    return bits ^ (lax.shift_right_arithmetic(bits, 31) & 0x7FFFFFFF)


def _dsa_thresh_kernel(qi_ref, wi_ref, ki_ref, thr_ref, need_ref, key_ref, *, tq, kt, qpos_base, n_sel, nkt):
    q0 = qpos_base + pl.program_id(1) * tq
    qpos = _row_pos(tq, tq, q0)
    last = jnp.minimum((q0 + tq - 1) // kt, nkt - 1)
    qi = qi_ref[...]
    wi = wi_ref[...]
    lane = lax.broadcasted_iota(jnp.int32, (1, kt), 1)

    def fill(jt, carry):
        ki = ki_ref[pl.ds(pl.multiple_of(jt * kt, kt), kt), :].astype(BF16)
        key_ref[jt] = _index_keys(qi, wi, ki, jt * kt + lane, qpos, tq)
        return carry

    lax.fori_loop(0, last + 1, fill, 0)

    def count(pred):
        def body(jt, acc):
            hit = pred(key_ref[jt])
            for u in range(kt // LANES):
                acc = acc + jnp.where(hit[:, u * LANES:(u + 1) * LANES], 1.0, 0.0)
            return acc
        acc = lax.fori_loop(0, last + 1, body, jnp.zeros((tq, LANES), F32))
        return jnp.sum(acc, axis=-1, keepdims=True)

    def count_ge(t):
        return count(lambda key: key >= t)

    def bit_step(it, t_u):
        cand = t_u | lax.shift_left(jnp.int32(1), 31 - it)
        ok = count_ge(cand ^ INT_MIN) >= n_sel
        return jnp.where(ok, cand, t_u)

    t_u = lax.fori_loop(0, 32, bit_step, jnp.zeros((tq, 1), jnp.int32))
    thr = t_u ^ INT_MIN

    n_gt = count(lambda key: key > thr)
    thr_ref[...] = thr
    need_ref[...] = n_sel - n_gt


def dsa_threshold(qi, wi, ki, *, tq, kt, qpos_base, n_sel):
    B, nQ = qi.shape[:2]
    Lk = ki.shape[1]
    nkt = Lk // kt
    return pl.pallas_call(
        functools.partial(_dsa_thresh_kernel, tq=tq, kt=kt, qpos_base=qpos_base, n_sel=n_sel, nkt=nkt),
        name='dsa_threshold',
        grid=(B, nQ),
        in_specs=[pl.BlockSpec((None, None, IDX_HEADS * tq, IDX_DIM), lambda b, i: (b, i, 0, 0)),
                  pl.BlockSpec((None, None, tq, IDX_HEADS), lambda b, i: (b, i, 0, 0)),
                  pl.BlockSpec((None, Lk, IDX_DIM), lambda b, i: (b, 0, 0))],
        out_specs=[pl.BlockSpec((None, None, tq, 1), lambda b, i: (b, i, 0, 0))] * 2,
        out_shape=[jax.ShapeDtypeStruct((B, nQ, tq, 1), jnp.int32),
                   jax.ShapeDtypeStruct((B, nQ, tq, 1), F32)],
        scratch_shapes=[pltpu.VMEM((nkt, tq, kt), jnp.int32)],
        compiler_params=_params(("arbitrary", "arbitrary")),
    )(qi, wi, ki)


def _flash_kernel(*refs, mode, tq, kt, nk, rc, qpos_base, kpos_base, window, blk, kfirst, klast):
    it = iter(refs)
    q_ref, k_ref, v_ref = next(it), next(it), next(it)
    if mode == 'fox':
        fq_ref, fk_ref = next(it), next(it)
    elif mode == 'blocks':
        sel_ref = next(it)
    elif mode == 'dsa':
        qi_ref, wi_ref, ki_ref, thr_ref, need_ref = (next(it) for _ in range(5))
    o_ref, m_ref, l_ref, acc_ref = next(it), next(it), next(it), next(it)
    s_ref, p_ref, a_ref, bias_ref = next(it), next(it), next(it), next(it)
    if mode == 'dsa':
        eq_ref = next(it)

    R = GROUP * tq
    i, j = pl.program_id(1), pl.program_id(2)
    q0 = qpos_base + i * tq
    jt = kfirst(i) + j
    kpos0 = kpos_base + jt * kt

    @pl.when(j == 0)
    def _():
        m_ref[...] = jnp.full(m_ref.shape, NEG_INF, F32)
        l_ref[...] = jnp.zeros(l_ref.shape, F32)
        acc_ref[...] = jnp.zeros(acc_ref.shape, F32)
        if mode == 'dsa':
            eq_ref[...] = jnp.zeros(eq_ref.shape, F32)

    def tile_mask():
        kpos = kpos0 + lax.broadcasted_iota(jnp.int32, (1, kt), 1)
        qpos = _row_pos(tq, tq, q0)
        mask = kpos <= qpos
        if window is not None:
            mask = mask & (qpos - kpos < window) & (kpos >= 0)
        if mode == 'dsa':
            key = _index_keys(qi_ref[...], wi_ref[...], ki_ref[...].astype(BF16), kpos, qpos, tq)
            thr = thr_ref[...]
            eq = key == thr
            r = lax.broadcasted_iota(jnp.int32, (kt, kt), 0)
            c = lax.broadcasted_iota(jnp.int32, (kt, kt), 1)
            rank = eq_ref[...] + jnp.dot(eq.astype(F32), (r <= c).astype(F32), preferred_element_type=F32)
            eq_ref[...] = rank[:, kt - 1:kt]
            mask = mask & ((key > thr) | (eq & (rank <= need_ref[...])))
        return mask

    def softmax_head(h, use_bias):
        hb = h if mode == 'blocks' else 0
        for g in range(GROUP):
            if mode == 'fox':
                fk = jnp.broadcast_to(fk_ref[h * GROUP + g:h * GROUP + g + 1, :], (rc, kt))

            def logits(t0):
                u = s_ref[h, g * tq + t0:g * tq + t0 + rc, :]
                if mode == 'fox':
                    u = u - fk
                if use_bias:
                    u = u + bias_ref[hb, t0:t0 + rc, :]
                return u

            for t0 in range(0, tq, rc):
                rows = slice(g * tq + t0, g * tq + t0 + rc)
                row = jnp.max(logits(t0), axis=-1, keepdims=True)
                if mode == 'fox':
                    row = row + fq_ref[h, rows, :]
                m_prev = m_ref[h, rows, :]
                m_new = jnp.maximum(m_prev, row)
                m_ref[h, rows, :] = m_new
                a_ref[h, rows, :] = jnp.exp2(m_prev - m_new)
            for t0 in range(0, tq, rc):
                rows = slice(g * tq + t0, g * tq + t0 + rc)
                shift = -m_ref[h, rows, :]
                if mode == 'fox':
                    shift = shift + fq_ref[h, rows, :]
                p = jnp.exp2(logits(t0) + shift)
                part = p[:, 0:LANES]
                for u in range(1, kt // LANES):
                    part = part + p[:, u * LANES:(u + 1) * LANES]
                l_ref[h, rows, :] = a_ref[h, rows, :] * l_ref[h, rows, :] + part
                p_ref[h, rows, :] = p.astype(p_ref.dtype)

    def heads(use_bias):
        kb = k_ref[...].astype(BF16)
        vb = v_ref[...].astype(BF16)
        if use_bias:
            base = jnp.where(tile_mask(), 0.0, -jnp.inf)
            if mode != 'blocks':
                bias_ref[0] = base
        for h in range(N_KV_HEADS):
            if mode == 'blocks':
                nbp = sel_ref.shape[-1]
                kblk = lax.shift_right_logical(kpos0 + lax.broadcasted_iota(jnp.int32, (nbp, kt), 1),
                                               int(np.log2(blk)))
                expand = (kblk == lax.broadcasted_iota(jnp.int32, (nbp, kt), 0)).astype(F32)
                picked = jnp.dot(sel_ref[h], expand, preferred_element_type=F32) > 0.5
                bias_ref[h] = jnp.where(picked, base, -jnp.inf)
            s_ref[h] = lax.dot_general(q_ref[h], kb, (((1,), (1,)), ((), ())), preferred_element_type=F32)
        for h in range(N_KV_HEADS):
            softmax_head(h, use_bias)
            acc_ref[h] = a_ref[h] * acc_ref[h] + jnp.dot(p_ref[h].astype(BF16), vb,
                                                         preferred_element_type=F32)

    if mode in ('plain', 'fox') and window is None:
        crosses = kpos0 + kt - 1 > q0
        pl.when((jt <= klast(i)) & crosses)(lambda: heads(True))
        pl.when((jt <= klast(i)) & jnp.logical_not(crosses))(lambda: heads(False))
    else:
        pl.when(jt <= klast(i))(lambda: heads(True))

    @pl.when(j == nk - 1)
    def _():
        out = jnp.zeros((R, KV_DIM), F32)
        for h in range(N_KV_HEADS):
            l = jnp.sum(l_ref[h], axis=-1, keepdims=True)
            out = jnp.where(_lane_head_mask(R, h), acc_ref[h] / jnp.where(l > 0.0, l, 1.0), out)
        o_ref[...] = out


def flash_attention(qpad, k, v, *, tq, kt, qpos_base, kpos_base=0, kv_len, mode='plain', window=None,
                    fq=None, fkT=None, sel=None, blk=None, dsa=None):
    B, nQ, _, R, _ = qpad.shape
    Lk = k.shape[1]
    nkt = Lk // kt
    last_tile = (kv_len - 1) // kt

    def klast(i):
        return jnp.minimum((qpos_base + (i + 1) * tq - 1 - kpos_base) // kt, last_tile)

    def kfirst(i):
        if window is None:
            return 0
        return jnp.maximum(qpos_base + i * tq - (window - 1) - kpos_base, 0) // kt

    def span(i):
        lo = 0 if window is None else max(qpos_base + i * tq - (window - 1) - kpos_base, 0) // kt
        return min((qpos_base + (i + 1) * tq - 1 - kpos_base) // kt, last_tile) - lo + 1

    nk = max(span(i) for i in range(nQ))

    def kv_map(b, i, j):
        return (b, jnp.minimum(kfirst(i) + j, klast(i)), 0)

    args = [qpad, k, v]
    specs = [pl.BlockSpec((None, None, N_KV_HEADS, R, KV_DIM), lambda b, i, j: (b, i, 0, 0, 0)),
             pl.BlockSpec((None, kt, KV_DIM), kv_map), pl.BlockSpec((None, kt, KV_DIM), kv_map)]
    rc = min(tq, 16)
    scratch = [pltpu.VMEM((N_KV_HEADS, R, 1), F32), pltpu.VMEM((N_KV_HEADS, R, LANES), F32),
               pltpu.VMEM((N_KV_HEADS, R, KV_DIM), F32),
               pltpu.VMEM((N_KV_HEADS, R, kt), F32),
               pltpu.VMEM((N_KV_HEADS, R, kt), BF16 if rc % 16 == 0 else F32),
               pltpu.VMEM((N_KV_HEADS, R, 1), F32),
               pltpu.VMEM((N_KV_HEADS if mode == 'blocks' else 1, tq, kt), F32)]
    if mode == 'fox':
        args += [fq, fkT]
        specs += [pl.BlockSpec((None, None, N_KV_HEADS, R, 1), lambda b, i, j: (b, i, 0, 0, 0)),
                  pl.BlockSpec((None, N_HEADS, kt), lambda b, i, j: kv_map(b, i, j)[:1] + (0, kv_map(b, i, j)[1]))]
    elif mode == 'blocks':
        NBP = sel.shape[-1]
        args += [sel]
        specs += [pl.BlockSpec((None, N_KV_HEADS, tq, NBP), lambda b, i, j: (b, 0, i, 0))]
    elif mode == 'dsa':
        qi, wi, ki, thr, need = dsa
        args += [qi, wi, ki, thr, need]
        specs += [pl.BlockSpec((None, None, IDX_HEADS * tq, IDX_DIM), lambda b, i, j: (b, i, 0, 0)),
                  pl.BlockSpec((None, None, tq, IDX_HEADS), lambda b, i, j: (b, i, 0, 0)),
                  pl.BlockSpec((None, kt, IDX_DIM), kv_map),
                  pl.BlockSpec((None, None, tq, 1), lambda b, i, j: (b, i, 0, 0)),
                  pl.BlockSpec((None, None, tq, 1), lambda b, i, j: (b, i, 0, 0))]
        scratch.append(pltpu.VMEM((tq, 1), F32))
    return pl.pallas_call(
        functools.partial(_flash_kernel, mode=mode, tq=tq, kt=kt, nk=nk, rc=rc, qpos_base=qpos_base,
                          kpos_base=kpos_base, window=window, blk=blk, kfirst=kfirst, klast=klast),
        name='flash_' + mode + ('_win' if window else ''),
        grid=(B, nQ, nk), in_specs=specs,
        out_specs=pl.BlockSpec((None, None, R, KV_DIM), lambda b, i, j: (b, i, 0, 0)),
        out_shape=jax.ShapeDtypeStruct((B, nQ, R, KV_DIM), F32),
        scratch_shapes=scratch,
        compiler_params=_params(("arbitrary", "arbitrary", "arbitrary")),
    )(*args)


def _moe_kernel(x_ref, g_ref, sc_ref, sh_ref, gate_ref, c_ref, w1_ref, w3_ref, w2_ref, o_ref, h_ref, acc_ref):
    e = pl.program_id(1)

    @pl.when(e == 0)
    def _():
        h_ref[...] = _adanorm(x_ref[...], g_ref[...], sc_ref[...], sh_ref[...]).astype(BF16)
        acc_ref[...] = jnp.zeros_like(acc_ref)

    h = h_ref[...]
    a = jnp.dot(h, w1_ref[...], preferred_element_type=F32)
    a = (a * jax.nn.sigmoid(a)) * jnp.dot(h, w3_ref[...], preferred_element_type=F32)
    y = jnp.dot(a.astype(BF16), w2_ref[...], preferred_element_type=F32)
    c = c_ref[...]
    lane = lax.broadcasted_iota(jnp.int32, c.shape, 1)
    ce = jnp.sum(jnp.where(lane == e, c, 0.0), axis=-1, keepdims=True)
    acc_ref[...] += jnp.where(ce != 0.0, ce * y, 0.0)

    @pl.when(e == pl.num_programs(1) - 1)
    def _():
        o_ref[...] = x_ref[...] + gate_ref[...] * acc_ref[...]


def moe_dense(x, coef, gain, sc, sh, gate, w1, w3, w2, *, seq_len, tm):
    N, D = x.shape
    E, _, DE = w1.shape
    tm = min(tm, N)
    row = lambda i, e: (i, 0)
    specs = [pl.BlockSpec((tm, D), row), pl.BlockSpec((1, D), lambda i, e: (0, 0))]
    args = [x, gain]
    for m in (sc, sh, gate):
        a, s = _mod_spec(m, tm, seq_len)
        args.append(a)
        imap = s.index_map
        specs.append(pl.BlockSpec(s.block_shape, lambda i, e, imap=imap: imap(i)))
    args += [coef, w1, w3, w2]
    specs += [pl.BlockSpec((tm, E), row),
              pl.BlockSpec((None, D, DE), lambda i, e: (e, 0, 0)),
              pl.BlockSpec((None, D, DE), lambda i, e: (e, 0, 0)),
              pl.BlockSpec((None, DE, D), lambda i, e: (e, 0, 0))]
    return pl.pallas_call(
        _moe_kernel, name='moe_dense', grid=(N // tm, E), in_specs=specs,
        out_specs=pl.BlockSpec((tm, D), row),
        out_shape=jax.ShapeDtypeStruct((N, D), F32),
        scratch_shapes=[pltpu.VMEM((tm, D), BF16), pltpu.VMEM((tm, D), F32)],
        compiler_params=_params(("arbitrary", "arbitrary")),
    )(*args)


def _head_norm(x, g):
    return x * lax.rsqrt(jnp.mean(x * x, axis=-1, keepdims=True) + EPS) * g


def _rope(x, pos):
    half = ROT_DIM // 2
    inv = ROPE_THETA ** (-jnp.arange(half, dtype=F32) / half)
    ang = pos.astype(F32)[:, None] * inv[None, :]
    cos, sin = jnp.cos(ang)[:, None, :], jnp.sin(ang)[:, None, :]
    x1, x2 = x[..., :half], x[..., half:ROT_DIM]
    return jnp.concatenate([x1 * cos - x2 * sin, x2 * cos + x1 * sin, x[..., ROT_DIM:]], axis=-1)


def _split(y, sizes):
    return jnp.split(y[..., :sum(sizes)], np.cumsum(sizes)[:-1].tolist(), axis=-1)


def _pad_cols(w, mult=256):
    return jnp.pad(w, ((0, 0), (0, _round_up(w.shape[1], mult) - w.shape[1])))


def _qpad(q, tq):
    B, T = q.shape[:2]
    nQ = T // tq
    q = (q * (ATTN_SCALE * LOG2E)).astype(BF16).reshape(B, nQ, tq, N_KV_HEADS, GROUP, HEAD_DIM)
    q = q.transpose(0, 1, 3, 4, 2, 5)
    eye = jnp.eye(N_KV_HEADS, dtype=BF16)
    q = q[:, :, :, :, :, None, :] * eye[None, None, :, None, None, :, None]
    return q.reshape(B, nQ, N_KV_HEADS, GROUP * tq, KV_DIM)


def _unpad_out(o, tq):
    B, nQ = o.shape[:2]
    o = o.reshape(B, nQ, GROUP, tq, N_KV_HEADS, HEAD_DIM).transpose(0, 1, 3, 4, 2, 5)
    return o.reshape(B, nQ * tq, N_HEADS, HEAD_DIM)


def _rows_gt(a, tq):
    B, T, _ = a.shape
    a = a.reshape(B, T // tq, tq, N_KV_HEADS, GROUP).transpose(0, 1, 3, 4, 2)
    return a.reshape(B, T // tq, N_KV_HEADS, GROUP * tq, 1)


def _pad_len(a, L):
    return jnp.pad(a, ((0, 0), (0, L - a.shape[1])) + ((0, 0),) * (a.ndim - 2))


def _modulation(c, ada_w, ada_b):
    m = jax.nn.silu(c) @ ada_w + ada_b
    return [a[:, None, :] for a in jnp.split(m, 6, axis=-1)]


class _Stream:
    def __init__(self, B, T, past):
        self.B, self.T, self.past = B, T, past
        self.tq = min(T, 128)
        self.tm = min(B * T, 512)
        self.pos = past + jnp.arange(T)


def _attend(st, q, k, v, kv_len, kt, **kw):
    o = flash_attention(_qpad(q, st.tq), k, v, tq=st.tq, kt=kt, qpos_base=st.past, kv_len=kv_len, **kw)
    return o


def _kv_tile(L):
    return max(d for d in (512, 256, 128) if L % d == 0)


def kernel(x_prompt, x_sample, cache_fox_k, cache_fox_v, cache_fox_logf, cache_moba_k, cache_moba_v,
           cache_nsa_cmp_k, cache_nsa_cmp_v, cache_nsa_slc_k, cache_nsa_slc_v, state_nsa_win_k, state_nsa_win_v,
           cache_dsa_k, cache_dsa_v, cache_dsa_idx_k, page_table, c_prompt, c_sample,
           ada_w, ada_b, norm_mix, norm_ffn,
           fox_w_in, fox_b_f, fox_q_norm, fox_k_norm, fox_w_o,
           moba_w_in, moba_q_norm, moba_k_norm, moba_w_o,
           nsa_w_in, nsa_q_norm, nsa_kc_norm, nsa_ks_norm, nsa_kw_norm, nsa_pe_k, nsa_w1_k, nsa_w2_k,
           nsa_pe_v, nsa_w1_v, nsa_w2_v, nsa_w_o,
           dsa_w_in, dsa_q_norm, dsa_k_norm, dsa_w_o,
           router_w, router_b, moe_w1, moe_w3, moe_w2):
    D = x_prompt.shape[-1]
    page = cache_fox_k.shape[1]
    past = page_table.shape[1] * page
    streams = (_Stream(x_prompt.shape[0], x_prompt.shape[1], 0),
               _Stream(x_sample.shape[0], x_sample.shape[1], past))
    conds = (c_prompt, c_sample)
    xs = [x_prompt.reshape(-1, D), x_sample.reshape(-1, D)]
    outs = [dict(), dict()]

    w_in = {'fox': fox_w_in, 'moba': moba_w_in, 'nsa': nsa_w_in, 'dsa': dsa_w_in}
    w_out = {'fox': fox_w_o, 'moba': moba_w_o, 'nsa': nsa_w_o, 'dsa': dsa_w_o}
    w_in = {n: _pad_cols(w).astype(BF16) for n, w in w_in.items()}
    w_out = {n: w.astype(BF16) for n, w in w_out.items()}
    moe_w1b, moe_w3b, moe_w2b = moe_w1.astype(BF16), moe_w3.astype(BF16), moe_w2.astype(BF16)

    def keys_of(st, new, cache):
        B, T = new.shape[:2]
        if st.past == 0:
            return new.reshape(B, T, -1), T
        return gather_rows(cache, page_table, new), st.past + T

    def fox(st, y, o):
        B, T = st.B, st.T
        q, k, v, fl = _split(y, [N_HEADS * HEAD_DIM, KV_DIM, KV_DIM, N_HEADS])
        q = _head_norm(q.reshape(B, T, N_HEADS, HEAD_DIM), fox_q_norm)
        k = _head_norm(k.reshape(B, T, N_KV_HEADS, HEAD_DIM), fox_k_norm)
        v = v.reshape(B, T, N_KV_HEADS, HEAD_DIM)
        logf = jax.nn.log_sigmoid(fl.reshape(B, T, N_HEADS) + fox_b_f)
        o.update(fox_k=k, fox_v=v, fox_logf=logf)
        K, L = keys_of(st, k, cache_fox_k)
        V, _ = keys_of(st, v, cache_fox_v)
        LF, _ = keys_of(st, logf, cache_fox_logf)
        F = cumsum_rows(LF) * LOG2E
        fq = _rows_gt(lax.slice_in_dim(F, st.past, st.past + T, axis=1), st.tq)
        return _unpad_out(_attend(st, q, K, V, L, _kv_tile(K.shape[1]), mode='fox', fq=fq,
                                  fkT=F.transpose(0, 2, 1)), st.tq)

    def moba(st, y, o):
        B, T = st.B, st.T
        q, k, v = _split(y, [N_HEADS * HEAD_DIM, KV_DIM, KV_DIM])
        q = _rope(_head_norm(q.reshape(B, T, N_HEADS, HEAD_DIM), moba_q_norm), st.pos)
        k = _rope(_head_norm(k.reshape(B, T, N_KV_HEADS, HEAD_DIM), moba_k_norm), st.pos)
        v = v.reshape(B, T, N_KV_HEADS, HEAD_DIM)
        o.update(moba_k=k, moba_v=v)
        K, L = keys_of(st, k, cache_moba_k)
        V, _ = keys_of(st, v, cache_moba_v)
        kbar = block_means(K, MOBA_BLOCK)
        NB = kbar.shape[1]
        NBP = _round_up(NB, LANES)
        kbt = jnp.tile(kbar.reshape(B, NB, N_KV_HEADS, 1, HEAD_DIM), (1, 1, 1, GROUP, 1))
        kbt = kbt.reshape(B, NB, N_KV_HEADS, KV_DIM).transpose(0, 2, 3, 1)
        kbt = jnp.pad(kbt, ((0, 0), (0, 0), (0, 0), (0, NBP - NB)))
        sel = moba_select(q.reshape(B, T, -1), kbt, tq=st.tq, qpos_base=st.past)
        return _unpad_out(_attend(st, q, K, V, L, _kv_tile(K.shape[1]), mode='blocks', sel=sel,
                                  blk=MOBA_BLOCK), st.tq)

    def compress(st, raw_new, cache, pe, w1, w2):
        B, T = st.B, st.T
        if st.past == 0:
            lo, hi = compress_lohi(raw_new.reshape(B * T, KV_DIM), pe, w1)
            nblk = T // CMP_STRIDE
            lo, hi = lo.reshape(B, nblk, KV_DIM), hi.reshape(B, nblk, KV_DIM)
            L = T
        else:
            P = cache.shape[0]
            lo_p, hi_p = compress_lohi(cache.reshape(P * page, KV_DIM), pe, w1)
            tail = jnp.pad(raw_new.reshape(B, T, KV_DIM), ((0, 0), (0, page - T), (0, 0)))
            lo_t, hi_t = compress_lohi(tail.reshape(B * page, KV_DIM), pe, w1)
            bpp = page // CMP_STRIDE
            lo = gather_pages(lo_p.reshape(P, bpp, KV_DIM), page_table, lo_t.reshape(B, bpp, KV_DIM))
            hi = gather_pages(hi_p.reshape(P, bpp, KV_DIM), page_table, hi_t.reshape(B, bpp, KV_DIM))
            L = st.past + T
        nc = _round_up(L, SLC_BLOCK) // CMP_STRIDE - 1
        NCP = _round_up(nc, LANES)
        lo = _pad_len(lo, NCP + 1)
        hi = _pad_len(hi, NCP + 1)
        out = compress_out(lo[:, :NCP].reshape(B * NCP, KV_DIM), hi[:, 1:NCP + 1].reshape(B * NCP, KV_DIM), w2)
        return out.reshape(B, NCP, KV_DIM), nc

    def nsa(st, y, o, win_k, win_v):
        B, T = st.B, st.T
        q, kc, vc, ks, vs, kw, vw, gl = _split(y, [N_HEADS * HEAD_DIM] + [KV_DIM] * 6 + [3 * N_HEADS])
        heads = lambda a: a.reshape(B, T, N_KV_HEADS, HEAD_DIM)
        q = _rope(_head_norm(q.reshape(B, T, N_HEADS, HEAD_DIM), nsa_q_norm), st.pos)
        ks = _rope(_head_norm(heads(ks), nsa_ks_norm), st.pos)
        kw = _rope(_head_norm(heads(kw), nsa_kw_norm), st.pos)
        kc, vc, vs, vw = heads(kc), heads(vc), heads(vs), heads(vw)
        gates = jax.nn.sigmoid(gl).reshape(B, T, N_HEADS, 3)
        o.update(nsa_cmp_k=kc, nsa_cmp_v=vc, nsa_slc_k=ks, nsa_slc_v=vs)
        qp = _qpad(q, st.tq)
        kcc, nc = compress(st, kc, cache_nsa_cmp_k, nsa_pe_k, nsa_w1_k, nsa_w2_k)
        vcc, _ = compress(st, vc, cache_nsa_cmp_v, nsa_pe_v, nsa_w1_v, nsa_w2_v)
        NCP = kcc.shape[1]
        cend = jnp.arange(NCP) * CMP_STRIDE + CMP_LEN - 1
        kcc = _rope(_head_norm(kcc.reshape(B, NCP, N_KV_HEADS, HEAD_DIM), nsa_kc_norm), cend)
        KS, L = keys_of(st, ks, cache_nsa_slc_k)
        VS, _ = keys_of(st, vs, cache_nsa_slc_v)
        nbs = _round_up(L, SLC_BLOCK) // SLC_BLOCK
        NBP = _round_up(nbs, LANES)
        ci = np.arange(NCP)[:, None]
        sj = np.arange(NBP)[None, :]
        cover = ((ci * CMP_STRIDE < (sj + 1) * SLC_BLOCK) & (ci * CMP_STRIDE + CMP_LEN > sj * SLC_BLOCK)
                 & (ci < nc) & (sj < nbs))
        oc, sel = nsa_compressed(qp, kcc.reshape(B, NCP, KV_DIM).astype(BF16), vcc.astype(BF16),
                                 jnp.asarray(cover, BF16), tq=st.tq, qpos_base=st.past, nc=nc)
        osl = flash_attention(qp, KS, VS, tq=st.tq, kt=_kv_tile(KS.shape[1]), qpos_base=st.past, kv_len=L,
                              mode='blocks', sel=sel, blk=SLC_BLOCK)
        if st.past == 0:
            KW, VW, kbase = kw.reshape(B, T, KV_DIM), vw.reshape(B, T, KV_DIM), 0
            wb = min(WINDOW, T)
            o.update(nsa_win_k=kw[:, T - wb:], nsa_win_v=vw[:, T - wb:])
        else:
            KWf = jnp.concatenate([win_k, kw], axis=1)
            VWf = jnp.concatenate([win_v, vw], axis=1)
            lw = KWf.shape[1]
            wb = min(WINDOW, lw)
            o.update(nsa_win_k=KWf[:, lw - wb:], nsa_win_v=VWf[:, lw - wb:])
            kbase = st.past - win_k.shape[1]
            Lp = _round_up(lw, LANES)
            KW = _pad_len(KWf.reshape(B, lw, KV_DIM), Lp)
            VW = _pad_len(VWf.reshape(B, lw, KV_DIM), Lp)
        ow = flash_attention(qp, KW, VW, tq=st.tq, kt=_kv_tile(KW.shape[1]), qpos_base=st.past,
                             kpos_base=kbase, kv_len=KW.shape[1], mode='plain', window=WINDOW)
        oc, osl, ow = (_unpad_out(a, st.tq) for a in (oc, osl, ow))
        return gates[..., 0:1] * oc + gates[..., 1:2] * osl + gates[..., 2:3] * ow

    def dsa(st, y, o):
        B, T, tq = st.B, st.T, st.tq
        q, k, v, qi, ki, wi = _split(y, [N_HEADS * HEAD_DIM, KV_DIM, KV_DIM, IDX_HEADS * IDX_DIM, IDX_DIM, IDX_HEADS])
        q = _rope(_head_norm(q.reshape(B, T, N_HEADS, HEAD_DIM), dsa_q_norm), st.pos)
        k = _rope(_head_norm(k.reshape(B, T, N_KV_HEADS, HEAD_DIM), dsa_k_norm), st.pos)
        v = v.reshape(B, T, N_KV_HEADS, HEAD_DIM)
        qi = _rope(qi.reshape(B, T, IDX_HEADS, IDX_DIM), st.pos)
        ki = _rope(ki.reshape(B, T, 1, IDX_DIM), st.pos)[:, :, 0]
        wi = wi * IDX_HEADS ** -0.5
        o.update(dsa_k=k, dsa_v=v, dsa_idx=ki)
        K, L = keys_of(st, k, cache_dsa_k)
        V, _ = keys_of(st, v, cache_dsa_v)
        KI, _ = keys_of(st, ki, cache_dsa_idx_k)
        kt = _kv_tile(K.shape[1])
        nQ = T // tq
        qi_r = qi.astype(BF16).reshape(B, nQ, tq, IDX_HEADS, IDX_DIM).transpose(0, 1, 3, 2, 4)
        qi_r = qi_r.reshape(B, nQ, IDX_HEADS * tq, IDX_DIM)
        wi_r = wi.reshape(B, nQ, tq, IDX_HEADS)
        n_sel = min(DSA_TOPK, L // 4)
        thr, need = dsa_threshold(qi_r, wi_r, KI, tq=tq, kt=kt, qpos_base=st.past, n_sel=n_sel)
        return _unpad_out(_attend(st, q, K, V, L, kt, mode='dsa', dsa=(qi_r, wi_r, KI, thr, need)), tq)

    for layer in range(ada_w.shape[0]):
        kind = ('fox', 'moba', 'nsa', 'dsa')[layer % 4]
        for s, st in enumerate(streams):
            x = xs[s]
            sh1, sc1, g1, sh2, sc2, g2 = _modulation(conds[s], ada_w[layer], ada_b[layer])
            y = fused_matmul(x, w_in[kind], seq_len=st.T, tm=st.tm,
                             norm=(norm_mix[layer][None, :], sc1, sh1)).reshape(st.B, st.T, -1)
            if kind == 'fox':
                att = fox(st, y, outs[s])
            elif kind == 'moba':
                att = moba(st, y, outs[s])
            elif kind == 'nsa':
                att = nsa(st, y, outs[s], state_nsa_win_k, state_nsa_win_v)
            else:
                att = dsa(st, y, outs[s])
            x = fused_matmul(att.reshape(-1, N_HEADS * HEAD_DIM), w_out[kind], seq_len=st.T, tm=st.tm,
                             epi=(x, g1))
            ffn_norm = (norm_ffn[layer][None, :], sc2, sh2)
            coef = route(x, router_w, router_b, ffn_norm, seq_len=st.T, tm=st.tm)
            xs[s] = moe_dense(x, coef, *ffn_norm, g2, moe_w1b[layer], moe_w3b[layer], moe_w2b[layer],
                              seq_len=st.T, tm=2 * st.tm)

    names = ('fox_k', 'fox_v', 'fox_logf', 'moba_k', 'moba_v', 'nsa_cmp_k', 'nsa_cmp_v', 'nsa_slc_k', 'nsa_slc_v',
             'nsa_win_k', 'nsa_win_v', 'dsa_k', 'dsa_v', 'dsa_idx')
    res = [xs[0].reshape(x_prompt.shape), xs[1].reshape(x_sample.shape)]
    for n in names:
        res += [outs[0][n], outs[1][n]]
    return tuple(res)
```

```python
import functools

import numpy as np
import jax
import jax.numpy as jnp
from jax import lax
from jax.experimental import pallas as pl
from jax.experimental.pallas import tpu as pltpu

HEAD_DIM = 64
N_KV_HEADS = 4
GROUP = 4
N_HEADS = N_KV_HEADS * GROUP
KV_DIM = N_KV_HEADS * HEAD_DIM
ROT_DIM = HEAD_DIM // 4
ROPE_THETA = 500000.0
ATTN_SCALE = HEAD_DIM ** -0.5
LOG2E = 1.4426950408889634
MOBA_BLOCK = 256
MOBA_TOPK = 3
CMP_LEN = 32
CMP_STRIDE = 16
CMP_HIDDEN = 64
SLC_BLOCK = 64
SLC_TOPK = 16
WINDOW = 512
IDX_HEADS = 8
IDX_DIM = 64
DSA_TOPK = 256
N_EXPERTS = 16
N_GROUPS = 4
EXPERTS_PER_GROUP = N_EXPERTS // N_GROUPS
TOP_K = 2
EPS = 1e-6
NEG_INF = -1e30

LANES = 128
VMEM_LIMIT = 56 * 1024 * 1024
BF16 = jnp.bfloat16
F32 = jnp.float32
INT_MIN = -2 ** 31


def _round_up(a, m):
    return (a + m - 1) // m * m


def _params(sem):
    return pltpu.CompilerParams(dimension_semantics=sem, vmem_limit_bytes=VMEM_LIMIT)


def _gather_kernel(pt_ref, *refs, pps, n_main):
    page_refs, tail_ref, o_ref = refs[:pps], refs[pps], refs[pps + 1]
    j = pl.program_id(1)

    @pl.when(j < n_main)
    def _():
        for p in range(pps):
            o_ref[p] = page_refs[p][...]

    @pl.when(j == n_main)
    def _():
        o_ref[0] = tail_ref[...]
        for p in range(1, pps):
            o_ref[p] = jnp.zeros(o_ref.shape[1:], o_ref.dtype)


def gather_pages(pool, page_table, tail):
    _, R, C = pool.shape
    B, n_pages = page_table.shape
    pps = max(d for d in (8, 4, 2, 1) if n_pages % d == 0)
    n_main = n_pages // pps

    def page_map(b, j, pt, p):
        return (pt[b, jnp.minimum(j * pps + p, n_pages - 1)], 0, 0)

    in_specs = [pl.BlockSpec((None, R, C), functools.partial(page_map, p=p)) for p in range(pps)]
    in_specs.append(pl.BlockSpec((None, R, C), lambda b, j, pt: (b, 0, 0)))
    out = pl.pallas_call(
        functools.partial(_gather_kernel, pps=pps, n_main=n_main), name='gather_pages',
        grid_spec=pltpu.PrefetchScalarGridSpec(
            num_scalar_prefetch=1, grid=(B, n_main + 1), in_specs=in_specs,
            out_specs=pl.BlockSpec((None, pps, R, C), lambda b, j, pt: (b, j, 0, 0))),
        out_shape=jax.ShapeDtypeStruct((B, (n_main + 1) * pps, R, C), pool.dtype),
        compiler_params=_params(("arbitrary", "arbitrary")),
    )(page_table, *([pool] * pps), tail)
    return out.reshape(B, (n_main + 1) * pps * R, C)


def _gather_t_kernel(pt_ref, *refs, pps, n_main, page):
    page_refs, tail_ref, o_ref = refs[:pps], refs[pps], refs[pps + 1]
    j = pl.program_id(1)

    @pl.when(j < n_main)
    def _():
        for p in range(pps):
            o_ref[:, p * page:(p + 1) * page] = page_refs[p][...]

    @pl.when(j == n_main)
    def _():
        o_ref[:, :page] = tail_ref[...]
        o_ref[:, page:] = jnp.zeros((o_ref.shape[0], (pps - 1) * page), o_ref.dtype)


def pool_t(pool):
    P, page = pool.shape[:2]
    return jnp.moveaxis(pool.reshape(P, page, -1), 1, 2)


def gather_rows_t(pool, page_table, new_rows):
    P, page = pool.shape[:2]
    B, n_pages = page_table.shape
    T = new_rows.shape[1]
    pt = pool_t(pool)
    C = pt.shape[1]
    tail = jnp.pad(new_rows.reshape(B, T, C), ((0, 0), (0, page - T), (0, 0))).transpose(0, 2, 1)
    pps = max(d for d in (8, 4, 2, 1) if n_pages % d == 0)
    n_main = n_pages // pps

    def page_map(b, j, tab, p):
        return (tab[b, jnp.minimum(j * pps + p, n_pages - 1)], 0, 0)

    in_specs = [pl.BlockSpec((None, C, page), functools.partial(page_map, p=p)) for p in range(pps)]
    in_specs.append(pl.BlockSpec((None, C, page), lambda b, j, tab: (b, 0, 0)))
    return pl.pallas_call(
        functools.partial(_gather_t_kernel, pps=pps, n_main=n_main, page=page), name='gather_pages_t',
        grid_spec=pltpu.PrefetchScalarGridSpec(
            num_scalar_prefetch=1, grid=(B, n_main + 1), in_specs=in_specs,
            out_specs=pl.BlockSpec((None, C, pps * page), lambda b, j, tab: (b, 0, j))),
        out_shape=jax.ShapeDtypeStruct((B, C, (n_main + 1) * pps * page), pool.dtype),
        compiler_params=_params(("arbitrary", "arbitrary")),
    )(page_table, *([pt] * pps), tail)


def _adanorm(x, g, sc, sh):
    y = x * lax.rsqrt(jnp.mean(x * x, axis=-1, keepdims=True) + EPS)
    return (y * g) * (1.0 + sc) + sh


def _matmul_kernel(*refs, has_norm, has_epi):
    it = iter(refs)
    x_ref, w_ref = next(it), next(it)
    x = x_ref[...]
    if has_norm:
        g_ref, sc_ref, sh_ref = next(it), next(it), next(it)
        x = _adanorm(x, g_ref[...], sc_ref[...], sh_ref[...])
    acc = jnp.dot(x.astype(BF16), w_ref[...], preferred_element_type=F32)
    if has_epi:
        res_ref, gate_ref = next(it), next(it)
        acc = res_ref[...] + gate_ref[...] * acc
    o_ref = next(it)
    o_ref[...] = acc


def _mod_spec(mod, tm, seq_len):
    D = mod.shape[-1]
    if seq_len % tm == 0:
        tps = seq_len // tm
        return mod, pl.BlockSpec((None, 1, D), lambda i: (i // tps, 0, 0))
    rows = jnp.broadcast_to(mod, (mod.shape[0], seq_len, D)).reshape(-1, D)
    return rows, pl.BlockSpec((tm, D), lambda i: (i, 0))


def fused_matmul(x, w, *, seq_len, tm, norm=None, epi=None):
    N, K = x.shape
    M = w.shape[1]
    tm = min(tm, N)
    args = [x, w]
    specs = [pl.BlockSpec((tm, K), lambda i: (i, 0)), pl.BlockSpec((K, M), lambda i: (0, 0))]
    if norm is not None:
        gain, sc, sh = norm
        args.append(gain)
        specs.append(pl.BlockSpec((1, K), lambda i: (0, 0)))
        for m in (sc, sh):
            a, s = _mod_spec(m, tm, seq_len)
            args.append(a)
            specs.append(s)
    if epi is not None:
        res, gate = epi
        args.append(res)
        specs.append(pl.BlockSpec((tm, M), lambda i: (i, 0)))
        a, s = _mod_spec(gate, tm, seq_len)
        args.append(a)
        specs.append(s)
    return pl.pallas_call(
        functools.partial(_matmul_kernel, has_norm=norm is not None, has_epi=epi is not None),
        name='proj_in' if norm is not None else 'proj_out',
        grid=(N // tm,), in_specs=specs,
        out_specs=pl.BlockSpec((tm, M), lambda i: (i, 0)),
        out_shape=jax.ShapeDtypeStruct((N, M), F32),
        compiler_params=_params(("arbitrary",)),
    )(*args)


def _proj_heads_kernel(*refs, n_branch, tq):
    it = iter(refs)
    o_refs = [next(it) for _ in range(n_branch)]
    gts_ref = next(it) if n_branch > 1 else None
    w_ref, res_ref, gate_ref, out_ref = next(it), next(it), next(it), next(it)
    R = GROUP * tq
    if n_branch == 1:
        o = o_refs[0][...]
    else:
        gts = gts_ref[...]
        o = None
        for c in range(n_branch):
            gfull = jnp.zeros((R, KV_DIM), F32)
            for h in range(N_KV_HEADS):
                col = h * n_branch + c
                gfull = jnp.where(_lane_head_mask(R, h), gts[:, col:col + 1], gfull)
            o = gfull * o_refs[c][...] if o is None else o + gfull * o_refs[c][...]
    ob = o.astype(BF16)
    acc = jnp.zeros(out_ref.shape, F32)
    for g in range(GROUP):
        acc += jnp.dot(ob[g * tq:(g + 1) * tq], w_ref[g], preferred_element_type=F32)
    out_ref[...] = res_ref[...] + gate_ref[...] * acc


def proj_heads(branches, w_o, res, gate, *, tq, gts=None):
    B, nQ, R, _ = branches[0].shape
    D = w_o.shape[1]
    nb = len(branches)
    w = w_o.reshape(N_KV_HEADS, GROUP, HEAD_DIM, D).transpose(1, 0, 2, 3).reshape(GROUP, KV_DIM, D)
    tile = lambda b, i: (b, i, 0, 0)
    args = list(branches)
    specs = [pl.BlockSpec((None, None, R, KV_DIM), tile)] * nb
    if nb > 1:
        args.append(gts)
        specs.append(pl.BlockSpec((None, None, R, gts.shape[-1]), tile))
    args += [w, res.reshape(B, nQ, tq, D), gate]
    specs += [pl.BlockSpec((GROUP, KV_DIM, D), lambda b, i: (0, 0, 0)),
              pl.BlockSpec((None, None, tq, D), tile), pl.BlockSpec((None, 1, D), lambda b, i: (b, 0, 0))]
    out = pl.pallas_call(
        functools.partial(_proj_heads_kernel, n_branch=nb, tq=tq), name='proj_heads',
        grid=(B, nQ), in_specs=specs,
        out_specs=pl.BlockSpec((None, None, tq, D), tile),
        out_shape=jax.ShapeDtypeStruct((B, nQ, tq, D), F32),
        compiler_params=_params(("arbitrary", "arbitrary")),
    )(*args)
    return out.reshape(B * nQ * tq, D)


ROUTE_ROWS = 8


def _first_match(vals, target):
    out, seen = [], None
    for v in vals:
        hit = v == target
        out.append(hit if seen is None else hit & jnp.logical_not(seen))
        seen = hit if seen is None else seen | hit
    return out


def _router_kernel(x_ref, g_ref, sc_ref, sh_ref, wt_ref, b_ref, o_ref):
    h = _adanorm(x_ref[...], g_ref[...], sc_ref[...], sh_ref[...]).astype(BF16)
    logit = lax.dot_general(wt_ref[...], h, (((1,), (1,)), ((), ())), preferred_element_type=F32)
    aff = jax.nn.sigmoid(logit)
    grp = aff + b_ref[...]
    n = EXPERTS_PER_GROUP
    a = [grp[k * ROUTE_ROWS:(k + 1) * ROUTE_ROWS] for k in range(n)]
    f = [aff[k * ROUTE_ROWS:(k + 1) * ROUTE_ROWS] for k in range(n)]
    top1 = functools.reduce(jnp.maximum, a)
    s1 = _first_match(a, top1)
    rest = [jnp.where(s, -jnp.inf, v) for s, v in zip(s1, a)]
    top2 = functools.reduce(jnp.maximum, rest)
    s2 = _first_match(rest, top2)
    row = lax.broadcasted_iota(jnp.int32, top1.shape, 0).astype(F32)
    score = jnp.where(row < N_GROUPS, top1 + top2, -jnp.inf)
    best = jnp.max(score, axis=0, keepdims=True)
    gsel = row == jnp.min(jnp.where(score == best, row, 1e9), axis=0, keepdims=True)
    w1 = sum(jnp.where(s, v, 0.0) for s, v in zip(s1, f))
    w2 = sum(jnp.where(s, v, 0.0) for s, v in zip(s2, f))
    den = w1 + w2
    for k in range(n):
        o_ref[k * ROUTE_ROWS:(k + 1) * ROUTE_ROWS, :] = jnp.where(
            gsel & s1[k], w1 / den, jnp.where(gsel & s2[k], w2 / den, 0.0))


def route(x, router_w, router_b, norm, *, seq_len, tm):
    assert TOP_K == 2
    N, D = x.shape
    tm = min(tm, N)
    wt = router_w.T.reshape(N_GROUPS, EXPERTS_PER_GROUP, D).transpose(1, 0, 2)
    wt = jnp.pad(wt, ((0, 0), (0, ROUTE_ROWS - N_GROUPS), (0, 0))).reshape(-1, D).astype(BF16)
    bt = router_b.reshape(N_GROUPS, EXPERTS_PER_GROUP).T
    bt = jnp.pad(bt, ((0, 0), (0, ROUTE_ROWS - N_GROUPS))).reshape(-1, 1)
    rows = wt.shape[0]
    gain, sc, sh = norm
    args, specs = [x, gain], [pl.BlockSpec((tm, D), lambda i: (i, 0)), pl.BlockSpec((1, D), lambda i: (0, 0))]
    for m in (sc, sh):
        a, s = _mod_spec(m, tm, seq_len)
        args.append(a)
        specs.append(s)
    args += [wt, bt]
    specs += [pl.BlockSpec((rows, D), lambda i: (0, 0)), pl.BlockSpec((rows, 1), lambda i: (0, 0))]
    coef_t = pl.pallas_call(
        _router_kernel, name='router', grid=(N // tm,), in_specs=specs,
        out_specs=pl.BlockSpec((rows, tm), lambda i: (0, i)),
        out_shape=jax.ShapeDtypeStruct((rows, N), F32),
        compiler_params=_params(("arbitrary",)),
    )(*args)
    coef_t = coef_t.reshape(EXPERTS_PER_GROUP, ROUTE_ROWS, N)[:, :N_GROUPS]
    return coef_t.transpose(2, 1, 0).reshape(N, N_EXPERTS)


def _cumsum_kernel(x_ref, o_ref, carry_ref, *, tc):
    @pl.when(pl.program_id(0) == 0)
    def _():
        carry_ref[...] = jnp.zeros_like(carry_ref)

    r = lax.broadcasted_iota(jnp.int32, (tc, tc), 0)
    c = lax.broadcasted_iota(jnp.int32, (tc, tc), 1)
    out = jnp.dot(x_ref[...], (r <= c).astype(F32), preferred_element_type=F32,
                  precision=lax.Precision.HIGHEST) + carry_ref[...]
    o_ref[...] = out
    carry_ref[...] = out[:, tc - 1:tc]


def cumsum_lanes(x, tc=256):
    rows, L = x.shape
    tc = min(tc, L)
    return pl.pallas_call(
        functools.partial(_cumsum_kernel, tc=tc), name='cumsum',
        grid=(L // tc,),
        in_specs=[pl.BlockSpec((rows, tc), lambda j: (0, j))],
        out_specs=pl.BlockSpec((rows, tc), lambda j: (0, j)),
        out_shape=jax.ShapeDtypeStruct((rows, L), F32),
        scratch_shapes=[pltpu.VMEM((rows, 1), F32)],
        compiler_params=_params(("arbitrary",)),
    )(x)


def _select_blocks(score, own, n_top):
    col = lax.broadcasted_iota(jnp.int32, score.shape, 1)
    colf = col.astype(F32)
    sc = jnp.where(col < own, score, -jnp.inf)
    sel = col == own
    for _ in range(n_top):
        m = jnp.max(sc, axis=-1, keepdims=True)
        idx = jnp.min(jnp.where(sc == m, colf, 1e9), axis=-1, keepdims=True)
        hit = colf == idx
        sel = sel | (hit & (m > -jnp.inf))
        sc = jnp.where(hit, -jnp.inf, sc)
    return sel


def _row_pos(tq, rows, base):
    assert tq & (tq - 1) == 0
    r = lax.broadcasted_iota(jnp.int32, (rows, 1), 0)
    return base + (r & (tq - 1))


def _moba_select_kernel(q_ref, kb_ref, o_ref, *, tq, qpos_base):
    qpos = _row_pos(tq, tq, qpos_base + pl.program_id(1) * tq)
    own = lax.shift_right_logical(qpos, int(np.log2(MOBA_BLOCK)))
    for h in range(N_KV_HEADS):
        gate = jnp.dot(q_ref[:, h * KV_DIM:(h + 1) * KV_DIM], kb_ref[h], preferred_element_type=F32)
        o_ref[h] = _select_blocks(gate, own, MOBA_TOPK).astype(F32)


def moba_select(q_nat, kbt, *, tq, qpos_base):
    B, T, D = q_nat.shape
    NBP = kbt.shape[-1]
    return pl.pallas_call(
        functools.partial(_moba_select_kernel, tq=tq, qpos_base=qpos_base), name='moba_select',
        grid=(B, T // tq),
        in_specs=[pl.BlockSpec((None, tq, D), lambda b, i: (b, i, 0)),
                  pl.BlockSpec((None, N_KV_HEADS, KV_DIM, NBP), lambda b, i: (b, 0, 0, 0))],
        out_specs=pl.BlockSpec((None, N_KV_HEADS, tq, NBP), lambda b, i: (b, 0, i, 0)),
        out_shape=jax.ShapeDtypeStruct((B, N_KV_HEADS, T, NBP), F32),
        compiler_params=_params(("arbitrary", "arbitrary")),
    )(q_nat, kbt)


def _block_mean_kernel(k_ref, o_ref, *, tl, blk):
    j = pl.program_id(1)

    @pl.when(j == 0)
    def _():
        o_ref[...] = jnp.zeros_like(o_ref)

    nbp = o_ref.shape[-1]
    key_blk = lax.shift_right_logical(j * tl + lax.broadcasted_iota(jnp.int32, (tl, nbp), 0), int(np.log2(blk)))
    avg = jnp.where(key_blk == lax.broadcasted_iota(jnp.int32, (tl, nbp), 1), 1.0 / blk, 0.0)
    o_ref[...] += jnp.dot(k_ref[...], avg, preferred_element_type=F32, precision=lax.Precision.HIGHEST)


def block_means_t(kt_, blk, nbp):
    B, C, L = kt_.shape
    tl = max(d for d in (1024, 512, 256, 128) if L % d == 0)
    return pl.pallas_call(
        functools.partial(_block_mean_kernel, tl=tl, blk=blk), name='block_means',
        grid=(B, L // tl),
        in_specs=[pl.BlockSpec((None, C, tl), lambda b, j: (b, 0, j))],
        out_specs=pl.BlockSpec((None, C, nbp), lambda b, j: (b, 0, 0)),
        out_shape=jax.ShapeDtypeStruct((B, C, nbp), F32),
        compiler_params=_params(("arbitrary", "arbitrary")),
    )(kt_)


def _compress_lohi_kernel(x_ref, pe_lo_ref, pe_hi_ref, wlo_ref, whi_ref, lo_ref, hi_ref, *scratch,
                          nblk, pool_major):
    if pool_major:
        xa_ref, xb_ref = scratch
        page = x_ref.shape[-1]
        for p in range(x_ref.shape[0]):
            xt = x_ref[p].T
            xa_ref[p * page:(p + 1) * page, :] = xt[:, :LANES]
            xb_ref[p * page:(p + 1) * page, :] = xt[:, LANES:]
    lo = jnp.zeros((nblk, KV_DIM), F32)
    hi = jnp.zeros((nblk, KV_DIM), F32)
    for t in range(CMP_STRIDE):
        if pool_major:
            x = jnp.concatenate([xa_ref[pl.ds(t, nblk, stride=CMP_STRIDE), :],
                                 xb_ref[pl.ds(t, nblk, stride=CMP_STRIDE), :]], axis=-1)
        else:
            x = jnp.concatenate([x_ref[pl.ds(2 * t, nblk, stride=2 * CMP_STRIDE), :],
                                 x_ref[pl.ds(2 * t + 1, nblk, stride=2 * CMP_STRIDE), :]], axis=-1)
        lo += jnp.dot((x + pe_lo_ref[t:t + 1, :]).astype(BF16), wlo_ref[t], preferred_element_type=F32)
        hi += jnp.dot((x + pe_hi_ref[t:t + 1, :]).astype(BF16), whi_ref[t], preferred_element_type=F32)
    lo_ref[...] = lo
    hi_ref[...] = hi


def compress_lohi(raw, pe, w1, tr=2048, pool_major=False):
    if pool_major:
        P, _, page = raw.shape
        N = P * page
    else:
        N = raw.shape[0]
    tr = max(d for d in (tr, tr // 2, tr // 4, tr // 8, LANES) if N % d == 0)
    nblk = tr // CMP_STRIDE
    if pool_major:
        x_arg = raw
        x_spec = pl.BlockSpec((tr // page, KV_DIM, page), lambda i: (i, 0, 0))
        scratch = [pltpu.VMEM((tr, LANES), F32), pltpu.VMEM((tr, LANES), F32)]
    else:
        x_arg = raw.reshape(2 * N, LANES)
        x_spec = pl.BlockSpec((2 * tr, LANES), lambda i: (i, 0))
        scratch = []
    eye = jnp.eye(N_KV_HEADS, dtype=F32)

    def blockdiag(w):
        w = w.reshape(CMP_STRIDE, HEAD_DIM, CMP_HIDDEN)
        return jnp.einsum('tdc,hk->thdkc', w, eye).reshape(CMP_STRIDE, KV_DIM, KV_DIM).astype(BF16)

    half = CMP_STRIDE * HEAD_DIM
    pe_lo = jnp.tile(pe[:CMP_STRIDE], (1, N_KV_HEADS))
    pe_hi = jnp.tile(pe[CMP_STRIDE:], (1, N_KV_HEADS))
    const = lambda i: (0, 0)
    const3 = lambda i: (0, 0, 0)
    return pl.pallas_call(
        functools.partial(_compress_lohi_kernel, nblk=nblk, pool_major=pool_major), name='compress_lohi',
        grid=(N // tr,),
        in_specs=[x_spec,
                  pl.BlockSpec((CMP_STRIDE, KV_DIM), const), pl.BlockSpec((CMP_STRIDE, KV_DIM), const),
                  pl.BlockSpec((CMP_STRIDE, KV_DIM, KV_DIM), const3),
                  pl.BlockSpec((CMP_STRIDE, KV_DIM, KV_DIM), const3)],
        out_specs=[pl.BlockSpec((nblk, KV_DIM), lambda i: (i, 0))] * 2,
        out_shape=[jax.ShapeDtypeStruct((N // CMP_STRIDE, KV_DIM), F32)] * 2,
        scratch_shapes=scratch,
        compiler_params=_params(("arbitrary",)),
    )(x_arg, pe_lo, pe_hi, blockdiag(w1[:half]), blockdiag(w1[half:]))


def _compress_out_kernel(lo_ref, hi_ref, w_ref, o_ref):
    z = lo_ref[...] + hi_ref[...]
    z = z * jax.nn.sigmoid(z)
    o_ref[...] = jnp.dot(z.astype(BF16), w_ref[...], preferred_element_type=F32)


def compress_out(lo, hi_next, w2):
    N = lo.shape[0]
    tm = max(d for d in (512, 256, 128, 64, 32, 16, 8) if N % d == 0)
    w = jnp.einsum('cd,hk->hckd', w2, jnp.eye(N_KV_HEADS, dtype=F32)).reshape(KV_DIM, KV_DIM).astype(BF16)
    return pl.pallas_call(
        _compress_out_kernel, name='compress_out', grid=(N // tm,),
        in_specs=[pl.BlockSpec((tm, KV_DIM), lambda i: (i, 0)), pl.BlockSpec((tm, KV_DIM), lambda i: (i, 0)),
                  pl.BlockSpec((KV_DIM, KV_DIM), lambda i: (0, 0))],
        out_specs=pl.BlockSpec((tm, KV_DIM), lambda i: (i, 0)),
        out_shape=jax.ShapeDtypeStruct((N, KV_DIM), F32),
        compiler_params=_params(("arbitrary",)),
    )(lo, hi_next, w)


def _lane_head_mask(rows, h):
    lane = lax.broadcasted_iota(jnp.int32, (rows, KV_DIM), 1)
    return (lane >= h * HEAD_DIM) & (lane < (h + 1) * HEAD_DIM)


def _spread_head(qh, h):
    d = lax.broadcasted_iota(jnp.int32, (HEAD_DIM, KV_DIM), 0)
    lane = lax.broadcasted_iota(jnp.int32, (HEAD_DIM, KV_DIM), 1)
    place = (lane == d + h * HEAD_DIM).astype(BF16)
    return jnp.dot(qh, place, preferred_element_type=F32).astype(BF16)


def _nsa_cmp_kernel(q_ref, kc_ref, vc_ref, cover_ref, o_ref, sel_ref, *, tq, qpos_base, nc):
    R = GROUP * tq
    NCP = kc_ref.shape[0]
    q0 = qpos_base + pl.program_id(1) * tq
    qpos = _row_pos(tq, R, q0)
    c = lax.broadcasted_iota(jnp.int32, (1, NCP), 1)
    mask = (c * CMP_STRIDE + (CMP_LEN - 1) <= qpos) & (c < nc)
    kc = kc_ref[...]
    vc = vc_ref[...]
    own = lax.shift_right_logical(_row_pos(tq, tq, q0), int(np.log2(SLC_BLOCK)))
    out = jnp.zeros((R, KV_DIM), F32)
    for h in range(N_KV_HEADS):
        s = lax.dot_general(_spread_head(q_ref[h], h), kc, (((1,), (1,)), ((), ())),
                            preferred_element_type=F32)
        s = jnp.where(mask, s, NEG_INF)
        m = jnp.max(s, axis=-1, keepdims=True)
        e = jnp.where(mask, jnp.exp2(s - m), 0.0)
        l = jnp.sum(e, axis=-1, keepdims=True)
        p = (e / jnp.where(l > 0.0, l, 1.0)).astype(BF16)
        out = jnp.where(_lane_head_mask(R, h), jnp.dot(p, vc, preferred_element_type=F32), out)
        imp = jnp.dot(p, cover_ref[...], preferred_element_type=F32)
        imp = jnp.sum(imp.reshape(GROUP, tq, imp.shape[-1]), axis=0)
        sel_ref[h] = _select_blocks(imp, own, SLC_TOPK - 1).astype(F32)
    o_ref[...] = out


def nsa_compressed(qpad, kcc, vcc, cover, *, tq, qpos_base, nc):
    B, nQ, _, R, _ = qpad.shape
    NCP, NBP = cover.shape
    return pl.pallas_call(
        functools.partial(_nsa_cmp_kernel, tq=tq, qpos_base=qpos_base, nc=nc), name='nsa_compressed',
        grid=(B, nQ),
        in_specs=[pl.BlockSpec((None, None, N_KV_HEADS, R, HEAD_DIM), lambda b, i: (b, i, 0, 0, 0)),
                  pl.BlockSpec((None, NCP, KV_DIM), lambda b, i: (b, 0, 0)),
                  pl.BlockSpec((None, NCP, KV_DIM), lambda b, i: (b, 0, 0)),
                  pl.BlockSpec((NCP, NBP), lambda b, i: (0, 0))],
        out_specs=[pl.BlockSpec((None, None, R, KV_DIM), lambda b, i: (b, i, 0, 0)),
                   pl.BlockSpec((None, N_KV_HEADS, tq, NBP), lambda b, i: (b, 0, i, 0))],
        out_shape=[jax.ShapeDtypeStruct((B, nQ, R, KV_DIM), F32),
                   jax.ShapeDtypeStruct((B, N_KV_HEADS, nQ * tq, NBP), F32)],
        compiler_params=_params(("arbitrary", "arbitrary")),
    )(qpad, kcc, vcc, cover)


def _index_keys(qi, wi, ki, kpos, qpos, tq):
    rel = jnp.maximum(jnp.dot(qi, ki, preferred_element_type=F32), 0.0)
    score = jnp.zeros((tq, rel.shape[1]), F32)
    for i in range(IDX_HEADS):
        score = score + wi[:, i:i + 1] * rel[i * tq:(i + 1) * tq]
    score = jnp.where(kpos <= qpos, score + 0.0, -jnp.inf)
    bits = lax.bitcast_convert_type(score, jnp.int32)
    return bits ^ (lax.shift_right_arithmetic(bits, 31) & 0x7FFFFFFF)


def _dsa_select_kernel(qi_ref, wi_ref, ki_ref, o_ref, key_ref, seen_ref, *, tq, kt, qpos_base, n_sel, nkt):
    q0 = qpos_base + pl.program_id(1) * tq
    qpos = _row_pos(tq, tq, q0)
    last = jnp.minimum((q0 + tq - 1) // kt, nkt - 1)
    lane = lax.broadcasted_iota(jnp.int32, (1, kt), 1)

    for jt in range(nkt):
        @pl.when(jt <= last)
        def _(jt=jt):
            ki = ki_ref[:, jt * kt:(jt + 1) * kt].astype(BF16)
            key_ref[jt] = _index_keys(qi_ref[...], wi_ref[...], ki, jt * kt + lane, qpos, tq)

    def count(pred):
        def body(jt, acc):
            hit = pred(key_ref[jt])
            for u in range(kt // LANES):
                acc = acc + jnp.where(hit[:, u * LANES:(u + 1) * LANES], 1.0, 0.0)
            return acc
        acc = lax.fori_loop(0, last + 1, body, jnp.zeros((tq, LANES), F32))
        return jnp.sum(acc, axis=-1, keepdims=True)

    def count_ge(t):
        return count(lambda key: key >= t)

    def bit_step(it, t_u):
        cand = t_u | lax.shift_left(jnp.int32(1), 31 - it)
        ok = count_ge(cand ^ INT_MIN) >= n_sel
        return jnp.where(ok, cand, t_u)

    t_u = lax.fori_loop(0, 32, bit_step, jnp.zeros((tq, 1), jnp.int32))
    thr = t_u ^ INT_MIN

    need = n_sel - count(lambda key: key > thr)

    r = lax.broadcasted_iota(jnp.int32, (kt, kt), 0)
    c = lax.broadcasted_iota(jnp.int32, (kt, kt), 1)
    tri = (r <= c).astype(F32)
    seen_ref[...] = jnp.zeros((tq, 1), F32)
    for jt in range(nkt):
        @pl.when(jt <= last)
        def _(jt=jt):
            key = key_ref[jt]
            eq = key == thr
            rank = seen_ref[...] + jnp.dot(eq.astype(F32), tri, preferred_element_type=F32)
            keep = ((key > thr) | (eq & (rank <= need))) & (jt * kt + lane <= qpos)
            o_ref[:, jt * kt:(jt + 1) * kt] = jnp.where(keep, 1.0, 0.0).astype(o_ref.dtype)
            seen_ref[...] = rank[:, kt - 1:kt]

        @pl.when(jt > last)
        def _(jt=jt):
            o_ref[:, jt * kt:(jt + 1) * kt] = jnp.zeros((tq, kt), o_ref.dtype)


def dsa_select(qi, wi, ki_t, *, tq, kt, qpos_base, n_sel):
    B, nQ = qi.shape[:2]
    Lk = ki_t.shape[2]
    nkt = Lk // kt
    return pl.pallas_call(
        functools.partial(_dsa_select_kernel, tq=tq, kt=kt, qpos_base=qpos_base, n_sel=n_sel, nkt=nkt),
        name='dsa_select',
        grid=(B, nQ),
        in_specs=[pl.BlockSpec((None, None, IDX_HEADS * tq, IDX_DIM), lambda b, i: (b, i, 0, 0)),
                  pl.BlockSpec((None, None, tq, IDX_HEADS), lambda b, i: (b, i, 0, 0)),
                  pl.BlockSpec((None, IDX_DIM, Lk), lambda b, i: (b, 0, 0))],
        out_specs=pl.BlockSpec((None, tq, Lk), lambda b, i: (b, i, 0)),
        out_shape=jax.ShapeDtypeStruct((B, nQ * tq, Lk), BF16 if tq % 16 == 0 else F32),
        scratch_shapes=[pltpu.VMEM((nkt, tq, kt), jnp.int32), pltpu.VMEM((tq, 1), F32)],
        compiler_params=_params(("arbitrary", "arbitrary")),
    )(qi, wi, ki_t)


def _flash_kernel(si_ref, st_ref, sf_ref, sl_ref, *refs, mode, tq, kt, rc, qpos_base, kpos_base, window, blk,
                  k_major, v_major):
    it = iter(refs)
    q_ref, k_ref, v_ref = next(it), next(it), next(it)
    if mode == 'fox':
        fq_ref, fk_ref = next(it), next(it)
    elif mode == 'blocks':
        sel_ref = next(it)
    elif mode == 'mask':
        mask_ref = next(it)
    o_ref, m_ref, l_ref, acc_ref = next(it), next(it), next(it), next(it)
    s_ref, p_ref, a_ref, bias_ref, qp_ref = next(it), next(it), next(it), next(it), next(it)

    R = GROUP * tq
    step = pl.program_id(1)
    q0 = qpos_base + si_ref[step] * tq
    kpos0 = kpos_base + st_ref[step] * kt

    @pl.when(sf_ref[step] == 1)
    def _():
        m_ref[...] = jnp.full(m_ref.shape, NEG_INF, F32)
        l_ref[...] = jnp.zeros(l_ref.shape, F32)
        acc_ref[...] = jnp.zeros(acc_ref.shape, F32)
        for h in range(N_KV_HEADS):
            qp_ref[h] = _spread_head(q_ref[h], h)

    def tile_mask():
        kpos = kpos0 + lax.broadcasted_iota(jnp.int32, (1, kt), 1)
        qpos = _row_pos(tq, tq, q0)
        if mode == 'mask':
            return mask_ref[...].astype(F32) > 0.5
        mask = kpos <= qpos
        if window is not None:
            mask = mask & (qpos - kpos < window) & (kpos >= 0)
        return mask

    def softmax_head(h, use_bias):
        hb = h if mode == 'blocks' else 0
        for g in range(GROUP):
            if mode == 'fox':
                fk = jnp.broadcast_to(fk_ref[h * GROUP + g:h * GROUP + g + 1, :], (rc, kt))

            def logits(t0):
                u = s_ref[h, g * tq + t0:g * tq + t0 + rc, :]
                if mode == 'fox':
                    u = u - fk
                if use_bias:
                    u = u + bias_ref[hb, t0:t0 + rc, :]
                return u

            for t0 in range(0, tq, rc):
                rows = slice(g * tq + t0, g * tq + t0 + rc)
                row = jnp.max(logits(t0), axis=-1, keepdims=True)
                if mode == 'fox':
                    row = row + fq_ref[h, rows, :]
                m_prev = m_ref[h, rows, :]
                m_new = jnp.maximum(m_prev, row)
                m_ref[h, rows, :] = m_new
                a_ref[h, rows, :] = jnp.exp2(m_prev - m_new)
            for t0 in range(0, tq, rc):
                rows = slice(g * tq + t0, g * tq + t0 + rc)
                shift = -m_ref[h, rows, :]
                if mode == 'fox':
                    shift = shift + fq_ref[h, rows, :]
                p = jnp.exp2(logits(t0) + shift)
                part = p[:, 0:LANES]
                for u in range(1, kt // LANES):
                    part = part + p[:, u * LANES:(u + 1) * LANES]
                l_ref[h, rows, :] = a_ref[h, rows, :] * l_ref[h, rows, :] + part
                p_ref[h, rows, :] = p.astype(p_ref.dtype)

    def heads(use_bias):
        kb = k_ref[...].astype(BF16)
        vb = v_ref[...].astype(BF16)
        if use_bias:
            base = jnp.where(tile_mask(), 0.0, -jnp.inf)
            if mode != 'blocks':
                bias_ref[0] = base
        for h in range(N_KV_HEADS):
            if mode == 'blocks':
                nbp = sel_ref.shape[-1]
                kblk = lax.shift_right_logical(kpos0 + lax.broadcasted_iota(jnp.int32, (nbp, kt), 1),
                                               int(np.log2(blk)))
                expand = (kblk == lax.broadcasted_iota(jnp.int32, (nbp, kt), 0)).astype(F32)
                picked = jnp.dot(sel_ref[h], expand, preferred_element_type=F32) > 0.5
                bias_ref[h] = jnp.where(picked, base, -jnp.inf)
            if k_major:
                s_ref[h] = jnp.dot(qp_ref[h], kb, preferred_element_type=F32)
            else:
                s_ref[h] = lax.dot_general(qp_ref[h], kb, (((1,), (1,)), ((), ())), preferred_element_type=F32)
        for h in range(N_KV_HEADS):
            softmax_head(h, use_bias)
            pb = p_ref[h].astype(BF16)
            if v_major:
                pv = lax.dot_general(pb, vb, (((1,), (1,)), ((), ())), preferred_element_type=F32)
            else:
                pv = jnp.dot(pb, vb, preferred_element_type=F32)
            acc_ref[h] = a_ref[h] * acc_ref[h] + pv

    if mode in ('plain', 'fox') and window is None:
        crosses = kpos0 + kt - 1 > q0
        pl.when(crosses)(lambda: heads(True))
        pl.when(jnp.logical_not(crosses))(lambda: heads(False))
    else:
        heads(True)

    @pl.when(sl_ref[step] == 1)
    def _():
        out = jnp.zeros((R, KV_DIM), F32)
        for h in range(N_KV_HEADS):
            l = jnp.sum(l_ref[h], axis=-1, keepdims=True)
            out = jnp.where(_lane_head_mask(R, h), acc_ref[h] / jnp.where(l > 0.0, l, 1.0), out)
        o_ref[...] = out


def flash_attention(q, k, v, *, tq, kt, qpos_base, kpos_base=0, kv_len, mode='plain', window=None,
                    fq=None, fkT=None, sel=None, blk=None, mask=None, k_major=False, v_major=False):
    B, nQ, _, R, _ = q.shape
    last_tile = (kv_len - 1) // kt

    steps = []
    for i in range(nQ):
        lo = 0 if window is None else max(qpos_base + i * tq - (window - 1) - kpos_base, 0) // kt
        hi = min((qpos_base + (i + 1) * tq - 1 - kpos_base) // kt, last_tile)
        steps += [(i, t, int(t == lo), int(t == hi)) for t in range(lo, hi + 1)]
    tables = [jnp.asarray(np.array(col, np.int32)) for col in zip(*steps)]

    def q_map(b, s, si, st, sf, sl):
        return (b, si[s], 0, 0, 0)

    def kv_map(major):
        if major:
            return lambda b, s, si, st, sf, sl: (b, 0, st[s])
        return lambda b, s, si, st, sf, sl: (b, st[s], 0)

    def kv_spec(major):
        return pl.BlockSpec((None, KV_DIM, kt) if major else (None, kt, KV_DIM), kv_map(major))

    args = [q, k, v]
    specs = [pl.BlockSpec((None, None, N_KV_HEADS, R, HEAD_DIM), q_map), kv_spec(k_major), kv_spec(v_major)]
    rc = min(tq, 16)
    scratch = [pltpu.VMEM((N_KV_HEADS, R, 1), F32), pltpu.VMEM((N_KV_HEADS, R, LANES), F32),
               pltpu.VMEM((N_KV_HEADS, R, KV_DIM), F32),
               pltpu.VMEM((N_KV_HEADS, R, kt), F32),
               pltpu.VMEM((N_KV_HEADS, R, kt), BF16 if rc % 16 == 0 else F32),
               pltpu.VMEM((N_KV_HEADS, R, 1), F32),
               pltpu.VMEM((N_KV_HEADS if mode == 'blocks' else 1, tq, kt), F32),
               pltpu.VMEM((N_KV_HEADS, R, KV_DIM), BF16)]
    if mode == 'fox':
        args += [fq, fkT]
        specs += [pl.BlockSpec((None, None, N_KV_HEADS, R, 1), q_map),
                  pl.BlockSpec((None, N_HEADS, kt), kv_map(True))]
    elif mode == 'blocks':
        NBP = sel.shape[-1]
        args += [sel]
        specs += [pl.BlockSpec((None, N_KV_HEADS, tq, NBP), lambda b, s, si, st, sf, sl: (b, 0, si[s], 0))]
    elif mode == 'mask':
        args += [mask]
        specs += [pl.BlockSpec((None, tq, kt), lambda b, s, si, st, sf, sl: (b, si[s], st[s]))]
    return pl.pallas_call(
        functools.partial(_flash_kernel, mode=mode, tq=tq, kt=kt, rc=rc, qpos_base=qpos_base,
                          kpos_base=kpos_base, window=window, blk=blk, k_major=k_major, v_major=v_major),
        name='flash_' + mode + ('_win' if window else ''),
        grid_spec=pltpu.PrefetchScalarGridSpec(
            num_scalar_prefetch=4, grid=(B, len(steps)), in_specs=specs,
            out_specs=pl.BlockSpec((None, None, R, KV_DIM), lambda b, s, si, st, sf, sl: (b, si[s], 0, 0)),
            scratch_shapes=scratch),
        out_shape=jax.ShapeDtypeStruct((B, nQ, R, KV_DIM), F32),
        compiler_params=_params(("arbitrary", "arbitrary")),
    )(*tables, *args)


def _moe_kernel(x_ref, g_ref, sc_ref, sh_ref, gate_ref, c_ref, w1_ref, w3_ref, w2_ref, o_ref, h_ref, acc_ref):
    e = pl.program_id(1)

    @pl.when(e == 0)
    def _():
        h_ref[...] = _adanorm(x_ref[...], g_ref[...], sc_ref[...], sh_ref[...]).astype(BF16)
        acc_ref[...] = jnp.zeros_like(acc_ref)

    h = h_ref[...]
    a = jnp.dot(h, w1_ref[...], preferred_element_type=F32)
    a = (a * jax.nn.sigmoid(a)) * jnp.dot(h, w3_ref[...], preferred_element_type=F32)
    y = jnp.dot(a.astype(BF16), w2_ref[...], preferred_element_type=F32)
    c = c_ref[...]
    lane = lax.broadcasted_iota(jnp.int32, c.shape, 1)
    ce = jnp.sum(jnp.where(lane == e, c, 0.0), axis=-1, keepdims=True)
    acc_ref[...] += jnp.where(ce != 0.0, ce * y, 0.0)

    @pl.when(e == pl.num_programs(1) - 1)
    def _():
        o_ref[...] = x_ref[...] + gate_ref[...] * acc_ref[...]


def moe_dense(x, coef, gain, sc, sh, gate, w1, w3, w2, *, seq_len, tm):
    N, D = x.shape
    E, _, DE = w1.shape
    tm = min(tm, N)
    row = lambda i, e: (i, 0)
    specs = [pl.BlockSpec((tm, D), row), pl.BlockSpec((1, D), lambda i, e: (0, 0))]
    args = [x, gain]
    for m in (sc, sh, gate):
        a, s = _mod_spec(m, tm, seq_len)
        args.append(a)
        imap = s.index_map
        specs.append(pl.BlockSpec(s.block_shape, lambda i, e, imap=imap: imap(i)))
    args += [coef, w1, w3, w2]
    specs += [pl.BlockSpec((tm, E), row),
              pl.BlockSpec((None, D, DE), lambda i, e: (e, 0, 0)),
              pl.BlockSpec((None, D, DE), lambda i, e: (e, 0, 0)),
              pl.BlockSpec((None, DE, D), lambda i, e: (e, 0, 0))]
    return pl.pallas_call(
        _moe_kernel, name='moe_dense', grid=(N // tm, E), in_specs=specs,
        out_specs=pl.BlockSpec((tm, D), row),
        out_shape=jax.ShapeDtypeStruct((N, D), F32),
        scratch_shapes=[pltpu.VMEM((tm, D), BF16), pltpu.VMEM((tm, D), F32)],
        compiler_params=_params(("arbitrary", "arbitrary")),
    )(*args)


def _head_norm(x, g):
    return x * lax.rsqrt(jnp.mean(x * x, axis=-1, keepdims=True) + EPS) * g


def _rope(x, pos):
    half = ROT_DIM // 2
    inv = ROPE_THETA ** (-jnp.arange(half, dtype=F32) / half)
    ang = pos.astype(F32)[:, None] * inv[None, :]
    cos, sin = jnp.cos(ang)[:, None, :], jnp.sin(ang)[:, None, :]
    x1, x2 = x[..., :half], x[..., half:ROT_DIM]
    return jnp.concatenate([x1 * cos - x2 * sin, x2 * cos + x1 * sin, x[..., ROT_DIM:]], axis=-1)


def _split(y, sizes):
    return jnp.split(y[..., :sum(sizes)], np.cumsum(sizes)[:-1].tolist(), axis=-1)


def _pad_cols(w, mult=256):
    return jnp.pad(w, ((0, 0), (0, _round_up(w.shape[1], mult) - w.shape[1])))


def _qrows(q, tq):
    B, T = q.shape[:2]
    nQ = T // tq
    q = (q * (ATTN_SCALE * LOG2E)).astype(BF16).reshape(B, nQ, tq, N_KV_HEADS, GROUP, HEAD_DIM)
    return q.transpose(0, 1, 3, 4, 2, 5).reshape(B, nQ, N_KV_HEADS, GROUP * tq, HEAD_DIM)


def _gate_rows(gates, tq):
    B, T, _, n = gates.shape
    g = gates.reshape(B, T // tq, tq, N_KV_HEADS, GROUP, n).transpose(0, 1, 4, 2, 3, 5)
    return g.reshape(B, T // tq, GROUP * tq, N_KV_HEADS * n)


def _unpad_out(o, tq):
    B, nQ = o.shape[:2]
    o = o.reshape(B, nQ, GROUP, tq, N_KV_HEADS, HEAD_DIM).transpose(0, 1, 3, 4, 2, 5)
    return o.reshape(B, nQ * tq, N_HEADS, HEAD_DIM)


def _rows_gt(a, tq):
    B, T, _ = a.shape
    a = a.reshape(B, T // tq, tq, N_KV_HEADS, GROUP).transpose(0, 1, 3, 4, 2)
    return a.reshape(B, T // tq, N_KV_HEADS, GROUP * tq, 1)


def _pad_len(a, L):
    return jnp.pad(a, ((0, 0), (0, L - a.shape[1])) + ((0, 0),) * (a.ndim - 2))


def _modulation(c, ada_w, ada_b):
    m = jax.nn.silu(c) @ ada_w + ada_b
    return [a[:, None, :] for a in jnp.split(m, 6, axis=-1)]


class _Stream:
    def __init__(self, B, T, past):
        self.B, self.T, self.past = B, T, past
        self.tq = min(T, 128)
        self.tm = min(B * T, 512)
        self.pos = past + jnp.arange(T)


def _kv_tile(L):
    return max(d for d in (512, 256, 128) if L % d == 0)


def kernel(x_prompt, x_sample, cache_fox_k, cache_fox_v, cache_fox_logf, cache_moba_k, cache_moba_v,
           cache_nsa_cmp_k, cache_nsa_cmp_v, cache_nsa_slc_k, cache_nsa_slc_v, state_nsa_win_k, state_nsa_win_v,
           cache_dsa_k, cache_dsa_v, cache_dsa_idx_k, page_table, c_prompt, c_sample,
           ada_w, ada_b, norm_mix, norm_ffn,
           fox_w_in, fox_b_f, fox_q_norm, fox_k_norm, fox_w_o,
           moba_w_in, moba_q_norm, moba_k_norm, moba_w_o,
           nsa_w_in, nsa_q_norm, nsa_kc_norm, nsa_ks_norm, nsa_kw_norm, nsa_pe_k, nsa_w1_k, nsa_w2_k,
           nsa_pe_v, nsa_w1_v, nsa_w2_v, nsa_w_o,
           dsa_w_in, dsa_q_norm, dsa_k_norm, dsa_w_o,
           router_w, router_b, moe_w1, moe_w3, moe_w2):
    D = x_prompt.shape[-1]
    page = cache_fox_k.shape[1]
    past = page_table.shape[1] * page
    streams = (_Stream(x_prompt.shape[0], x_prompt.shape[1], 0),
               _Stream(x_sample.shape[0], x_sample.shape[1], past))
    conds = (c_prompt, c_sample)
    xs = [x_prompt.reshape(-1, D), x_sample.reshape(-1, D)]
    outs = [dict(), dict()]

    w_in = {'fox': fox_w_in, 'moba': moba_w_in, 'nsa': nsa_w_in, 'dsa': dsa_w_in}
    w_out = {'fox': fox_w_o, 'moba': moba_w_o, 'nsa': nsa_w_o, 'dsa': dsa_w_o}
    w_in = {n: _pad_cols(w).astype(BF16) for n, w in w_in.items()}
    w_out = {n: w.astype(BF16) for n, w in w_out.items()}
    moe_w1b, moe_w3b, moe_w2b = moe_w1.astype(BF16), moe_w3.astype(BF16), moe_w2.astype(BF16)

    def keys_of(st, new, cache):
        B, T = new.shape[:2]
        if st.past == 0:
            return new.reshape(B, T, -1), T, False
        return gather_rows_t(cache, page_table, new), st.past + T, True

    def attend(st, q, K, V, L, major, **kw):
        Lk = K.shape[2] if major else K.shape[1]
        return flash_attention(q, K, V, tq=st.tq, kt=_kv_tile(Lk), qpos_base=st.past, kv_len=L,
                               k_major=major, v_major=major, **kw)

    def channels_major(a, major):
        return a if major else a.transpose(0, 2, 1)

    def fox(st, y, o):
        B, T = st.B, st.T
        q, k, v, fl = _split(y, [N_HEADS * HEAD_DIM, KV_DIM, KV_DIM, N_HEADS])
        q = _head_norm(q.reshape(B, T, N_HEADS, HEAD_DIM), fox_q_norm)
        k = _head_norm(k.reshape(B, T, N_KV_HEADS, HEAD_DIM), fox_k_norm)
        v = v.reshape(B, T, N_KV_HEADS, HEAD_DIM)
        logf = jax.nn.log_sigmoid(fl.reshape(B, T, N_HEADS) + fox_b_f)
        o.update(fox_k=k, fox_v=v, fox_logf=logf)
        K, L, major = keys_of(st, k, cache_fox_k)
        V, _, _ = keys_of(st, v, cache_fox_v)
        LF, _, lf_major = keys_of(st, logf, cache_fox_logf)
        LF = channels_major(LF, lf_major)
        F = cumsum_lanes(LF.reshape(B * N_HEADS, -1)).reshape(LF.shape) * LOG2E
        fq = _rows_gt(lax.slice_in_dim(F, st.past, st.past + T, axis=2).transpose(0, 2, 1), st.tq)
        return [attend(st, _qrows(q, st.tq), K, V, L, major, mode='fox', fq=fq, fkT=F)], None

    def moba(st, y, o):
        B, T = st.B, st.T
        q, k, v = _split(y, [N_HEADS * HEAD_DIM, KV_DIM, KV_DIM])
        q = _rope(_head_norm(q.reshape(B, T, N_HEADS, HEAD_DIM), moba_q_norm), st.pos)
        k = _rope(_head_norm(k.reshape(B, T, N_KV_HEADS, HEAD_DIM), moba_k_norm), st.pos)
        v = v.reshape(B, T, N_KV_HEADS, HEAD_DIM)
        o.update(moba_k=k, moba_v=v)
        K, L, major = keys_of(st, k, cache_moba_k)
        V, _, _ = keys_of(st, v, cache_moba_v)
        Kt = channels_major(K, major)
        NBP = _round_up(Kt.shape[2] // MOBA_BLOCK, LANES)
        kbar = block_means_t(Kt, MOBA_BLOCK, NBP)
        kbt = jnp.tile(kbar.reshape(B, N_KV_HEADS, 1, HEAD_DIM, NBP), (1, 1, GROUP, 1, 1))
        kbt = kbt.reshape(B, N_KV_HEADS, KV_DIM, NBP)
        sel = moba_select(q.reshape(B, T, -1), kbt, tq=st.tq, qpos_base=st.past)
        return [attend(st, _qrows(q, st.tq), K, V, L, major, mode='blocks', sel=sel, blk=MOBA_BLOCK)], None

    def compress(st, raw_new, cache, pe, w1, w2):
        B, T = st.B, st.T
        if st.past == 0:
            lo, hi = compress_lohi(raw_new.reshape(B * T, KV_DIM), pe, w1)
            nblk = T // CMP_STRIDE
            lo, hi = lo.reshape(B, nblk, KV_DIM), hi.reshape(B, nblk, KV_DIM)
            L = T
        else:
            P = cache.shape[0]
            lo_p, hi_p = compress_lohi(pool_t(cache), pe, w1, pool_major=True)
            tail = jnp.pad(raw_new.reshape(B, T, KV_DIM), ((0, 0), (0, page - T), (0, 0)))
            lo_t, hi_t = compress_lohi(tail.reshape(B * page, KV_DIM), pe, w1)
            bpp = page // CMP_STRIDE
            lo = gather_pages(lo_p.reshape(P, bpp, KV_DIM), page_table, lo_t.reshape(B, bpp, KV_DIM))
            hi = gather_pages(hi_p.reshape(P, bpp, KV_DIM), page_table, hi_t.reshape(B, bpp, KV_DIM))
            L = st.past + T
        nc = _round_up(L, SLC_BLOCK) // CMP_STRIDE - 1
        NCP = _round_up(nc, LANES)
        lo = _pad_len(lo, NCP + 1)
        hi = _pad_len(hi, NCP + 1)
        out = compress_out(lo[:, :NCP].reshape(B * NCP, KV_DIM), hi[:, 1:NCP + 1].reshape(B * NCP, KV_DIM), w2)
        return out.reshape(B, NCP, KV_DIM), nc

    def nsa(st, y, o, win_k, win_v):
        B, T = st.B, st.T
        q, kc, vc, ks, vs, kw, vw, gl = _split(y, [N_HEADS * HEAD_DIM] + [KV_DIM] * 6 + [3 * N_HEADS])
        heads = lambda a: a.reshape(B, T, N_KV_HEADS, HEAD_DIM)
        q = _rope(_head_norm(q.reshape(B, T, N_HEADS, HEAD_DIM), nsa_q_norm), st.pos)
        ks = _rope(_head_norm(heads(ks), nsa_ks_norm), st.pos)
        kw = _rope(_head_norm(heads(kw), nsa_kw_norm), st.pos)
        kc, vc, vs, vw = heads(kc), heads(vc), heads(vs), heads(vw)
        gates = jax.nn.sigmoid(gl).reshape(B, T, N_HEADS, 3)
        o.update(nsa_cmp_k=kc, nsa_cmp_v=vc, nsa_slc_k=ks, nsa_slc_v=vs)
        qp = _qrows(q, st.tq)
        kcc, nc = compress(st, kc, cache_nsa_cmp_k, nsa_pe_k, nsa_w1_k, nsa_w2_k)
        vcc, _ = compress(st, vc, cache_nsa_cmp_v, nsa_pe_v, nsa_w1_v, nsa_w2_v)
        NCP = kcc.shape[1]
        cend = jnp.arange(NCP) * CMP_STRIDE + CMP_LEN - 1
        kcc = _rope(_head_norm(kcc.reshape(B, NCP, N_KV_HEADS, HEAD_DIM), nsa_kc_norm), cend)
        KS, L, major = keys_of(st, ks, cache_nsa_slc_k)
        VS, _, _ = keys_of(st, vs, cache_nsa_slc_v)
        nbs = _round_up(L, SLC_BLOCK) // SLC_BLOCK
        NBP = _round_up(nbs, LANES)
        ci = np.arange(NCP)[:, None]
        sj = np.arange(NBP)[None, :]
        cover = ((ci * CMP_STRIDE < (sj + 1) * SLC_BLOCK) & (ci * CMP_STRIDE + CMP_LEN > sj * SLC_BLOCK)
                 & (ci < nc) & (sj < nbs))
        oc, sel = nsa_compressed(qp, kcc.reshape(B, NCP, KV_DIM).astype(BF16), vcc.astype(BF16),
                                 jnp.asarray(cover, BF16), tq=st.tq, qpos_base=st.past, nc=nc)
        osl = attend(st, qp, KS, VS, L, major, mode='blocks', sel=sel, blk=SLC_BLOCK)
        if st.past == 0:
            KW, VW, kbase = kw.reshape(B, T, KV_DIM), vw.reshape(B, T, KV_DIM), 0
            wb = min(WINDOW, T)
            o.update(nsa_win_k=kw[:, T - wb:], nsa_win_v=vw[:, T - wb:])
        else:
            KWf = jnp.concatenate([win_k, kw], axis=1)
            VWf = jnp.concatenate([win_v, vw], axis=1)
            lw = KWf.shape[1]
            wb = min(WINDOW, lw)
            o.update(nsa_win_k=KWf[:, lw - wb:], nsa_win_v=VWf[:, lw - wb:])
            kbase = st.past - win_k.shape[1]
            Lp = _round_up(lw, LANES)
            KW = _pad_len(KWf.reshape(B, lw, KV_DIM), Lp)
            VW = _pad_len(VWf.reshape(B, lw, KV_DIM), Lp)
        ow = flash_attention(qp, KW, VW, tq=st.tq, kt=_kv_tile(KW.shape[1]), qpos_base=st.past,
                             kpos_base=kbase, kv_len=KW.shape[1], mode='plain', window=WINDOW)
        return [oc, osl, ow], gates

    def dsa(st, y, o):
        B, T, tq = st.B, st.T, st.tq
        q, k, v, qi, ki, wi = _split(y, [N_HEADS * HEAD_DIM, KV_DIM, KV_DIM, IDX_HEADS * IDX_DIM, IDX_DIM, IDX_HEADS])
        q = _rope(_head_norm(q.reshape(B, T, N_HEADS, HEAD_DIM), dsa_q_norm), st.pos)
        k = _rope(_head_norm(k.reshape(B, T, N_KV_HEADS, HEAD_DIM), dsa_k_norm), st.pos)
        v = v.reshape(B, T, N_KV_HEADS, HEAD_DIM)
        qi = _rope(qi.reshape(B, T, IDX_HEADS, IDX_DIM), st.pos)
        ki = _rope(ki.reshape(B, T, 1, IDX_DIM), st.pos)[:, :, 0]
        wi = wi * IDX_HEADS ** -0.5
        o.update(dsa_k=k, dsa_v=v, dsa_idx=ki)
        K, L, major = keys_of(st, k, cache_dsa_k)
        V, _, _ = keys_of(st, v, cache_dsa_v)
        KI, _, ki_major = keys_of(st, ki, cache_dsa_idx_k)
        KI = channels_major(KI, ki_major)
        kt = _kv_tile(KI.shape[2])
        nQ = T // tq
        qi_r = qi.astype(BF16).reshape(B, nQ, tq, IDX_HEADS, IDX_DIM).transpose(0, 1, 3, 2, 4)
        qi_r = qi_r.reshape(B, nQ, IDX_HEADS * tq, IDX_DIM)
        wi_r = wi.reshape(B, nQ, tq, IDX_HEADS)
        n_sel = min(DSA_TOPK, L // 4)
        picked = dsa_select(qi_r, wi_r, KI, tq=tq, kt=kt, qpos_base=st.past, n_sel=n_sel)
        return [attend(st, _qrows(q, tq), K, V, L, major, mode='mask', mask=picked)], None

    for layer in range(ada_w.shape[0]):
        kind = ('fox', 'moba', 'nsa', 'dsa')[layer % 4]
        for s, st in enumerate(streams):
            x = xs[s]
            sh1, sc1, g1, sh2, sc2, g2 = _modulation(conds[s], ada_w[layer], ada_b[layer])
            y = fused_matmul(x, w_in[kind], seq_len=st.T, tm=st.tm,
                             norm=(norm_mix[layer][None, :], sc1, sh1)).reshape(st.B, st.T, -1)
            if kind == 'fox':
                branches, gates = fox(st, y, outs[s])
            elif kind == 'moba':
                branches, gates = moba(st, y, outs[s])
            elif kind == 'nsa':
                branches, gates = nsa(st, y, outs[s], state_nsa_win_k, state_nsa_win_v)
            else:
                branches, gates = dsa(st, y, outs[s])
            if st.tq % 16 == 0:
                gts = None if gates is None else _gate_rows(gates, st.tq)
                x = proj_heads(branches, w_out[kind], x, g1, tq=st.tq, gts=gts)
            else:
                att = [_unpad_out(a, st.tq) for a in branches]
                if gates is not None:
                    att = [sum(gates[..., c:c + 1] * a for c, a in enumerate(att))]
                x = fused_matmul(att[0].reshape(-1, N_HEADS * HEAD_DIM), w_out[kind], seq_len=st.T, tm=st.tm,
                                 epi=(x, g1))
            ffn_norm = (norm_ffn[layer][None, :], sc2, sh2)
            coef = route(x, router_w, router_b, ffn_norm, seq_len=st.T, tm=st.tm)
            xs[s] = moe_dense(x, coef, *ffn_norm, g2, moe_w1b[layer], moe_w3b[layer], moe_w2b[layer],
                              seq_len=st.T, tm=2 * st.tm)

    names = ('fox_k', 'fox_v', 'fox_logf', 'moba_k', 'moba_v', 'nsa_cmp_k', 'nsa_cmp_v', 'nsa_slc_k', 'nsa_slc_v',
             'nsa_win_k', 'nsa_win_v', 'dsa_k', 'dsa_v', 'dsa_idx')
    res = [xs[0].reshape(x_prompt.shape), xs[1].reshape(x_sample.shape)]
    for n in names:
        res += [outs[0][n], outs[1][n]]
    return tuple(res)
```

```python
import functools

import numpy as np
import jax
import jax.numpy as jnp
from jax import lax
from jax.experimental import pallas as pl
from jax.experimental.pallas import tpu as pltpu

HEAD_DIM = 64
N_KV_HEADS = 4
GROUP = 4
N_HEADS = N_KV_HEADS * GROUP
KV_DIM = N_KV_HEADS * HEAD_DIM
ROT_DIM = HEAD_DIM // 4
ROPE_THETA = 500000.0
ATTN_SCALE = HEAD_DIM ** -0.5
LOG2E = 1.4426950408889634
MOBA_BLOCK = 256
MOBA_TOPK = 3
CMP_LEN = 32
CMP_STRIDE = 16
CMP_HIDDEN = 64
SLC_BLOCK = 64
SLC_TOPK = 16
WINDOW = 512
IDX_HEADS = 8
IDX_DIM = 64
DSA_TOPK = 256
N_EXPERTS = 16
N_GROUPS = 4
EXPERTS_PER_GROUP = N_EXPERTS // N_GROUPS
TOP_K = 2
EPS = 1e-6
NEG_INF = -1e30

LANES = 128
VMEM_LIMIT = 56 * 1024 * 1024
BF16 = jnp.bfloat16
F32 = jnp.float32
INT_MIN = -2 ** 31


def _round_up(a, m):
    return (a + m - 1) // m * m


def _params(sem):
    return pltpu.CompilerParams(dimension_semantics=sem, vmem_limit_bytes=VMEM_LIMIT)


def _gather_kernel(pt_ref, *refs, pps, n_main):
    page_refs, tail_ref, o_ref = refs[:pps], refs[pps], refs[pps + 1]
    j = pl.program_id(1)

    @pl.when(j < n_main)
    def _():
        for p in range(pps):
            o_ref[p] = page_refs[p][...]

    @pl.when(j == n_main)
    def _():
        o_ref[0] = tail_ref[...]
        for p in range(1, pps):
            o_ref[p] = jnp.zeros(o_ref.shape[1:], o_ref.dtype)


def gather_pages(pool, page_table, tail):
    _, R, C = pool.shape
    B, n_pages = page_table.shape
    pps = max(d for d in (8, 4, 2, 1) if n_pages % d == 0)
    n_main = n_pages // pps

    def page_map(b, j, pt, p):
        return (pt[b, jnp.minimum(j * pps + p, n_pages - 1)], 0, 0)

    in_specs = [pl.BlockSpec((None, R, C), functools.partial(page_map, p=p)) for p in range(pps)]
    in_specs.append(pl.BlockSpec((None, R, C), lambda b, j, pt: (b, 0, 0)))
    out = pl.pallas_call(
        functools.partial(_gather_kernel, pps=pps, n_main=n_main), name='gather_pages',
        grid_spec=pltpu.PrefetchScalarGridSpec(
            num_scalar_prefetch=1, grid=(B, n_main + 1), in_specs=in_specs,
            out_specs=pl.BlockSpec((None, pps, R, C), lambda b, j, pt: (b, j, 0, 0))),
        out_shape=jax.ShapeDtypeStruct((B, (n_main + 1) * pps, R, C), pool.dtype),
        compiler_params=_params(("arbitrary", "arbitrary")),
    )(page_table, *([pool] * pps), tail)
    return out.reshape(B, (n_main + 1) * pps * R, C)


def _gather_t_kernel(pt_ref, *refs, pps, n_main, page):
    page_refs, tail_ref, o_ref = refs[:pps], refs[pps], refs[pps + 1]
    j = pl.program_id(1)

    @pl.when(j < n_main)
    def _():
        for p in range(pps):
            o_ref[:, p * page:(p + 1) * page] = page_refs[p][...]

    @pl.when(j == n_main)
    def _():
        o_ref[:, :page] = tail_ref[...]
        o_ref[:, page:] = jnp.zeros((o_ref.shape[0], (pps - 1) * page), o_ref.dtype)


def pool_t(pool):
    P, page = pool.shape[:2]
    return jnp.moveaxis(pool.reshape(P, page, -1), 1, 2)


def gather_rows_t(pool, page_table, new_rows):
    P, page = pool.shape[:2]
    B, n_pages = page_table.shape
    T = new_rows.shape[1]
    pt = pool_t(pool)
    C = pt.shape[1]
    tail = jnp.pad(new_rows.reshape(B, T, C), ((0, 0), (0, page - T), (0, 0))).transpose(0, 2, 1)
    pps = max(d for d in (8, 4, 2, 1) if n_pages % d == 0)
    n_main = n_pages // pps

    def page_map(b, j, tab, p):
        return (tab[b, jnp.minimum(j * pps + p, n_pages - 1)], 0, 0)

    in_specs = [pl.BlockSpec((None, C, page), functools.partial(page_map, p=p)) for p in range(pps)]
    in_specs.append(pl.BlockSpec((None, C, page), lambda b, j, tab: (b, 0, 0)))
    return pl.pallas_call(
        functools.partial(_gather_t_kernel, pps=pps, n_main=n_main, page=page), name='gather_pages_t',
        grid_spec=pltpu.PrefetchScalarGridSpec(
            num_scalar_prefetch=1, grid=(B, n_main + 1), in_specs=in_specs,
            out_specs=pl.BlockSpec((None, C, pps * page), lambda b, j, tab: (b, 0, j))),
        out_shape=jax.ShapeDtypeStruct((B, C, (n_main + 1) * pps * page), pool.dtype),
        compiler_params=_params(("arbitrary", "arbitrary")),
    )(page_table, *([pt] * pps), tail)


def _adanorm(x, g, sc, sh):
    y = x * lax.rsqrt(jnp.mean(x * x, axis=-1, keepdims=True) + EPS)
    return (y * g) * (1.0 + sc) + sh


def _matmul_kernel(*refs, has_norm, has_epi):
    it = iter(refs)
    x_ref, w_ref = next(it), next(it)
    x = x_ref[...]
    if has_norm:
        g_ref, sc_ref, sh_ref = next(it), next(it), next(it)
        x = _adanorm(x, g_ref[...], sc_ref[...], sh_ref[...])
    acc = jnp.dot(x.astype(BF16), w_ref[...], preferred_element_type=F32)
    if has_epi:
        res_ref, gate_ref = next(it), next(it)
        acc = res_ref[...] + gate_ref[...] * acc
    o_ref = next(it)
    o_ref[...] = acc


def _mod_spec(mod, tm, seq_len):
    D = mod.shape[-1]
    if seq_len % tm == 0:
        tps = seq_len // tm
        return mod, pl.BlockSpec((None, 1, D), lambda i: (i // tps, 0, 0))
    rows = jnp.broadcast_to(mod, (mod.shape[0], seq_len, D)).reshape(-1, D)
    return rows, pl.BlockSpec((tm, D), lambda i: (i, 0))


def fused_matmul(x, w, *, seq_len, tm, norm=None, epi=None):
    N, K = x.shape
    M = w.shape[1]
    tm = min(tm, N)
    args = [x, w]
    specs = [pl.BlockSpec((tm, K), lambda i: (i, 0)), pl.BlockSpec((K, M), lambda i: (0, 0))]
    if norm is not None:
        gain, sc, sh = norm
        args.append(gain)
        specs.append(pl.BlockSpec((1, K), lambda i: (0, 0)))
        for m in (sc, sh):
            a, s = _mod_spec(m, tm, seq_len)
            args.append(a)
            specs.append(s)
    if epi is not None:
        res, gate = epi
        args.append(res)
        specs.append(pl.BlockSpec((tm, M), lambda i: (i, 0)))
        a, s = _mod_spec(gate, tm, seq_len)
        args.append(a)
        specs.append(s)
    return pl.pallas_call(
        functools.partial(_matmul_kernel, has_norm=norm is not None, has_epi=epi is not None),
        name='proj_in' if norm is not None else 'proj_out',
        grid=(N // tm,), in_specs=specs,
        out_specs=pl.BlockSpec((tm, M), lambda i: (i, 0)),
        out_shape=jax.ShapeDtypeStruct((N, M), F32),
        compiler_params=_params(("arbitrary",)),
    )(*args)


def _proj_heads_kernel(*refs, n_branch, tq):
    it = iter(refs)
    o_refs = [next(it) for _ in range(n_branch)]
    gts_ref = next(it) if n_branch > 1 else None
    w_ref, res_ref, gate_ref, out_ref = next(it), next(it), next(it), next(it)
    R = GROUP * tq
    if n_branch == 1:
        o = o_refs[0][...]
    else:
        gts = gts_ref[...]
        o = None
        for c in range(n_branch):
            gfull = jnp.zeros((R, KV_DIM), F32)
            for h in range(N_KV_HEADS):
                col = h * n_branch + c
                gfull = jnp.where(_lane_head_mask(R, h), gts[:, col:col + 1], gfull)
            o = gfull * o_refs[c][...] if o is None else o + gfull * o_refs[c][...]
    ob = o.astype(BF16)
    acc = jnp.zeros(out_ref.shape, F32)
    for g in range(GROUP):
        acc += jnp.dot(ob[g * tq:(g + 1) * tq], w_ref[g], preferred_element_type=F32)
    out_ref[...] = res_ref[...] + gate_ref[...] * acc


def proj_heads(branches, w_o, res, gate, *, tq, gts=None):
    B, nQ, R, _ = branches[0].shape
    D = w_o.shape[1]
    nb = len(branches)
    w = w_o.reshape(N_KV_HEADS, GROUP, HEAD_DIM, D).transpose(1, 0, 2, 3).reshape(GROUP, KV_DIM, D)
    tile = lambda b, i: (b, i, 0, 0)
    args = list(branches)
    specs = [pl.BlockSpec((None, None, R, KV_DIM), tile)] * nb
    if nb > 1:
        args.append(gts)
        specs.append(pl.BlockSpec((None, None, R, gts.shape[-1]), tile))
    args += [w, res.reshape(B, nQ, tq, D), gate]
    specs += [pl.BlockSpec((GROUP, KV_DIM, D), lambda b, i: (0, 0, 0)),
              pl.BlockSpec((None, None, tq, D), tile), pl.BlockSpec((None, 1, D), lambda b, i: (b, 0, 0))]
    out = pl.pallas_call(
        functools.partial(_proj_heads_kernel, n_branch=nb, tq=tq), name='proj_heads',
        grid=(B, nQ), in_specs=specs,
        out_specs=pl.BlockSpec((None, None, tq, D), tile),
        out_shape=jax.ShapeDtypeStruct((B, nQ, tq, D), F32),
        compiler_params=_params(("arbitrary", "arbitrary")),
    )(*args)
    return out.reshape(B * nQ * tq, D)


ROUTE_ROWS = 8


def _first_match(vals, target):
    out, seen = [], None
    for v in vals:
        hit = v == target
        out.append(hit if seen is None else hit & jnp.logical_not(seen))
        seen = hit if seen is None else seen | hit
    return out


def _router_kernel(x_ref, g_ref, sc_ref, sh_ref, wt_ref, b_ref, o_ref):
    h = _adanorm(x_ref[...], g_ref[...], sc_ref[...], sh_ref[...]).astype(BF16)
    logit = lax.dot_general(wt_ref[...], h, (((1,), (1,)), ((), ())), preferred_element_type=F32)
    aff = jax.nn.sigmoid(logit)
    grp = aff + b_ref[...]
    n = EXPERTS_PER_GROUP
    a = [grp[k * ROUTE_ROWS:(k + 1) * ROUTE_ROWS] for k in range(n)]
    f = [aff[k * ROUTE_ROWS:(k + 1) * ROUTE_ROWS] for k in range(n)]
    top1 = functools.reduce(jnp.maximum, a)
    s1 = _first_match(a, top1)
    rest = [jnp.where(s, -jnp.inf, v) for s, v in zip(s1, a)]
    top2 = functools.reduce(jnp.maximum, rest)
    s2 = _first_match(rest, top2)
    row = lax.broadcasted_iota(jnp.int32, top1.shape, 0).astype(F32)
    score = jnp.where(row < N_GROUPS, top1 + top2, -jnp.inf)
    best = jnp.max(score, axis=0, keepdims=True)
    gsel = row == jnp.min(jnp.where(score == best, row, 1e9), axis=0, keepdims=True)
    w1 = sum(jnp.where(s, v, 0.0) for s, v in zip(s1, f))
    w2 = sum(jnp.where(s, v, 0.0) for s, v in zip(s2, f))
    den = w1 + w2
    for k in range(n):
        o_ref[k * ROUTE_ROWS:(k + 1) * ROUTE_ROWS, :] = jnp.where(
            gsel & s1[k], w1 / den, jnp.where(gsel & s2[k], w2 / den, 0.0))


def route(x, router_w, router_b, norm, *, seq_len, tm):
    assert TOP_K == 2
    N, D = x.shape
    tm = min(tm, N)
    wt = router_w.T.reshape(N_GROUPS, EXPERTS_PER_GROUP, D).transpose(1, 0, 2)
    wt = jnp.pad(wt, ((0, 0), (0, ROUTE_ROWS - N_GROUPS), (0, 0))).reshape(-1, D).astype(BF16)
    bt = router_b.reshape(N_GROUPS, EXPERTS_PER_GROUP).T
    bt = jnp.pad(bt, ((0, 0), (0, ROUTE_ROWS - N_GROUPS))).reshape(-1, 1)
    rows = wt.shape[0]
    gain, sc, sh = norm
    args, specs = [x, gain], [pl.BlockSpec((tm, D), lambda i: (i, 0)), pl.BlockSpec((1, D), lambda i: (0, 0))]
    for m in (sc, sh):
        a, s = _mod_spec(m, tm, seq_len)
        args.append(a)
        specs.append(s)
    args += [wt, bt]
    specs += [pl.BlockSpec((rows, D), lambda i: (0, 0)), pl.BlockSpec((rows, 1), lambda i: (0, 0))]
    coef_t = pl.pallas_call(
        _router_kernel, name='router', grid=(N // tm,), in_specs=specs,
        out_specs=pl.BlockSpec((rows, tm), lambda i: (0, i)),
        out_shape=jax.ShapeDtypeStruct((rows, N), F32),
        compiler_params=_params(("arbitrary",)),
    )(*args)
    coef_t = coef_t.reshape(EXPERTS_PER_GROUP, ROUTE_ROWS, N)[:, :N_GROUPS]
    return coef_t.transpose(2, 1, 0).reshape(N, N_EXPERTS)


def _cumsum_kernel(x_ref, o_ref, carry_ref, *, tc):
    @pl.when(pl.program_id(0) == 0)
    def _():
        carry_ref[...] = jnp.zeros_like(carry_ref)

    r = lax.broadcasted_iota(jnp.int32, (tc, tc), 0)
    c = lax.broadcasted_iota(jnp.int32, (tc, tc), 1)
    out = jnp.dot(x_ref[...], (r <= c).astype(F32), preferred_element_type=F32,
                  precision=lax.Precision.HIGHEST) + carry_ref[...]
    o_ref[...] = out
    carry_ref[...] = out[:, tc - 1:tc]


def cumsum_lanes(x, tc=256):
    rows, L = x.shape
    tc = min(tc, L)
    return pl.pallas_call(
        functools.partial(_cumsum_kernel, tc=tc), name='cumsum',
        grid=(L // tc,),
        in_specs=[pl.BlockSpec((rows, tc), lambda j: (0, j))],
        out_specs=pl.BlockSpec((rows, tc), lambda j: (0, j)),
        out_shape=jax.ShapeDtypeStruct((rows, L), F32),
        scratch_shapes=[pltpu.VMEM((rows, 1), F32)],
        compiler_params=_params(("arbitrary",)),
    )(x)


def _select_blocks(score, own, n_top):
    col = lax.broadcasted_iota(jnp.int32, score.shape, 1)
    colf = col.astype(F32)
    sc = jnp.where(col < own, score, -jnp.inf)
    sel = col == own
    for _ in range(n_top):
        m = jnp.max(sc, axis=-1, keepdims=True)
        idx = jnp.min(jnp.where(sc == m, colf, 1e9), axis=-1, keepdims=True)
        hit = colf == idx
        sel = sel | (hit & (m > -jnp.inf))
        sc = jnp.where(hit, -jnp.inf, sc)
    return sel


def _row_pos(tq, rows, base):
    assert tq & (tq - 1) == 0
    r = lax.broadcasted_iota(jnp.int32, (rows, 1), 0)
    return base + (r & (tq - 1))


def _moba_select_kernel(q_ref, kb_ref, o_ref, *, tq, qpos_base):
    qpos = _row_pos(tq, tq, qpos_base + pl.program_id(1) * tq)
    own = lax.shift_right_logical(qpos, int(np.log2(MOBA_BLOCK)))
    for h in range(N_KV_HEADS):
        gate = jnp.dot(q_ref[:, h * KV_DIM:(h + 1) * KV_DIM], kb_ref[h], preferred_element_type=F32)
        o_ref[h] = _select_blocks(gate, own, MOBA_TOPK).astype(F32)


def moba_select(q_nat, kbt, *, tq, qpos_base):
    B, T, D = q_nat.shape
    NBP = kbt.shape[-1]
    return pl.pallas_call(
        functools.partial(_moba_select_kernel, tq=tq, qpos_base=qpos_base), name='moba_select',
        grid=(B, T // tq),
        in_specs=[pl.BlockSpec((None, tq, D), lambda b, i: (b, i, 0)),
                  pl.BlockSpec((None, N_KV_HEADS, KV_DIM, NBP), lambda b, i: (b, 0, 0, 0))],
        out_specs=pl.BlockSpec((None, N_KV_HEADS, tq, NBP), lambda b, i: (b, 0, i, 0)),
        out_shape=jax.ShapeDtypeStruct((B, N_KV_HEADS, T, NBP), F32),
        compiler_params=_params(("arbitrary", "arbitrary")),
    )(q_nat, kbt)


def _block_mean_kernel(k_ref, o_ref, *, tl, blk):
    j = pl.program_id(1)

    @pl.when(j == 0)
    def _():
        o_ref[...] = jnp.zeros_like(o_ref)

    nbp = o_ref.shape[-1]
    key_blk = lax.shift_right_logical(j * tl + lax.broadcasted_iota(jnp.int32, (tl, nbp), 0), int(np.log2(blk)))
    avg = jnp.where(key_blk == lax.broadcasted_iota(jnp.int32, (tl, nbp), 1), 1.0 / blk, 0.0)
    o_ref[...] += jnp.dot(k_ref[...], avg, preferred_element_type=F32, precision=lax.Precision.HIGHEST)


def block_means_t(kt_, blk, nbp):
    B, C, L = kt_.shape
    tl = max(d for d in (1024, 512, 256, 128) if L % d == 0)
    return pl.pallas_call(
        functools.partial(_block_mean_kernel, tl=tl, blk=blk), name='block_means',
        grid=(B, L // tl),
        in_specs=[pl.BlockSpec((None, C, tl), lambda b, j: (b, 0, j))],
        out_specs=pl.BlockSpec((None, C, nbp), lambda b, j: (b, 0, 0)),
        out_shape=jax.ShapeDtypeStruct((B, C, nbp), F32),
        compiler_params=_params(("arbitrary", "arbitrary")),
    )(kt_)


def _compress_lohi_kernel(x_ref, pe_lo_ref, pe_hi_ref, wlo_ref, whi_ref, lo_ref, hi_ref, *scratch,
                          nblk, pool_major):
    if pool_major:
        xa_ref, xb_ref = scratch
        page = x_ref.shape[-1]
        for p in range(x_ref.shape[0]):
            xt = x_ref[p].T
            xa_ref[p * page:(p + 1) * page, :] = xt[:, :LANES]
            xb_ref[p * page:(p + 1) * page, :] = xt[:, LANES:]
    lo = jnp.zeros((nblk, KV_DIM), F32)
    hi = jnp.zeros((nblk, KV_DIM), F32)
    for t in range(CMP_STRIDE):
        if pool_major:
            x = jnp.concatenate([xa_ref[pl.ds(t, nblk, stride=CMP_STRIDE), :],
                                 xb_ref[pl.ds(t, nblk, stride=CMP_STRIDE), :]], axis=-1)
        else:
            x = jnp.concatenate([x_ref[pl.ds(2 * t, nblk, stride=2 * CMP_STRIDE), :],
                                 x_ref[pl.ds(2 * t + 1, nblk, stride=2 * CMP_STRIDE), :]], axis=-1)
        lo += jnp.dot((x + pe_lo_ref[t:t + 1, :]).astype(BF16), wlo_ref[t], preferred_element_type=F32)
        hi += jnp.dot((x + pe_hi_ref[t:t + 1, :]).astype(BF16), whi_ref[t], preferred_element_type=F32)
    lo_ref[...] = lo
    hi_ref[...] = hi


def compress_lohi(raw, pe, w1, tr=2048, pool_major=False):
    if pool_major:
        P, _, page = raw.shape
        N = P * page
    else:
        N = raw.shape[0]
    tr = max(d for d in (tr, tr // 2, tr // 4, tr // 8, LANES) if N % d == 0)
    nblk = tr // CMP_STRIDE
    if pool_major:
        x_arg = raw
        x_spec = pl.BlockSpec((tr // page, KV_DIM, page), lambda i: (i, 0, 0))
        scratch = [pltpu.VMEM((tr, LANES), F32), pltpu.VMEM((tr, LANES), F32)]
    else:
        x_arg = raw.reshape(2 * N, LANES)
        x_spec = pl.BlockSpec((2 * tr, LANES), lambda i: (i, 0))
        scratch = []
    eye = jnp.eye(N_KV_HEADS, dtype=F32)

    def blockdiag(w):
        w = w.reshape(CMP_STRIDE, HEAD_DIM, CMP_HIDDEN)
        return jnp.einsum('tdc,hk->thdkc', w, eye).reshape(CMP_STRIDE, KV_DIM, KV_DIM).astype(BF16)

    half = CMP_STRIDE * HEAD_DIM
    pe_lo = jnp.tile(pe[:CMP_STRIDE], (1, N_KV_HEADS))
    pe_hi = jnp.tile(pe[CMP_STRIDE:], (1, N_KV_HEADS))
    const = lambda i: (0, 0)
    const3 = lambda i: (0, 0, 0)
    return pl.pallas_call(
        functools.partial(_compress_lohi_kernel, nblk=nblk, pool_major=pool_major), name='compress_lohi',
        grid=(N // tr,),
        in_specs=[x_spec,
                  pl.BlockSpec((CMP_STRIDE, KV_DIM), const), pl.BlockSpec((CMP_STRIDE, KV_DIM), const),
                  pl.BlockSpec((CMP_STRIDE, KV_DIM, KV_DIM), const3),
                  pl.BlockSpec((CMP_STRIDE, KV_DIM, KV_DIM), const3)],
        out_specs=[pl.BlockSpec((nblk, KV_DIM), lambda i: (i, 0))] * 2,
        out_shape=[jax.ShapeDtypeStruct((N // CMP_STRIDE, KV_DIM), F32)] * 2,
        scratch_shapes=scratch,
        compiler_params=_params(("arbitrary",)),
    )(x_arg, pe_lo, pe_hi, blockdiag(w1[:half]), blockdiag(w1[half:]))


def _compress_out_kernel(lo_ref, hi_ref, w_ref, o_ref):
    z = lo_ref[...] + hi_ref[...]
    z = z * jax.nn.sigmoid(z)
    o_ref[...] = jnp.dot(z.astype(BF16), w_ref[...], preferred_element_type=F32)


def compress_out(lo, hi_next, w2):
    N = lo.shape[0]
    tm = max(d for d in (512, 256, 128, 64, 32, 16, 8) if N % d == 0)
    w = jnp.einsum('cd,hk->hckd', w2, jnp.eye(N_KV_HEADS, dtype=F32)).reshape(KV_DIM, KV_DIM).astype(BF16)
    return pl.pallas_call(
        _compress_out_kernel, name='compress_out', grid=(N // tm,),
        in_specs=[pl.BlockSpec((tm, KV_DIM), lambda i: (i, 0)), pl.BlockSpec((tm, KV_DIM), lambda i: (i, 0)),
                  pl.BlockSpec((KV_DIM, KV_DIM), lambda i: (0, 0))],
        out_specs=pl.BlockSpec((tm, KV_DIM), lambda i: (i, 0)),
        out_shape=jax.ShapeDtypeStruct((N, KV_DIM), F32),
        compiler_params=_params(("arbitrary",)),
    )(lo, hi_next, w)


def _lane_head_mask(rows, h):
    lane = lax.broadcasted_iota(jnp.int32, (rows, KV_DIM), 1)
    return (lane >= h * HEAD_DIM) & (lane < (h + 1) * HEAD_DIM)


def _spread_head(qh, h):
    d = lax.broadcasted_iota(jnp.int32, (HEAD_DIM, KV_DIM), 0)
    lane = lax.broadcasted_iota(jnp.int32, (HEAD_DIM, KV_DIM), 1)
    place = (lane == d + h * HEAD_DIM).astype(BF16)
    return jnp.dot(qh, place, preferred_element_type=F32).astype(BF16)


def _nsa_cmp_kernel(q_ref, kc_ref, vc_ref, cover_ref, o_ref, sel_ref, *, tq, qpos_base, nc):
    R = GROUP * tq
    NCP = kc_ref.shape[0]
    q0 = qpos_base + pl.program_id(1) * tq
    qpos = _row_pos(tq, R, q0)
    c = lax.broadcasted_iota(jnp.int32, (1, NCP), 1)
    mask = (c * CMP_STRIDE + (CMP_LEN - 1) <= qpos) & (c < nc)
    kc = kc_ref[...]
    vc = vc_ref[...]
    own = lax.shift_right_logical(_row_pos(tq, tq, q0), int(np.log2(SLC_BLOCK)))
    out = jnp.zeros((R, KV_DIM), F32)
    imps = []
    for h in range(N_KV_HEADS):
        s = lax.dot_general(_spread_head(q_ref[h], h), kc, (((1,), (1,)), ((), ())),
                            preferred_element_type=F32)
        s = jnp.where(mask, s, NEG_INF)
        m = jnp.max(s, axis=-1, keepdims=True)
        e = jnp.where(mask, jnp.exp2(s - m), 0.0)
        l = jnp.sum(e, axis=-1, keepdims=True)
        p = (e / jnp.where(l > 0.0, l, 1.0)).astype(BF16)
        out = jnp.where(_lane_head_mask(R, h), jnp.dot(p, vc, preferred_element_type=F32), out)
        imp = jnp.dot(p, cover_ref[...], preferred_element_type=F32)
        imps.append(jnp.sum(imp.reshape(GROUP, tq, imp.shape[-1]), axis=0))
    o_ref[...] = out
    sel = _select_blocks(jnp.concatenate(imps, axis=0), jnp.concatenate([own] * N_KV_HEADS, axis=0),
                         SLC_TOPK - 1).astype(F32)
    for h in range(N_KV_HEADS):
        sel_ref[h] = sel[h * tq:(h + 1) * tq]


def nsa_compressed(qpad, kcc, vcc, cover, *, tq, qpos_base, nc):
    B, nQ, _, R, _ = qpad.shape
    NCP, NBP = cover.shape
    return pl.pallas_call(
        functools.partial(_nsa_cmp_kernel, tq=tq, qpos_base=qpos_base, nc=nc), name='nsa_compressed',
        grid=(B, nQ),
        in_specs=[pl.BlockSpec((None, None, N_KV_HEADS, R, HEAD_DIM), lambda b, i: (b, i, 0, 0, 0)),
                  pl.BlockSpec((None, NCP, KV_DIM), lambda b, i: (b, 0, 0)),
                  pl.BlockSpec((None, NCP, KV_DIM), lambda b, i: (b, 0, 0)),
                  pl.BlockSpec((NCP, NBP), lambda b, i: (0, 0))],
        out_specs=[pl.BlockSpec((None, None, R, KV_DIM), lambda b, i: (b, i, 0, 0)),
                   pl.BlockSpec((None, N_KV_HEADS, tq, NBP), lambda b, i: (b, 0, i, 0))],
        out_shape=[jax.ShapeDtypeStruct((B, nQ, R, KV_DIM), F32),
                   jax.ShapeDtypeStruct((B, N_KV_HEADS, nQ * tq, NBP), F32)],
        compiler_params=_params(("arbitrary", "arbitrary")),
    )(qpad, kcc, vcc, cover)


def _index_keys(qi, wi, ki, kpos, qpos, tq):
    rel = jnp.maximum(jnp.dot(qi, ki, preferred_element_type=F32), 0.0)
    score = jnp.zeros((tq, rel.shape[1]), F32)
    for i in range(IDX_HEADS):
        score = score + wi[:, i:i + 1] * rel[i * tq:(i + 1) * tq]
    score = jnp.where(kpos <= qpos, score + 0.0, -jnp.inf)
    bits = lax.bitcast_convert_type(score, jnp.int32)
    return bits ^ (lax.shift_right_arithmetic(bits, 31) & 0x7FFFFFFF)


def _dsa_select_kernel(qi_ref, wi_ref, ki_ref, o_ref, key_ref, seen_ref, *, tq, kt, qpos_base, n_sel, nkt):
    q0 = qpos_base + pl.program_id(1) * tq
    qpos = _row_pos(tq, tq, q0)
    last = jnp.minimum((q0 + tq - 1) // kt, nkt - 1)
    lane = lax.broadcasted_iota(jnp.int32, (1, kt), 1)

    for jt in range(nkt):
        @pl.when(jt <= last)
        def _(jt=jt):
            ki = ki_ref[:, jt * kt:(jt + 1) * kt].astype(BF16)
            key_ref[jt] = _index_keys(qi_ref[...], wi_ref[...], ki, jt * kt + lane, qpos, tq)

    def count(pred):
        def body(jt, acc):
            hit = pred(key_ref[jt])
            for u in range(kt // LANES):
                acc = acc + jnp.where(hit[:, u * LANES:(u + 1) * LANES], 1.0, 0.0)
            return acc
        acc = lax.fori_loop(0, last + 1, body, jnp.zeros((tq, LANES), F32))
        return jnp.sum(acc, axis=-1, keepdims=True)

    def count_ge(t):
        return count(lambda key: key >= t)

    def bit_step(it, t_u):
        cand = t_u | lax.shift_left(jnp.int32(1), 31 - it)
        ok = count_ge(cand ^ INT_MIN) >= n_sel
        return jnp.where(ok, cand, t_u)

    t_u = lax.fori_loop(0, 32, bit_step, jnp.zeros((tq, 1), jnp.int32))
    thr = t_u ^ INT_MIN

    need = n_sel - count(lambda key: key > thr)

    r = lax.broadcasted_iota(jnp.int32, (kt, kt), 0)
    c = lax.broadcasted_iota(jnp.int32, (kt, kt), 1)
    tri = (r <= c).astype(F32)
    seen_ref[...] = jnp.zeros((tq, 1), F32)
    for jt in range(nkt):
        @pl.when(jt <= last)
        def _(jt=jt):
            key = key_ref[jt]
            eq = key == thr
            rank = seen_ref[...] + jnp.dot(eq.astype(F32), tri, preferred_element_type=F32)
            keep = ((key > thr) | (eq & (rank <= need))) & (jt * kt + lane <= qpos)
            o_ref[:, jt * kt:(jt + 1) * kt] = jnp.where(keep, 1.0, 0.0).astype(o_ref.dtype)
            seen_ref[...] = rank[:, kt - 1:kt]

        @pl.when(jt > last)
        def _(jt=jt):
            o_ref[:, jt * kt:(jt + 1) * kt] = jnp.zeros((tq, kt), o_ref.dtype)


def dsa_select(qi, wi, ki_t, *, tq, kt, qpos_base, n_sel):
    B, nQ = qi.shape[:2]
    Lk = ki_t.shape[2]
    nkt = Lk // kt
    return pl.pallas_call(
        functools.partial(_dsa_select_kernel, tq=tq, kt=kt, qpos_base=qpos_base, n_sel=n_sel, nkt=nkt),
        name='dsa_select',
        grid=(B, nQ),
        in_specs=[pl.BlockSpec((None, None, IDX_HEADS * tq, IDX_DIM), lambda b, i: (b, i, 0, 0)),
                  pl.BlockSpec((None, None, tq, IDX_HEADS), lambda b, i: (b, i, 0, 0)),
                  pl.BlockSpec((None, IDX_DIM, Lk), lambda b, i: (b, 0, 0))],
        out_specs=pl.BlockSpec((None, tq, Lk), lambda b, i: (b, i, 0)),
        out_shape=jax.ShapeDtypeStruct((B, nQ * tq, Lk), BF16 if tq % 16 == 0 else F32),
        scratch_shapes=[pltpu.VMEM((nkt, tq, kt), jnp.int32), pltpu.VMEM((tq, 1), F32)],
        compiler_params=_params(("arbitrary", "arbitrary")),
    )(qi, wi, ki_t)


class _PerHead:
    def __init__(self, refs):
        self.refs = refs

    def __getitem__(self, idx):
        idx = idx if isinstance(idx, tuple) else (idx,)
        return self.refs[idx[0]][idx[1:] if len(idx) > 1 else ...]

    def __setitem__(self, idx, value):
        idx = idx if isinstance(idx, tuple) else (idx,)
        self.refs[idx[0]][idx[1:] if len(idx) > 1 else ...] = value

    @property
    def dtype(self):
        return self.refs[0].dtype


def _flash_kernel(si_ref, st_ref, sf_ref, sl_ref, sp_ref, skc_ref, skn_ref, *refs, mode, tq, kt, rc, qpos_base,
                  kpos_base, window, blk, k_major, v_major):
    it = iter(refs)
    q_ref, kfirst_ref, k_ref, v_ref = next(it), next(it), next(it), next(it)
    if mode == 'fox':
        fq_ref, fk_ref = next(it), next(it)
    elif mode == 'blocks':
        sel_ref = next(it)
    elif mode == 'mask':
        mask_ref = next(it)
    o_ref = next(it)
    per_head = lambda: _PerHead([next(it) for _ in range(N_KV_HEADS)])
    m_ref, l_ref, acc_ref, s2_ref, p_ref, a_ref, bias_ref, qp_ref = (per_head() for _ in range(8))

    R = GROUP * tq
    step = pl.program_id(1)
    q0 = qpos_base + si_ref[step] * tq
    kpos0 = kpos_base + st_ref[step] * kt
    slot = sp_ref[step]
    s_ref = _PerHead([r.at[slot] for r in s2_ref.refs])
    s_next_ref = _PerHead([r.at[1 - slot] for r in s2_ref.refs])

    def qk(h, keys_ref):
        kb = keys_ref[...].astype(BF16)
        if k_major:
            return jnp.dot(qp_ref[h], kb, preferred_element_type=F32)
        return lax.dot_general(qp_ref[h], kb, (((1,), (1,)), ((), ())), preferred_element_type=F32)

    @pl.when(sf_ref[step] == 1)
    def _():
        for h in range(N_KV_HEADS):
            m_ref[h] = jnp.full((R, 1), NEG_INF, F32)
            l_ref[h] = jnp.zeros((R, LANES), F32)
            acc_ref[h] = jnp.zeros((R, KV_DIM), F32)
            qp_ref[h] = _spread_head(q_ref[h], h)
            s_ref[h] = qk(h, kfirst_ref)

    def tile_mask():
        kpos = kpos0 + lax.broadcasted_iota(jnp.int32, (1, kt), 1)
        qpos = _row_pos(tq, tq, q0)
        if mode == 'mask':
            return mask_ref[...].astype(F32) > 0.5
        mask = kpos <= qpos
        if window is not None:
            mask = mask & (qpos - kpos < window) & (kpos >= 0)
        return mask

    def softmax_head(h, use_bias):
        hb = h if mode == 'blocks' else 0
        for g in range(GROUP):
            if mode == 'fox':
                fk = jnp.broadcast_to(fk_ref[h * GROUP + g:h * GROUP + g + 1, :], (rc, kt))

            def logits(t0):
                u = s_ref[h, g * tq + t0:g * tq + t0 + rc, :]
                if mode == 'fox':
                    u = u - fk
                if use_bias:
                    u = u + bias_ref[hb, t0:t0 + rc, :]
                return u

            for t0 in range(0, tq, rc):
                rows = slice(g * tq + t0, g * tq + t0 + rc)
                row = jnp.max(logits(t0), axis=-1, keepdims=True)
                if mode == 'fox':
                    row = row + fq_ref[h, rows, :]
                m_prev = m_ref[h, rows, :]
                m_new = jnp.maximum(m_prev, row)
                m_ref[h, rows, :] = m_new
                a_ref[h, rows, :] = jnp.exp2(m_prev - m_new)
            for t0 in range(0, tq, rc):
                rows = slice(g * tq + t0, g * tq + t0 + rc)
                shift = -m_ref[h, rows, :]
                if mode == 'fox':
                    shift = shift + fq_ref[h, rows, :]
                p = jnp.exp2(logits(t0) + shift)
                part = p[:, 0:LANES]
                for u in range(1, kt // LANES):
                    part = part + p[:, u * LANES:(u + 1) * LANES]
                l_ref[h, rows, :] = a_ref[h, rows, :] * l_ref[h, rows, :] + part
                p_ref[h, rows, :] = p.astype(p_ref.dtype)

    def heads(use_bias):
        vb = v_ref[...].astype(BF16)
        for h in range(N_KV_HEADS):
            s_next_ref[h] = qk(h, k_ref)
        if use_bias:
            base = jnp.where(tile_mask(), 0.0, -jnp.inf)
            if mode != 'blocks':
                bias_ref[0] = base
        for h in range(N_KV_HEADS):
            if mode == 'blocks':
                nbp = sel_ref.shape[-1]
                kblk = lax.shift_right_logical(kpos0 + lax.broadcasted_iota(jnp.int32, (nbp, kt), 1),
                                               int(np.log2(blk)))
                expand = (kblk == lax.broadcasted_iota(jnp.int32, (nbp, kt), 0)).astype(F32)
                picked = jnp.dot(sel_ref[h], expand, preferred_element_type=F32) > 0.5
                bias_ref[h] = jnp.where(picked, base, -jnp.inf)
        for h in range(N_KV_HEADS):
            softmax_head(h, use_bias)
            pb = p_ref[h].astype(BF16)
            if v_major:
                pv = lax.dot_general(pb, vb, (((1,), (1,)), ((), ())), preferred_element_type=F32)
            else:
                pv = jnp.dot(pb, vb, preferred_element_type=F32)
            acc_ref[h] = a_ref[h] * acc_ref[h] + pv

    if mode in ('plain', 'fox') and window is None:
        crosses = kpos0 + kt - 1 > q0
        pl.when(crosses)(lambda: heads(True))
        pl.when(jnp.logical_not(crosses))(lambda: heads(False))
    else:
        heads(True)

    @pl.when(sl_ref[step] == 1)
    def _():
        out = jnp.zeros((R, KV_DIM), F32)
        for h in range(N_KV_HEADS):
            l = jnp.sum(l_ref[h], axis=-1, keepdims=True)
            out = jnp.where(_lane_head_mask(R, h), acc_ref[h] / jnp.where(l > 0.0, l, 1.0), out)
        o_ref[...] = out


def flash_attention(q, k, v, *, tq, kt, qpos_base, kpos_base=0, kv_len, mode='plain', window=None,
                    fq=None, fkT=None, sel=None, blk=None, mask=None, k_major=False, v_major=False):
    B, nQ, _, R, _ = q.shape
    last_tile = (kv_len - 1) // kt

    steps = []
    for i in range(nQ):
        lo = 0 if window is None else max(qpos_base + i * tq - (window - 1) - kpos_base, 0) // kt
        hi = min((qpos_base + (i + 1) * tq - 1 - kpos_base) // kt, last_tile)
        steps += [(i, t, int(t == lo), int(t == hi), (t - lo) % 2, lo, min(t + 1, hi)) for t in range(lo, hi + 1)]
    tables = [jnp.asarray(np.array(col, np.int32)) for col in zip(*steps)]
    SI, ST, SKC, SKN = 0, 1, 5, 6

    def q_map(b, s, *tab):
        return (b, tab[SI][s], 0, 0, 0)

    def kv_map(major, which):
        if major:
            return lambda b, s, *tab: (b, 0, tab[which][s])
        return lambda b, s, *tab: (b, tab[which][s], 0)

    def kv_spec(major, which):
        return pl.BlockSpec((None, KV_DIM, kt) if major else (None, kt, KV_DIM), kv_map(major, which))

    args = [q, k, k, v]
    specs = [pl.BlockSpec((None, None, N_KV_HEADS, R, HEAD_DIM), q_map), kv_spec(k_major, SKC),
             kv_spec(k_major, SKN), kv_spec(v_major, ST)]
    rc = min(tq, 16)
    scratch = []
    for shape, dtype in (((R, 1), F32), ((R, LANES), F32), ((R, KV_DIM), F32), ((2, R, kt), F32),
                         ((R, kt), BF16 if rc % 16 == 0 else F32), ((R, 1), F32), ((tq, kt), F32),
                         ((R, KV_DIM), BF16)):
        scratch += [pltpu.VMEM(shape, dtype)] * N_KV_HEADS
    if mode == 'fox':
        args += [fq, fkT]
        specs += [pl.BlockSpec((None, None, N_KV_HEADS, R, 1), q_map),
                  pl.BlockSpec((None, N_HEADS, kt), kv_map(True, ST))]
    elif mode == 'blocks':
        NBP = sel.shape[-1]
        args += [sel]
        specs += [pl.BlockSpec((None, N_KV_HEADS, tq, NBP), lambda b, s, *tab: (b, 0, tab[SI][s], 0))]
    elif mode == 'mask':
        args += [mask]
        specs += [pl.BlockSpec((None, tq, kt), lambda b, s, *tab: (b, tab[SI][s], tab[ST][s]))]
    return pl.pallas_call(
        functools.partial(_flash_kernel, mode=mode, tq=tq, kt=kt, rc=rc, qpos_base=qpos_base,
                          kpos_base=kpos_base, window=window, blk=blk, k_major=k_major, v_major=v_major),
        name='flash_' + mode + ('_win' if window else ''),
        grid_spec=pltpu.PrefetchScalarGridSpec(
            num_scalar_prefetch=len(tables), grid=(B, len(steps)), in_specs=specs,
            out_specs=pl.BlockSpec((None, None, R, KV_DIM), lambda b, s, *tab: (b, tab[SI][s], 0, 0)),
            scratch_shapes=scratch),
        out_shape=jax.ShapeDtypeStruct((B, nQ, R, KV_DIM), F32),
        compiler_params=_params(("arbitrary", "arbitrary")),
    )(*tables, *args)


def _moe_kernel(x_ref, g_ref, sc_ref, sh_ref, gate_ref, c_ref, w1_ref, w3_ref, w2_ref, o_ref, h_ref, acc_ref):
    e = pl.program_id(1)

    @pl.when(e == 0)
    def _():
        h_ref[...] = _adanorm(x_ref[...], g_ref[...], sc_ref[...], sh_ref[...]).astype(BF16)
        acc_ref[...] = jnp.zeros_like(acc_ref)

    h = h_ref[...]
    a = jnp.dot(h, w1_ref[...], preferred_element_type=F32)
    a = (a * jax.nn.sigmoid(a)) * jnp.dot(h, w3_ref[...], preferred_element_type=F32)
    y = jnp.dot(a.astype(BF16), w2_ref[...], preferred_element_type=F32)
    c = c_ref[...]
    lane = lax.broadcasted_iota(jnp.int32, c.shape, 1)
    ce = jnp.sum(jnp.where(lane == e, c, 0.0), axis=-1, keepdims=True)
    acc_ref[...] += jnp.where(ce != 0.0, ce * y, 0.0)

    @pl.when(e == pl.num_programs(1) - 1)
    def _():
        o_ref[...] = x_ref[...] + gate_ref[...] * acc_ref[...]


def moe_dense(x, coef, gain, sc, sh, gate, w1, w3, w2, *, seq_len, tm):
    N, D = x.shape
    E, _, DE = w1.shape
    tm = min(tm, N)
    row = lambda i, e: (i, 0)
    specs = [pl.BlockSpec((tm, D), row), pl.BlockSpec((1, D), lambda i, e: (0, 0))]
    args = [x, gain]
    for m in (sc, sh, gate):
        a, s = _mod_spec(m, tm, seq_len)
        args.append(a)
        imap = s.index_map
        specs.append(pl.BlockSpec(s.block_shape, lambda i, e, imap=imap: imap(i)))
    args += [coef, w1, w3, w2]
    specs += [pl.BlockSpec((tm, E), row),
              pl.BlockSpec((None, D, DE), lambda i, e: (e, 0, 0)),
              pl.BlockSpec((None, D, DE), lambda i, e: (e, 0, 0)),
              pl.BlockSpec((None, DE, D), lambda i, e: (e, 0, 0))]
    return pl.pallas_call(
        _moe_kernel, name='moe_dense', grid=(N // tm, E), in_specs=specs,
        out_specs=pl.BlockSpec((tm, D), row),
        out_shape=jax.ShapeDtypeStruct((N, D), F32),
        scratch_shapes=[pltpu.VMEM((tm, D), BF16), pltpu.VMEM((tm, D), F32)],
        compiler_params=_params(("arbitrary", "arbitrary")),
    )(*args)


def _head_norm(x, g):
    return x * lax.rsqrt(jnp.mean(x * x, axis=-1, keepdims=True) + EPS) * g


def _rope(x, pos):
    half = ROT_DIM // 2
    inv = ROPE_THETA ** (-jnp.arange(half, dtype=F32) / half)
    ang = pos.astype(F32)[:, None] * inv[None, :]
    cos, sin = jnp.cos(ang)[:, None, :], jnp.sin(ang)[:, None, :]
    x1, x2 = x[..., :half], x[..., half:ROT_DIM]
    return jnp.concatenate([x1 * cos - x2 * sin, x2 * cos + x1 * sin, x[..., ROT_DIM:]], axis=-1)


def _split(y, sizes):
    return jnp.split(y[..., :sum(sizes)], np.cumsum(sizes)[:-1].tolist(), axis=-1)


def _pad_cols(w, mult=256):
    return jnp.pad(w, ((0, 0), (0, _round_up(w.shape[1], mult) - w.shape[1])))


def _qrows(q, tq):
    B, T = q.shape[:2]
    nQ = T // tq
    q = (q * (ATTN_SCALE * LOG2E)).astype(BF16).reshape(B, nQ, tq, N_KV_HEADS, GROUP, HEAD_DIM)
    return q.transpose(0, 1, 3, 4, 2, 5).reshape(B, nQ, N_KV_HEADS, GROUP * tq, HEAD_DIM)


def _gate_rows(gates, tq):
    B, T, _, n = gates.shape
    g = gates.reshape(B, T // tq, tq, N_KV_HEADS, GROUP, n).transpose(0, 1, 4, 2, 3, 5)
    return g.reshape(B, T // tq, GROUP * tq, N_KV_HEADS * n)


def _unpad_out(o, tq):
    B, nQ = o.shape[:2]
    o = o.reshape(B, nQ, GROUP, tq, N_KV_HEADS, HEAD_DIM).transpose(0, 1, 3, 4, 2, 5)
    return o.reshape(B, nQ * tq, N_HEADS, HEAD_DIM)


def _rows_gt(a, tq):
    B, T, _ = a.shape
    a = a.reshape(B, T // tq, tq, N_KV_HEADS, GROUP).transpose(0, 1, 3, 4, 2)
    return a.reshape(B, T // tq, N_KV_HEADS, GROUP * tq, 1)


def _pad_len(a, L):
    return jnp.pad(a, ((0, 0), (0, L - a.shape[1])) + ((0, 0),) * (a.ndim - 2))


def _modulation(c, ada_w, ada_b):
    m = jax.nn.silu(c) @ ada_w + ada_b
    return [a[:, None, :] for a in jnp.split(m, 6, axis=-1)]


class _Stream:
    def __init__(self, B, T, past):
        self.B, self.T, self.past = B, T, past
        self.tq = min(T, 128)
        self.tm = min(B * T, 512)
        self.pos = past + jnp.arange(T)


def _kv_tile(L):
    return max(d for d in (512, 256, 128) if L % d == 0)


def kernel(x_prompt, x_sample, cache_fox_k, cache_fox_v, cache_fox_logf, cache_moba_k, cache_moba_v,
           cache_nsa_cmp_k, cache_nsa_cmp_v, cache_nsa_slc_k, cache_nsa_slc_v, state_nsa_win_k, state_nsa_win_v,
           cache_dsa_k, cache_dsa_v, cache_dsa_idx_k, page_table, c_prompt, c_sample,
           ada_w, ada_b, norm_mix, norm_ffn,
           fox_w_in, fox_b_f, fox_q_norm, fox_k_norm, fox_w_o,
           moba_w_in, moba_q_norm, moba_k_norm, moba_w_o,
           nsa_w_in, nsa_q_norm, nsa_kc_norm, nsa_ks_norm, nsa_kw_norm, nsa_pe_k, nsa_w1_k, nsa_w2_k,
           nsa_pe_v, nsa_w1_v, nsa_w2_v, nsa_w_o,
           dsa_w_in, dsa_q_norm, dsa_k_norm, dsa_w_o,
           router_w, router_b, moe_w1, moe_w3, moe_w2):
    D = x_prompt.shape[-1]
    page = cache_fox_k.shape[1]
    past = page_table.shape[1] * page
    streams = (_Stream(x_prompt.shape[0], x_prompt.shape[1], 0),
               _Stream(x_sample.shape[0], x_sample.shape[1], past))
    conds = (c_prompt, c_sample)
    xs = [x_prompt.reshape(-1, D), x_sample.reshape(-1, D)]
    outs = [dict(), dict()]

    w_in = {'fox': fox_w_in, 'moba': moba_w_in, 'nsa': nsa_w_in, 'dsa': dsa_w_in}
    w_out = {'fox': fox_w_o, 'moba': moba_w_o, 'nsa': nsa_w_o, 'dsa': dsa_w_o}
    w_in = {n: _pad_cols(w).astype(BF16) for n, w in w_in.items()}
    w_out = {n: w.astype(BF16) for n, w in w_out.items()}
    moe_w1b, moe_w3b, moe_w2b = moe_w1.astype(BF16), moe_w3.astype(BF16), moe_w2.astype(BF16)

    def keys_of(st, new, cache):
        B, T = new.shape[:2]
        if st.past == 0:
            return new.reshape(B, T, -1), T, False
        return gather_rows_t(cache, page_table, new), st.past + T, True

    def attend(st, q, K, V, L, major, **kw):
        Lk = K.shape[2] if major else K.shape[1]
        kt = _kv_tile(Lk)
        if st.tq < 16 and Lk % (2 * kt) == 0:
            kt *= 2
        return flash_attention(q, K, V, tq=st.tq, kt=kt, qpos_base=st.past, kv_len=L,
                               k_major=major, v_major=major, **kw)

    def channels_major(a, major):
        return a if major else a.transpose(0, 2, 1)

    def fox(st, y, o):
        B, T = st.B, st.T
        q, k, v, fl = _split(y, [N_HEADS * HEAD_DIM, KV_DIM, KV_DIM, N_HEADS])
        q = _head_norm(q.reshape(B, T, N_HEADS, HEAD_DIM), fox_q_norm)
        k = _head_norm(k.reshape(B, T, N_KV_HEADS, HEAD_DIM), fox_k_norm)
        v = v.reshape(B, T, N_KV_HEADS, HEAD_DIM)
        logf = jax.nn.log_sigmoid(fl.reshape(B, T, N_HEADS) + fox_b_f)
        o.update(fox_k=k, fox_v=v, fox_logf=logf)
        K, L, major = keys_of(st, k, cache_fox_k)
        V, _, _ = keys_of(st, v, cache_fox_v)
        LF, _, lf_major = keys_of(st, logf, cache_fox_logf)
        LF = channels_major(LF, lf_major)
        F = cumsum_lanes(LF.reshape(B * N_HEADS, -1)).reshape(LF.shape) * LOG2E
        fq = _rows_gt(lax.slice_in_dim(F, st.past, st.past + T, axis=2).transpose(0, 2, 1), st.tq)
        return [attend(st, _qrows(q, st.tq), K, V, L, major, mode='fox', fq=fq, fkT=F)], None

    def moba(st, y, o):
        B, T = st.B, st.T
        q, k, v = _split(y, [N_HEADS * HEAD_DIM, KV_DIM, KV_DIM])
        q = _rope(_head_norm(q.reshape(B, T, N_HEADS, HEAD_DIM), moba_q_norm), st.pos)
        k = _rope(_head_norm(k.reshape(B, T, N_KV_HEADS, HEAD_DIM), moba_k_norm), st.pos)
        v = v.reshape(B, T, N_KV_HEADS, HEAD_DIM)
        o.update(moba_k=k, moba_v=v)
        K, L, major = keys_of(st, k, cache_moba_k)
        V, _, _ = keys_of(st, v, cache_moba_v)
        Kt = channels_major(K, major)
        NBP = _round_up(Kt.shape[2] // MOBA_BLOCK, LANES)
        kbar = block_means_t(Kt, MOBA_BLOCK, NBP)
        kbt = jnp.tile(kbar.reshape(B, N_KV_HEADS, 1, HEAD_DIM, NBP), (1, 1, GROUP, 1, 1))
        kbt = kbt.reshape(B, N_KV_HEADS, KV_DIM, NBP)
        sel = moba_select(q.reshape(B, T, -1), kbt, tq=st.tq, qpos_base=st.past)
        return [attend(st, _qrows(q, st.tq), K, V, L, major, mode='blocks', sel=sel, blk=MOBA_BLOCK)], None

    def compress(st, raw_new, cache, pe, w1, w2):
        B, T = st.B, st.T
        if st.past == 0:
            lo, hi = compress_lohi(raw_new.reshape(B * T, KV_DIM), pe, w1)
            nblk = T // CMP_STRIDE
            lo, hi = lo.reshape(B, nblk, KV_DIM), hi.reshape(B, nblk, KV_DIM)
            L = T
        else:
            P = cache.shape[0]
            lo_p, hi_p = compress_lohi(pool_t(cache), pe, w1, pool_major=True)
            tail = jnp.pad(raw_new.reshape(B, T, KV_DIM), ((0, 0), (0, page - T), (0, 0)))
            lo_t, hi_t = compress_lohi(tail.reshape(B * page, KV_DIM), pe, w1)
            bpp = page // CMP_STRIDE
            lo = gather_pages(lo_p.reshape(P, bpp, KV_DIM), page_table, lo_t.reshape(B, bpp, KV_DIM))
            hi = gather_pages(hi_p.reshape(P, bpp, KV_DIM), page_table, hi_t.reshape(B, bpp, KV_DIM))
            L = st.past + T
        nc = _round_up(L, SLC_BLOCK) // CMP_STRIDE - 1
        NCP = _round_up(nc, LANES)
        lo = _pad_len(lo, NCP + 1)
        hi = _pad_len(hi, NCP + 1)
        out = compress_out(lo[:, :NCP].reshape(B * NCP, KV_DIM), hi[:, 1:NCP + 1].reshape(B * NCP, KV_DIM), w2)
        return out.reshape(B, NCP, KV_DIM), nc

    def nsa(st, y, o, win_k, win_v):
        B, T = st.B, st.T
        q, kc, vc, ks, vs, kw, vw, gl = _split(y, [N_HEADS * HEAD_DIM] + [KV_DIM] * 6 + [3 * N_HEADS])
        heads = lambda a: a.reshape(B, T, N_KV_HEADS, HEAD_DIM)
        q = _rope(_head_norm(q.reshape(B, T, N_HEADS, HEAD_DIM), nsa_q_norm), st.pos)
        ks = _rope(_head_norm(heads(ks), nsa_ks_norm), st.pos)
        kw = _rope(_head_norm(heads(kw), nsa_kw_norm), st.pos)
        kc, vc, vs, vw = heads(kc), heads(vc), heads(vs), heads(vw)
        gates = jax.nn.sigmoid(gl).reshape(B, T, N_HEADS, 3)
        o.update(nsa_cmp_k=kc, nsa_cmp_v=vc, nsa_slc_k=ks, nsa_slc_v=vs)
        qp = _qrows(q, st.tq)
        kcc, nc = compress(st, kc, cache_nsa_cmp_k, nsa_pe_k, nsa_w1_k, nsa_w2_k)
        vcc, _ = compress(st, vc, cache_nsa_cmp_v, nsa_pe_v, nsa_w1_v, nsa_w2_v)
        NCP = kcc.shape[1]
        cend = jnp.arange(NCP) * CMP_STRIDE + CMP_LEN - 1
        kcc = _rope(_head_norm(kcc.reshape(B, NCP, N_KV_HEADS, HEAD_DIM), nsa_kc_norm), cend)
        KS, L, major = keys_of(st, ks, cache_nsa_slc_k)
        VS, _, _ = keys_of(st, vs, cache_nsa_slc_v)
        nbs = _round_up(L, SLC_BLOCK) // SLC_BLOCK
        NBP = _round_up(nbs, LANES)
        ci = np.arange(NCP)[:, None]
        sj = np.arange(NBP)[None, :]
        cover = ((ci * CMP_STRIDE < (sj + 1) * SLC_BLOCK) & (ci * CMP_STRIDE + CMP_LEN > sj * SLC_BLOCK)
                 & (ci < nc) & (sj < nbs))
        oc, sel = nsa_compressed(qp, kcc.reshape(B, NCP, KV_DIM).astype(BF16), vcc.astype(BF16),
                                 jnp.asarray(cover, BF16), tq=st.tq, qpos_base=st.past, nc=nc)
        osl = attend(st, qp, KS, VS, L, major, mode='blocks', sel=sel, blk=SLC_BLOCK)
        if st.past == 0:
            KW, VW, kbase = kw.reshape(B, T, KV_DIM), vw.reshape(B, T, KV_DIM), 0
            wb = min(WINDOW, T)
            o.update(nsa_win_k=kw[:, T - wb:], nsa_win_v=vw[:, T - wb:])
        else:
            KWf = jnp.concatenate([win_k, kw], axis=1)
            VWf = jnp.concatenate([win_v, vw], axis=1)
            lw = KWf.shape[1]
            wb = min(WINDOW, lw)
            o.update(nsa_win_k=KWf[:, lw - wb:], nsa_win_v=VWf[:, lw - wb:])
            kbase = st.past - win_k.shape[1]
            Lp = _round_up(lw, LANES)
            KW = _pad_len(KWf.reshape(B, lw, KV_DIM), Lp)
            VW = _pad_len(VWf.reshape(B, lw, KV_DIM), Lp)
        ow = flash_attention(qp, KW, VW, tq=st.tq, kt=_kv_tile(KW.shape[1]), qpos_base=st.past,
                             kpos_base=kbase, kv_len=KW.shape[1], mode='plain', window=WINDOW)
        return [oc, osl, ow], gates

    def dsa(st, y, o):
        B, T, tq = st.B, st.T, st.tq
        q, k, v, qi, ki, wi = _split(y, [N_HEADS * HEAD_DIM, KV_DIM, KV_DIM, IDX_HEADS * IDX_DIM, IDX_DIM, IDX_HEADS])
        q = _rope(_head_norm(q.reshape(B, T, N_HEADS, HEAD_DIM), dsa_q_norm), st.pos)
        k = _rope(_head_norm(k.reshape(B, T, N_KV_HEADS, HEAD_DIM), dsa_k_norm), st.pos)
        v = v.reshape(B, T, N_KV_HEADS, HEAD_DIM)
        qi = _rope(qi.reshape(B, T, IDX_HEADS, IDX_DIM), st.pos)
        ki = _rope(ki.reshape(B, T, 1, IDX_DIM), st.pos)[:, :, 0]
        wi = wi * IDX_HEADS ** -0.5
        o.update(dsa_k=k, dsa_v=v, dsa_idx=ki)
        K, L, major = keys_of(st, k, cache_dsa_k)
        V, _, _ = keys_of(st, v, cache_dsa_v)
        KI, _, ki_major = keys_of(st, ki, cache_dsa_idx_k)
        KI = channels_major(KI, ki_major)
        kt = _kv_tile(KI.shape[2])
        nQ = T // tq
        qi_r = qi.astype(BF16).reshape(B, nQ, tq, IDX_HEADS, IDX_DIM).transpose(0, 1, 3, 2, 4)
        qi_r = qi_r.reshape(B, nQ, IDX_HEADS * tq, IDX_DIM)
        wi_r = wi.reshape(B, nQ, tq, IDX_HEADS)
        n_sel = min(DSA_TOPK, L // 4)
        picked = dsa_select(qi_r, wi_r, KI, tq=tq, kt=kt, qpos_base=st.past, n_sel=n_sel)
        return [attend(st, _qrows(q, tq), K, V, L, major, mode='mask', mask=picked)], None

    for layer in range(ada_w.shape[0]):
        kind = ('fox', 'moba', 'nsa', 'dsa')[layer % 4]
        for s, st in enumerate(streams):
            x = xs[s]
            sh1, sc1, g1, sh2, sc2, g2 = _modulation(conds[s], ada_w[layer], ada_b[layer])
            y = fused_matmul(x, w_in[kind], seq_len=st.T, tm=st.tm,
                             norm=(norm_mix[layer][None, :], sc1, sh1)).reshape(st.B, st.T, -1)
            if kind == 'fox':
                branches, gates = fox(st, y, outs[s])
            elif kind == 'moba':
                branches, gates = moba(st, y, outs[s])
            elif kind == 'nsa':
                branches, gates = nsa(st, y, outs[s], state_nsa_win_k, state_nsa_win_v)
            else:
                branches, gates = dsa(st, y, outs[s])
            if st.tq % 16 == 0:
                gts = None if gates is None else _gate_rows(gates, st.tq)
                x = proj_heads(branches, w_out[kind], x, g1, tq=st.tq, gts=gts)
            else:
                att = [_unpad_out(a, st.tq) for a in branches]
                if gates is not None:
                    att = [sum(gates[..., c:c + 1] * a for c, a in enumerate(att))]
                x = fused_matmul(att[0].reshape(-1, N_HEADS * HEAD_DIM), w_out[kind], seq_len=st.T, tm=st.tm,
                                 epi=(x, g1))
            ffn_norm = (norm_ffn[layer][None, :], sc2, sh2)
            coef = route(x, router_w, router_b, ffn_norm, seq_len=st.T, tm=st.tm)
            xs[s] = moe_dense(x, coef, *ffn_norm, g2, moe_w1b[layer], moe_w3b[layer], moe_w2b[layer],
                              seq_len=st.T, tm=2 * st.tm)

    names = ('fox_k', 'fox_v', 'fox_logf', 'moba_k', 'moba_v', 'nsa_cmp_k', 'nsa_cmp_v', 'nsa_slc_k', 'nsa_slc_v',
             'nsa_win_k', 'nsa_win_v', 'dsa_k', 'dsa_v', 'dsa_idx')
    res = [xs[0].reshape(x_prompt.shape), xs[1].reshape(x_sample.shape)]
    for n in names:
        res += [outs[0][n], outs[1][n]]
    return tuple(res)
```

```python
import functools

import numpy as np
import jax
import jax.numpy as jnp
from jax import lax
from jax.experimental import pallas as pl
from jax.experimental.pallas import tpu as pltpu

HEAD_DIM = 64
N_KV_HEADS = 4
GROUP = 4
N_HEADS = N_KV_HEADS * GROUP
KV_DIM = N_KV_HEADS * HEAD_DIM
ROT_DIM = HEAD_DIM // 4
ROPE_THETA = 500000.0
ATTN_SCALE = HEAD_DIM ** -0.5
LOG2E = 1.4426950408889634
MOBA_BLOCK = 256
MOBA_TOPK = 3
CMP_LEN = 32
CMP_STRIDE = 16
CMP_HIDDEN = 64
SLC_BLOCK = 64
SLC_TOPK = 16
WINDOW = 512
IDX_HEADS = 8
IDX_DIM = 64
DSA_TOPK = 256
N_EXPERTS = 16
N_GROUPS = 4
EXPERTS_PER_GROUP = N_EXPERTS // N_GROUPS
TOP_K = 2
EPS = 1e-6
NEG_INF = -1e30

LANES = 128
VMEM_LIMIT = 56 * 1024 * 1024
BF16 = jnp.bfloat16
F32 = jnp.float32
INT_MIN = -2 ** 31


def _round_up(a, m):
    return (a + m - 1) // m * m


def _params(sem):
    return pltpu.CompilerParams(dimension_semantics=sem, vmem_limit_bytes=VMEM_LIMIT)


def _gather_kernel(pt_ref, *refs, pps, n_main):
    page_refs, tail_ref, o_ref = refs[:pps], refs[pps], refs[pps + 1]
    j = pl.program_id(1)

    @pl.when(j < n_main)
    def _():
        for p in range(pps):
            o_ref[p] = page_refs[p][...]

    @pl.when(j == n_main)
    def _():
        o_ref[0] = tail_ref[...]
        for p in range(1, pps):
            o_ref[p] = jnp.zeros(o_ref.shape[1:], o_ref.dtype)


def gather_pages(pool, page_table, tail):
    _, R, C = pool.shape
    B, n_pages = page_table.shape
    pps = max(d for d in (8, 4, 2, 1) if n_pages % d == 0)
    n_main = n_pages // pps

    def page_map(b, j, pt, p):
        return (pt[b, jnp.minimum(j * pps + p, n_pages - 1)], 0, 0)

    in_specs = [pl.BlockSpec((None, R, C), functools.partial(page_map, p=p)) for p in range(pps)]
    in_specs.append(pl.BlockSpec((None, R, C), lambda b, j, pt: (b, 0, 0)))
    out = pl.pallas_call(
        functools.partial(_gather_kernel, pps=pps, n_main=n_main), name='gather_pages',
        grid_spec=pltpu.PrefetchScalarGridSpec(
            num_scalar_prefetch=1, grid=(B, n_main + 1), in_specs=in_specs,
            out_specs=pl.BlockSpec((None, pps, R, C), lambda b, j, pt: (b, j, 0, 0))),
        out_shape=jax.ShapeDtypeStruct((B, (n_main + 1) * pps, R, C), pool.dtype),
        compiler_params=_params(("arbitrary", "arbitrary")),
    )(page_table, *([pool] * pps), tail)
    return out.reshape(B, (n_main + 1) * pps * R, C)


def _gather_t_kernel(pt_ref, *refs, pps, n_main, page):
    page_refs, tail_ref, o_ref = refs[:pps], refs[pps], refs[pps + 1]
    j = pl.program_id(1)

    @pl.when(j < n_main)
    def _():
        for p in range(pps):
            o_ref[:, p * page:(p + 1) * page] = page_refs[p][...].astype(o_ref.dtype)

    @pl.when(j == n_main)
    def _():
        o_ref[:, :page] = tail_ref[...].astype(o_ref.dtype)
        o_ref[:, page:] = jnp.zeros((o_ref.shape[0], (pps - 1) * page), o_ref.dtype)


def pool_t(pool):
    P, page = pool.shape[:2]
    return jnp.moveaxis(pool.reshape(P, page, -1), 1, 2)


def gather_rows_t(pool, page_table, new_rows, dtype=None):
    dtype = pool.dtype if dtype is None else dtype
    P, page = pool.shape[:2]
    B, n_pages = page_table.shape
    T = new_rows.shape[1]
    pt = pool_t(pool)
    C = pt.shape[1]
    tail = jnp.pad(new_rows.reshape(B, T, C), ((0, 0), (0, page - T), (0, 0))).transpose(0, 2, 1)
    pps = max(d for d in (8, 4, 2, 1) if n_pages % d == 0)
    n_main = n_pages // pps

    def page_map(b, j, tab, p):
        return (tab[b, jnp.minimum(j * pps + p, n_pages - 1)], 0, 0)

    in_specs = [pl.BlockSpec((None, C, page), functools.partial(page_map, p=p)) for p in range(pps)]
    in_specs.append(pl.BlockSpec((None, C, page), lambda b, j, tab: (b, 0, 0)))
    return pl.pallas_call(
        functools.partial(_gather_t_kernel, pps=pps, n_main=n_main, page=page), name='gather_pages_t',
        grid_spec=pltpu.PrefetchScalarGridSpec(
            num_scalar_prefetch=1, grid=(B, n_main + 1), in_specs=in_specs,
            out_specs=pl.BlockSpec((None, C, pps * page), lambda b, j, tab: (b, 0, j))),
        out_shape=jax.ShapeDtypeStruct((B, C, (n_main + 1) * pps * page), dtype),
        compiler_params=_params(("arbitrary", "arbitrary")),
    )(page_table, *([pt] * pps), tail)


def _adanorm(x, g, sc, sh):
    y = x * lax.rsqrt(jnp.mean(x * x, axis=-1, keepdims=True) + EPS)
    return (y * g) * (1.0 + sc) + sh


def _matmul_kernel(*refs, has_norm, has_epi):
    it = iter(refs)
    x_ref, w_ref = next(it), next(it)
    x = x_ref[...]
    if has_norm:
        g_ref, sc_ref, sh_ref = next(it), next(it), next(it)
        x = _adanorm(x, g_ref[...], sc_ref[...], sh_ref[...])
    acc = jnp.dot(x.astype(BF16), w_ref[...], preferred_element_type=F32)
    if has_epi:
        res_ref, gate_ref = next(it), next(it)
        acc = res_ref[...] + gate_ref[...] * acc
    o_ref = next(it)
    o_ref[...] = acc


def _mod_spec(mod, tm, seq_len):
    D = mod.shape[-1]
    if seq_len % tm == 0:
        tps = seq_len // tm
        return mod, pl.BlockSpec((None, 1, D), lambda i: (i // tps, 0, 0))
    rows = jnp.broadcast_to(mod, (mod.shape[0], seq_len, D)).reshape(-1, D)
    return rows, pl.BlockSpec((tm, D), lambda i: (i, 0))


def fused_matmul(x, w, *, seq_len, tm, norm=None, epi=None):
    N, K = x.shape
    M = w.shape[1]
    tm = min(tm, N)
    args = [x, w]
    specs = [pl.BlockSpec((tm, K), lambda i: (i, 0)), pl.BlockSpec((K, M), lambda i: (0, 0))]
    if norm is not None:
        gain, sc, sh = norm
        args.append(gain)
        specs.append(pl.BlockSpec((1, K), lambda i: (0, 0)))
        for m in (sc, sh):
            a, s = _mod_spec(m, tm, seq_len)
            args.append(a)
            specs.append(s)
    if epi is not None:
        res, gate = epi
        args.append(res)
        specs.append(pl.BlockSpec((tm, M), lambda i: (i, 0)))
        a, s = _mod_spec(gate, tm, seq_len)
        args.append(a)
        specs.append(s)
    return pl.pallas_call(
        functools.partial(_matmul_kernel, has_norm=norm is not None, has_epi=epi is not None),
        name='proj_in' if norm is not None else 'proj_out',
        grid=(N // tm,), in_specs=specs,
        out_specs=pl.BlockSpec((tm, M), lambda i: (i, 0)),
        out_shape=jax.ShapeDtypeStruct((N, M), F32),
        compiler_params=_params(("arbitrary",)),
    )(*args)


def _proj_heads_kernel(*refs, n_branch, tq):
    it = iter(refs)
    o_refs = [next(it) for _ in range(n_branch)]
    gts_ref = next(it) if n_branch > 1 else None
    w_ref, res_ref, gate_ref, out_ref = next(it), next(it), next(it), next(it)
    R = GROUP * tq
    if n_branch == 1:
        o = o_refs[0][...]
    else:
        gts = gts_ref[...]
        o = None
        for c in range(n_branch):
            gfull = jnp.zeros((R, KV_DIM), F32)
            for h in range(N_KV_HEADS):
                col = h * n_branch + c
                gfull = jnp.where(_lane_head_mask(R, h), gts[:, col:col + 1], gfull)
            o = gfull * o_refs[c][...] if o is None else o + gfull * o_refs[c][...]
    ob = o.astype(BF16)
    acc = jnp.zeros(out_ref.shape, F32)
    for g in range(GROUP):
        acc += jnp.dot(ob[g * tq:(g + 1) * tq], w_ref[g], preferred_element_type=F32)
    out_ref[...] = res_ref[...] + gate_ref[...] * acc


def proj_heads(branches, w_o, res, gate, *, tq, gts=None):
    B, nQ, R, _ = branches[0].shape
    D = w_o.shape[1]
    nb = len(branches)
    w = w_o.reshape(N_KV_HEADS, GROUP, HEAD_DIM, D).transpose(1, 0, 2, 3).reshape(GROUP, KV_DIM, D)
    tile = lambda b, i: (b, i, 0, 0)
    args = list(branches)
    specs = [pl.BlockSpec((None, None, R, KV_DIM), tile)] * nb
    if nb > 1:
        args.append(gts)
        specs.append(pl.BlockSpec((None, None, R, gts.shape[-1]), tile))
    args += [w, res.reshape(B, nQ, tq, D), gate]
    specs += [pl.BlockSpec((GROUP, KV_DIM, D), lambda b, i: (0, 0, 0)),
              pl.BlockSpec((None, None, tq, D), tile), pl.BlockSpec((None, 1, D), lambda b, i: (b, 0, 0))]
    out = pl.pallas_call(
        functools.partial(_proj_heads_kernel, n_branch=nb, tq=tq), name='proj_heads',
        grid=(B, nQ), in_specs=specs,
        out_specs=pl.BlockSpec((None, None, tq, D), tile),
        out_shape=jax.ShapeDtypeStruct((B, nQ, tq, D), F32),
        compiler_params=_params(("arbitrary", "arbitrary")),
    )(*args)
    return out.reshape(B * nQ * tq, D)


ROUTE_ROWS = 8


def _first_match(vals, target):
    out, seen = [], None
    for v in vals:
        hit = v == target
        out.append(hit if seen is None else hit & jnp.logical_not(seen))
        seen = hit if seen is None else seen | hit
    return out


def _router_kernel(x_ref, g_ref, sc_ref, sh_ref, wt_ref, b_ref, o_ref):
    h = _adanorm(x_ref[...], g_ref[...], sc_ref[...], sh_ref[...]).astype(BF16)
    logit = lax.dot_general(wt_ref[...], h, (((1,), (1,)), ((), ())), preferred_element_type=F32)
    aff = jax.nn.sigmoid(logit)
    grp = aff + b_ref[...]
    n = EXPERTS_PER_GROUP
    a = [grp[k * ROUTE_ROWS:(k + 1) * ROUTE_ROWS] for k in range(n)]
    f = [aff[k * ROUTE_ROWS:(k + 1) * ROUTE_ROWS] for k in range(n)]
    top1 = functools.reduce(jnp.maximum, a)
    s1 = _first_match(a, top1)
    rest = [jnp.where(s, -jnp.inf, v) for s, v in zip(s1, a)]
    top2 = functools.reduce(jnp.maximum, rest)
    s2 = _first_match(rest, top2)
    row = lax.broadcasted_iota(jnp.int32, top1.shape, 0).astype(F32)
    score = jnp.where(row < N_GROUPS, top1 + top2, -jnp.inf)
    best = jnp.max(score, axis=0, keepdims=True)
    gsel = row == jnp.min(jnp.where(score == best, row, 1e9), axis=0, keepdims=True)
    w1 = sum(jnp.where(s, v, 0.0) for s, v in zip(s1, f))
    w2 = sum(jnp.where(s, v, 0.0) for s, v in zip(s2, f))
    den = w1 + w2
    for k in range(n):
        o_ref[k * ROUTE_ROWS:(k + 1) * ROUTE_ROWS, :] = jnp.where(
            gsel & s1[k], w1 / den, jnp.where(gsel & s2[k], w2 / den, 0.0))


def route(x, router_w, router_b, norm, *, seq_len, tm):
    assert TOP_K == 2
    N, D = x.shape
    tm = min(tm, N)
    wt = router_w.T.reshape(N_GROUPS, EXPERTS_PER_GROUP, D).transpose(1, 0, 2)
    wt = jnp.pad(wt, ((0, 0), (0, ROUTE_ROWS - N_GROUPS), (0, 0))).reshape(-1, D).astype(BF16)
    bt = router_b.reshape(N_GROUPS, EXPERTS_PER_GROUP).T
    bt = jnp.pad(bt, ((0, 0), (0, ROUTE_ROWS - N_GROUPS))).reshape(-1, 1)
    rows = wt.shape[0]
    gain, sc, sh = norm
    args, specs = [x, gain], [pl.BlockSpec((tm, D), lambda i: (i, 0)), pl.BlockSpec((1, D), lambda i: (0, 0))]
    for m in (sc, sh):
        a, s = _mod_spec(m, tm, seq_len)
        args.append(a)
        specs.append(s)
    args += [wt, bt]
    specs += [pl.BlockSpec((rows, D), lambda i: (0, 0)), pl.BlockSpec((rows, 1), lambda i: (0, 0))]
    coef_t = pl.pallas_call(
        _router_kernel, name='router', grid=(N // tm,), in_specs=specs,
        out_specs=pl.BlockSpec((rows, tm), lambda i: (0, i)),
        out_shape=jax.ShapeDtypeStruct((rows, N), F32),
        compiler_params=_params(("arbitrary",)),
    )(*args)
    coef_t = coef_t.reshape(EXPERTS_PER_GROUP, ROUTE_ROWS, N)[:, :N_GROUPS]
    return coef_t.transpose(2, 1, 0).reshape(N, N_EXPERTS)


def _cumsum_kernel(x_ref, o_ref, carry_ref, *, tc):
    @pl.when(pl.program_id(0) == 0)
    def _():
        carry_ref[...] = jnp.zeros_like(carry_ref)

    r = lax.broadcasted_iota(jnp.int32, (tc, tc), 0)
    c = lax.broadcasted_iota(jnp.int32, (tc, tc), 1)
    out = jnp.dot(x_ref[...], (r <= c).astype(F32), preferred_element_type=F32,
                  precision=lax.Precision.HIGHEST) + carry_ref[...]
    o_ref[...] = out
    carry_ref[...] = out[:, tc - 1:tc]


def cumsum_lanes(x, tc=256):
    rows, L = x.shape
    tc = min(tc, L)
    return pl.pallas_call(
        functools.partial(_cumsum_kernel, tc=tc), name='cumsum',
        grid=(L // tc,),
        in_specs=[pl.BlockSpec((rows, tc), lambda j: (0, j))],
        out_specs=pl.BlockSpec((rows, tc), lambda j: (0, j)),
        out_shape=jax.ShapeDtypeStruct((rows, L), F32),
        scratch_shapes=[pltpu.VMEM((rows, 1), F32)],
        compiler_params=_params(("arbitrary",)),
    )(x)


def _select_blocks(score, own, n_top):
    col = lax.broadcasted_iota(jnp.int32, score.shape, 1)
    colf = col.astype(F32)
    sc = jnp.where(col < own, score, -jnp.inf)
    sel = col == own
    for _ in range(n_top):
        m = jnp.max(sc, axis=-1, keepdims=True)
        idx = jnp.min(jnp.where(sc == m, colf, 1e9), axis=-1, keepdims=True)
        hit = colf == idx
        sel = sel | (hit & (m > -jnp.inf))
        sc = jnp.where(hit, -jnp.inf, sc)
    return sel


def _row_pos(tq, rows, base):
    assert tq & (tq - 1) == 0
    r = lax.broadcasted_iota(jnp.int32, (rows, 1), 0)
    return base + (r & (tq - 1))


def _moba_select_kernel(q_ref, kb_ref, o_ref, *, tq, qpos_base):
    qpos = _row_pos(tq, tq, qpos_base + pl.program_id(1) * tq)
    own = lax.shift_right_logical(qpos, int(np.log2(MOBA_BLOCK)))
    for h in range(N_KV_HEADS):
        gate = jnp.dot(q_ref[:, h * KV_DIM:(h + 1) * KV_DIM], kb_ref[h], preferred_element_type=F32)
        o_ref[h] = _select_blocks(gate, own, MOBA_TOPK).astype(F32)


def moba_select(q_nat, kbt, *, tq, qpos_base):
    B, T, D = q_nat.shape
    NBP = kbt.shape[-1]
    return pl.pallas_call(
        functools.partial(_moba_select_kernel, tq=tq, qpos_base=qpos_base), name='moba_select',
        grid=(B, T // tq),
        in_specs=[pl.BlockSpec((None, tq, D), lambda b, i: (b, i, 0)),
                  pl.BlockSpec((None, N_KV_HEADS, KV_DIM, NBP), lambda b, i: (b, 0, 0, 0))],
        out_specs=pl.BlockSpec((None, N_KV_HEADS, tq, NBP), lambda b, i: (b, 0, i, 0)),
        out_shape=jax.ShapeDtypeStruct((B, N_KV_HEADS, T, NBP), F32),
        compiler_params=_params(("arbitrary", "arbitrary")),
    )(q_nat, kbt)


def _block_mean_kernel(k_ref, o_ref, *, tl, blk):
    j = pl.program_id(1)

    @pl.when(j == 0)
    def _():
        o_ref[...] = jnp.zeros_like(o_ref)

    nbp = o_ref.shape[-1]
    key_blk = lax.shift_right_logical(j * tl + lax.broadcasted_iota(jnp.int32, (tl, nbp), 0), int(np.log2(blk)))
    avg = jnp.where(key_blk == lax.broadcasted_iota(jnp.int32, (tl, nbp), 1), 1.0 / blk, 0.0)
    o_ref[...] += jnp.dot(k_ref[...], avg, preferred_element_type=F32, precision=lax.Precision.HIGHEST)


def block_means_t(kt_, blk, nbp):
    B, C, L = kt_.shape
    tl = max(d for d in (1024, 512, 256, 128) if L % d == 0)
    return pl.pallas_call(
        functools.partial(_block_mean_kernel, tl=tl, blk=blk), name='block_means',
        grid=(B, L // tl),
        in_specs=[pl.BlockSpec((None, C, tl), lambda b, j: (b, 0, j))],
        out_specs=pl.BlockSpec((None, C, nbp), lambda b, j: (b, 0, 0)),
        out_shape=jax.ShapeDtypeStruct((B, C, nbp), F32),
        compiler_params=_params(("arbitrary", "arbitrary")),
    )(kt_)


def _compress_lohi_kernel(x_ref, pe_lo_ref, pe_hi_ref, wlo_ref, whi_ref, lo_ref, hi_ref, *scratch,
                          nblk, pool_major):
    if pool_major:
        xa_ref, xb_ref = scratch
        page = x_ref.shape[-1]
        for p in range(x_ref.shape[0]):
            xt = x_ref[p].T
            xa_ref[p * page:(p + 1) * page, :] = xt[:, :LANES]
            xb_ref[p * page:(p + 1) * page, :] = xt[:, LANES:]
    lo = jnp.zeros((nblk, KV_DIM), F32)
    hi = jnp.zeros((nblk, KV_DIM), F32)
    for t in range(CMP_STRIDE):
        if pool_major:
            x = jnp.concatenate([xa_ref[pl.ds(t, nblk, stride=CMP_STRIDE), :],
                                 xb_ref[pl.ds(t, nblk, stride=CMP_STRIDE), :]], axis=-1)
        else:
            x = jnp.concatenate([x_ref[pl.ds(2 * t, nblk, stride=2 * CMP_STRIDE), :],
                                 x_ref[pl.ds(2 * t + 1, nblk, stride=2 * CMP_STRIDE), :]], axis=-1)
        lo += jnp.dot((x + pe_lo_ref[t:t + 1, :]).astype(BF16), wlo_ref[t], preferred_element_type=F32)
        hi += jnp.dot((x + pe_hi_ref[t:t + 1, :]).astype(BF16), whi_ref[t], preferred_element_type=F32)
    lo_ref[...] = lo
    hi_ref[...] = hi


def compress_lohi(raw, pe, w1, tr=2048, pool_major=False):
    if pool_major:
        P, _, page = raw.shape
        N = P * page
    else:
        N = raw.shape[0]
    tr = max(d for d in (tr, tr // 2, tr // 4, tr // 8, LANES) if N % d == 0)
    nblk = tr // CMP_STRIDE
    if pool_major:
        x_arg = raw
        x_spec = pl.BlockSpec((tr // page, KV_DIM, page), lambda i: (i, 0, 0))
        scratch = [pltpu.VMEM((tr, LANES), F32), pltpu.VMEM((tr, LANES), F32)]
    else:
        x_arg = raw.reshape(2 * N, LANES)
        x_spec = pl.BlockSpec((2 * tr, LANES), lambda i: (i, 0))
        scratch = []
    eye = jnp.eye(N_KV_HEADS, dtype=F32)

    def blockdiag(w):
        w = w.reshape(CMP_STRIDE, HEAD_DIM, CMP_HIDDEN)
        return jnp.einsum('tdc,hk->thdkc', w, eye).reshape(CMP_STRIDE, KV_DIM, KV_DIM).astype(BF16)

    half = CMP_STRIDE * HEAD_DIM
    pe_lo = jnp.tile(pe[:CMP_STRIDE], (1, N_KV_HEADS))
    pe_hi = jnp.tile(pe[CMP_STRIDE:], (1, N_KV_HEADS))
    const = lambda i: (0, 0)
    const3 = lambda i: (0, 0, 0)
    return pl.pallas_call(
        functools.partial(_compress_lohi_kernel, nblk=nblk, pool_major=pool_major), name='compress_lohi',
        grid=(N // tr,),
        in_specs=[x_spec,
                  pl.BlockSpec((CMP_STRIDE, KV_DIM), const), pl.BlockSpec((CMP_STRIDE, KV_DIM), const),
                  pl.BlockSpec((CMP_STRIDE, KV_DIM, KV_DIM), const3),
                  pl.BlockSpec((CMP_STRIDE, KV_DIM, KV_DIM), const3)],
        out_specs=[pl.BlockSpec((nblk, KV_DIM), lambda i: (i, 0))] * 2,
        out_shape=[jax.ShapeDtypeStruct((N // CMP_STRIDE, KV_DIM), F32)] * 2,
        scratch_shapes=scratch,
        compiler_params=_params(("arbitrary",)),
    )(x_arg, pe_lo, pe_hi, blockdiag(w1[:half]), blockdiag(w1[half:]))


def _compress_out_kernel(lo_ref, hi_ref, w_ref, o_ref):
    z = lo_ref[...] + hi_ref[...]
    z = z * jax.nn.sigmoid(z)
    o_ref[...] = jnp.dot(z.astype(BF16), w_ref[...], preferred_element_type=F32)


def compress_out(lo, hi_next, w2):
    N = lo.shape[0]
    tm = max(d for d in (512, 256, 128, 64, 32, 16, 8) if N % d == 0)
    w = jnp.einsum('cd,hk->hckd', w2, jnp.eye(N_KV_HEADS, dtype=F32)).reshape(KV_DIM, KV_DIM).astype(BF16)
    return pl.pallas_call(
        _compress_out_kernel, name='compress_out', grid=(N // tm,),
        in_specs=[pl.BlockSpec((tm, KV_DIM), lambda i: (i, 0)), pl.BlockSpec((tm, KV_DIM), lambda i: (i, 0)),
                  pl.BlockSpec((KV_DIM, KV_DIM), lambda i: (0, 0))],
        out_specs=pl.BlockSpec((tm, KV_DIM), lambda i: (i, 0)),
        out_shape=jax.ShapeDtypeStruct((N, KV_DIM), F32),
        compiler_params=_params(("arbitrary",)),
    )(lo, hi_next, w)


def _lane_head_mask(rows, h):
    lane = lax.broadcasted_iota(jnp.int32, (rows, KV_DIM), 1)
    return (lane >= h * HEAD_DIM) & (lane < (h + 1) * HEAD_DIM)


def _spread_head(qh, h):
    d = lax.broadcasted_iota(jnp.int32, (HEAD_DIM, KV_DIM), 0)
    lane = lax.broadcasted_iota(jnp.int32, (HEAD_DIM, KV_DIM), 1)
    place = (lane == d + h * HEAD_DIM).astype(BF16)
    return jnp.dot(qh, place, preferred_element_type=F32).astype(BF16)


def _nsa_cmp_kernel(q_ref, kc_ref, vc_ref, cover_ref, o_ref, sel_ref, *, tq, qpos_base, nc):
    R = GROUP * tq
    NCP = kc_ref.shape[0]
    q0 = qpos_base + pl.program_id(1) * tq
    qpos = _row_pos(tq, R, q0)
    c = lax.broadcasted_iota(jnp.int32, (1, NCP), 1)
    mask = (c * CMP_STRIDE + (CMP_LEN - 1) <= qpos) & (c < nc)
    kc = kc_ref[...]
    vc = vc_ref[...]
    own = lax.shift_right_logical(_row_pos(tq, tq, q0), int(np.log2(SLC_BLOCK)))
    out = jnp.zeros((R, KV_DIM), F32)
    imps = []
    for h in range(N_KV_HEADS):
        s = lax.dot_general(_spread_head(q_ref[h], h), kc, (((1,), (1,)), ((), ())),
                            preferred_element_type=F32)
        s = jnp.where(mask, s, NEG_INF)
        m = jnp.max(s, axis=-1, keepdims=True)
        e = jnp.where(mask, jnp.exp2(s - m), 0.0)
        l = jnp.sum(e, axis=-1, keepdims=True)
        p = (e / jnp.where(l > 0.0, l, 1.0)).astype(BF16)
        out = jnp.where(_lane_head_mask(R, h), jnp.dot(p, vc, preferred_element_type=F32), out)
        imp = jnp.dot(p, cover_ref[...], preferred_element_type=F32)
        imps.append(jnp.sum(imp.reshape(GROUP, tq, imp.shape[-1]), axis=0))
    o_ref[...] = out
    sel = _select_blocks(jnp.concatenate(imps, axis=0), jnp.concatenate([own] * N_KV_HEADS, axis=0),
                         SLC_TOPK - 1).astype(F32)
    for h in range(N_KV_HEADS):
        sel_ref[h] = sel[h * tq:(h + 1) * tq]


def nsa_compressed(qpad, kcc, vcc, cover, *, tq, qpos_base, nc):
    B, nQ, _, R, _ = qpad.shape
    NCP, NBP = cover.shape
    return pl.pallas_call(
        functools.partial(_nsa_cmp_kernel, tq=tq, qpos_base=qpos_base, nc=nc), name='nsa_compressed',
        grid=(B, nQ),
        in_specs=[pl.BlockSpec((None, None, N_KV_HEADS, R, HEAD_DIM), lambda b, i: (b, i, 0, 0, 0)),
                  pl.BlockSpec((None, NCP, KV_DIM), lambda b, i: (b, 0, 0)),
                  pl.BlockSpec((None, NCP, KV_DIM), lambda b, i: (b, 0, 0)),
                  pl.BlockSpec((NCP, NBP), lambda b, i: (0, 0))],
        out_specs=[pl.BlockSpec((None, None, R, KV_DIM), lambda b, i: (b, i, 0, 0)),
                   pl.BlockSpec((None, N_KV_HEADS, tq, NBP), lambda b, i: (b, 0, i, 0))],
        out_shape=[jax.ShapeDtypeStruct((B, nQ, R, KV_DIM), F32),
                   jax.ShapeDtypeStruct((B, N_KV_HEADS, nQ * tq, NBP), F32)],
        compiler_params=_params(("arbitrary", "arbitrary")),
    )(qpad, kcc, vcc, cover)


def _index_keys(qi, wi, ki, kpos, qpos, tq):
    rel = jnp.maximum(jnp.dot(qi, ki, preferred_element_type=F32), 0.0)
    score = jnp.zeros((tq, rel.shape[1]), F32)
    for i in range(IDX_HEADS):
        score = score + wi[:, i:i + 1] * rel[i * tq:(i + 1) * tq]
    score = jnp.where(kpos <= qpos, score + 0.0, -jnp.inf)
    bits = lax.bitcast_convert_type(score, jnp.int32)
    return bits ^ (lax.shift_right_arithmetic(bits, 31) & 0x7FFFFFFF)


def _dsa_select_kernel(qi_ref, wi_ref, ki_ref, o_ref, key_ref, seen_ref, *, tq, kt, qpos_base, n_sel, nkt):
    q0 = qpos_base + pl.program_id(1) * tq
    qpos = _row_pos(tq, tq, q0)
    last = jnp.minimum((q0 + tq - 1) // kt, nkt - 1)
    lane = lax.broadcasted_iota(jnp.int32, (1, kt), 1)

    for jt in range(nkt):
        @pl.when(jt <= last)
        def _(jt=jt):
            ki = ki_ref[:, jt * kt:(jt + 1) * kt].astype(BF16)
            key_ref[jt] = _index_keys(qi_ref[...], wi_ref[...], ki, jt * kt + lane, qpos, tq)

    def count(pred):
        def body(jt, acc):
            hit = pred(key_ref[jt])
            for u in range(kt // LANES):
                acc = acc + jnp.where(hit[:, u * LANES:(u + 1) * LANES], 1.0, 0.0)
            return acc
        acc = lax.fori_loop(0, last + 1, body, jnp.zeros((tq, LANES), F32))
        return jnp.sum(acc, axis=-1, keepdims=True)

    def count_ge(t):
        return count(lambda key: key >= t)

    def bit_step(it, t_u):
        cand = t_u | lax.shift_left(jnp.int32(1), 31 - it)
        ok = count_ge(cand ^ INT_MIN) >= n_sel
        return jnp.where(ok, cand, t_u)

    t_u = lax.fori_loop(0, 32, bit_step, jnp.zeros((tq, 1), jnp.int32))
    thr = t_u ^ INT_MIN

    need = n_sel - count(lambda key: key > thr)

    r = lax.broadcasted_iota(jnp.int32, (kt, kt), 0)
    c = lax.broadcasted_iota(jnp.int32, (kt, kt), 1)
    tri = (r <= c).astype(F32)
    seen_ref[...] = jnp.zeros((tq, 1), F32)
    for jt in range(nkt):
        @pl.when(jt <= last)
        def _(jt=jt):
            key = key_ref[jt]
            eq = key == thr
            rank = seen_ref[...] + jnp.dot(eq.astype(F32), tri, preferred_element_type=F32)
            keep = ((key > thr) | (eq & (rank <= need))) & (jt * kt + lane <= qpos)
            o_ref[:, jt * kt:(jt + 1) * kt] = jnp.where(keep, 1.0, 0.0).astype(o_ref.dtype)
            seen_ref[...] = rank[:, kt - 1:kt]

        @pl.when(jt > last)
        def _(jt=jt):
            o_ref[:, jt * kt:(jt + 1) * kt] = jnp.zeros((tq, kt), o_ref.dtype)


def dsa_select(qi, wi, ki_t, *, tq, kt, qpos_base, n_sel):
    B, nQ = qi.shape[:2]
    Lk = ki_t.shape[2]
    nkt = Lk // kt
    return pl.pallas_call(
        functools.partial(_dsa_select_kernel, tq=tq, kt=kt, qpos_base=qpos_base, n_sel=n_sel, nkt=nkt),
        name='dsa_select',
        grid=(B, nQ),
        in_specs=[pl.BlockSpec((None, None, IDX_HEADS * tq, IDX_DIM), lambda b, i: (b, i, 0, 0)),
                  pl.BlockSpec((None, None, tq, IDX_HEADS), lambda b, i: (b, i, 0, 0)),
                  pl.BlockSpec((None, IDX_DIM, Lk), lambda b, i: (b, 0, 0))],
        out_specs=pl.BlockSpec((None, tq, Lk), lambda b, i: (b, i, 0)),
        out_shape=jax.ShapeDtypeStruct((B, nQ * tq, Lk), BF16 if tq % 16 == 0 else F32),
        scratch_shapes=[pltpu.VMEM((nkt, tq, kt), jnp.int32), pltpu.VMEM((tq, 1), F32)],
        compiler_params=_params(("arbitrary", "arbitrary")),
    )(qi, wi, ki_t)


class _PerHead:
    def __init__(self, refs):
        self.refs = refs

    def __getitem__(self, idx):
        idx = idx if isinstance(idx, tuple) else (idx,)
        return self.refs[idx[0]][idx[1:] if len(idx) > 1 else ...]

    def __setitem__(self, idx, value):
        idx = idx if isinstance(idx, tuple) else (idx,)
        self.refs[idx[0]][idx[1:] if len(idx) > 1 else ...] = value

    @property
    def dtype(self):
        return self.refs[0].dtype


def _flash_kernel(si_ref, st_ref, sf_ref, sl_ref, sp_ref, skc_ref, skn_ref, *refs, mode, tq, kt, rc, qpos_base,
                  kpos_base, window, blk, k_major, v_major):
    it = iter(refs)
    q_ref, kfirst_ref, k_ref, v_ref = next(it), next(it), next(it), next(it)
    if mode == 'fox':
        fq_ref, fk_ref = next(it), next(it)
    elif mode == 'blocks':
        sel_ref = next(it)
    elif mode == 'mask':
        mask_ref = next(it)
    o_ref = next(it)
    per_head = lambda: _PerHead([next(it) for _ in range(N_KV_HEADS)])
    m_ref, l_ref, acc_ref, s2_ref, p_ref, a_ref, bias_ref, qp_ref = (per_head() for _ in range(8))

    R = GROUP * tq
    step = pl.program_id(1)
    q0 = qpos_base + si_ref[step] * tq
    kpos0 = kpos_base + st_ref[step] * kt
    slot = sp_ref[step]
    s_ref = _PerHead([r.at[slot] for r in s2_ref.refs])
    s_next_ref = _PerHead([r.at[1 - slot] for r in s2_ref.refs])

    def qk(h, keys_ref):
        kb = keys_ref[...].astype(BF16)
        if k_major:
            return jnp.dot(qp_ref[h], kb, preferred_element_type=F32)
        return lax.dot_general(qp_ref[h], kb, (((1,), (1,)), ((), ())), preferred_element_type=F32)

    @pl.when(sf_ref[step] == 1)
    def _():
        for h in range(N_KV_HEADS):
            m_ref[h] = jnp.full((R, 1), NEG_INF, F32)
            l_ref[h] = jnp.zeros((R, LANES), F32)
            acc_ref[h] = jnp.zeros((R, KV_DIM), F32)
            qp_ref[h] = _spread_head(q_ref[h], h)
            s_ref[h] = qk(h, kfirst_ref)

    def tile_mask():
        kpos = kpos0 + lax.broadcasted_iota(jnp.int32, (1, kt), 1)
        qpos = _row_pos(tq, tq, q0)
        if mode == 'mask':
            return mask_ref[...].astype(F32) > 0.5
        mask = kpos <= qpos
        if window is not None:
            mask = mask & (qpos - kpos < window) & (kpos >= 0)
        return mask

    def softmax_head(h, use_bias):
        hb = h if mode == 'blocks' else 0
        for g in range(GROUP):
            if mode == 'fox':
                fk = jnp.broadcast_to(fk_ref[h * GROUP + g:h * GROUP + g + 1, :], (rc, kt))

            def logits(t0):
                u = s_ref[h, g * tq + t0:g * tq + t0 + rc, :]
                if mode == 'fox':
                    u = u - fk
                if use_bias:
                    u = u + bias_ref[hb, t0:t0 + rc, :]
                return u

            for t0 in range(0, tq, rc):
                rows = slice(g * tq + t0, g * tq + t0 + rc)
                row = jnp.max(logits(t0), axis=-1, keepdims=True)
                if mode == 'fox':
                    row = row + fq_ref[h, rows, :]
                m_prev = m_ref[h, rows, :]
                m_new = jnp.maximum(m_prev, row)
                m_ref[h, rows, :] = m_new
                a_ref[h, rows, :] = jnp.exp2(m_prev - m_new)
            for t0 in range(0, tq, rc):
                rows = slice(g * tq + t0, g * tq + t0 + rc)
                shift = -m_ref[h, rows, :]
                if mode == 'fox':
                    shift = shift + fq_ref[h, rows, :]
                p = jnp.exp2(logits(t0) + shift)
                part = p[:, 0:LANES]
                for u in range(1, kt // LANES):
                    part = part + p[:, u * LANES:(u + 1) * LANES]
                l_ref[h, rows, :] = a_ref[h, rows, :] * l_ref[h, rows, :] + part
                p_ref[h, rows, :] = p.astype(p_ref.dtype)

    def heads(use_bias):
        vb = v_ref[...].astype(BF16)
        for h in range(N_KV_HEADS):
            s_next_ref[h] = qk(h, k_ref)
        if use_bias:
            base = jnp.where(tile_mask(), 0.0, -jnp.inf)
            if mode != 'blocks':
                bias_ref[0] = base
        for h in range(N_KV_HEADS):
            if mode == 'blocks':
                nbp = sel_ref.shape[-1]
                kblk = lax.shift_right_logical(kpos0 + lax.broadcasted_iota(jnp.int32, (nbp, kt), 1),
                                               int(np.log2(blk)))
                expand = (kblk == lax.broadcasted_iota(jnp.int32, (nbp, kt), 0)).astype(F32)
                picked = jnp.dot(sel_ref[h], expand, preferred_element_type=F32) > 0.5
                bias_ref[h] = jnp.where(picked, base, -jnp.inf)
        for h in range(N_KV_HEADS):
            softmax_head(h, use_bias)
            pb = p_ref[h].astype(BF16)
            if v_major:
                pv = lax.dot_general(pb, vb, (((1,), (1,)), ((), ())), preferred_element_type=F32)
            else:
                pv = jnp.dot(pb, vb, preferred_element_type=F32)
            acc_ref[h] = a_ref[h] * acc_ref[h] + pv

    if mode in ('plain', 'fox') and window is None:
        crosses = kpos0 + kt - 1 > q0
        pl.when(crosses)(lambda: heads(True))
        pl.when(jnp.logical_not(crosses))(lambda: heads(False))
    else:
        heads(True)

    @pl.when(sl_ref[step] == 1)
    def _():
        out = jnp.zeros((R, KV_DIM), F32)
        for h in range(N_KV_HEADS):
            l = jnp.sum(l_ref[h], axis=-1, keepdims=True)
            out = jnp.where(_lane_head_mask(R, h), acc_ref[h] / jnp.where(l > 0.0, l, 1.0), out)
        o_ref[...] = out


def flash_attention(q, k, v, *, tq, kt, qpos_base, kpos_base=0, kv_len, mode='plain', window=None,
                    fq=None, fkT=None, sel=None, blk=None, mask=None, k_major=False, v_major=False):
    B, nQ, _, R, _ = q.shape
    last_tile = (kv_len - 1) // kt

    steps = []
    for i in range(nQ):
        lo = 0 if window is None else max(qpos_base + i * tq - (window - 1) - kpos_base, 0) // kt
        hi = min((qpos_base + (i + 1) * tq - 1 - kpos_base) // kt, last_tile)
        steps += [(i, t, int(t == lo), int(t == hi), (t - lo) % 2, lo, min(t + 1, hi)) for t in range(lo, hi + 1)]
    tables = [jnp.asarray(np.array(col, np.int32)) for col in zip(*steps)]
    SI, ST, SKC, SKN = 0, 1, 5, 6

    def q_map(b, s, *tab):
        return (b, tab[SI][s], 0, 0, 0)

    def kv_map(major, which):
        if major:
            return lambda b, s, *tab: (b, 0, tab[which][s])
        return lambda b, s, *tab: (b, tab[which][s], 0)

    def kv_spec(major, which):
        return pl.BlockSpec((None, KV_DIM, kt) if major else (None, kt, KV_DIM), kv_map(major, which))

    args = [q, k, k, v]
    specs = [pl.BlockSpec((None, None, N_KV_HEADS, R, HEAD_DIM), q_map), kv_spec(k_major, SKC),
             kv_spec(k_major, SKN), kv_spec(v_major, ST)]
    rc = min(tq, 16)
    scratch = []
    for shape, dtype in (((R, 1), F32), ((R, LANES), F32), ((R, KV_DIM), F32), ((2, R, kt), F32),
                         ((R, kt), BF16 if rc % 16 == 0 else F32), ((R, 1), F32), ((tq, kt), F32),
                         ((R, KV_DIM), BF16)):
        scratch += [pltpu.VMEM(shape, dtype)] * N_KV_HEADS
    if mode == 'fox':
        args += [fq, fkT]
        specs += [pl.BlockSpec((None, None, N_KV_HEADS, R, 1), q_map),
                  pl.BlockSpec((None, N_HEADS, kt), kv_map(True, ST))]
    elif mode == 'blocks':
        NBP = sel.shape[-1]
        args += [sel]
        specs += [pl.BlockSpec((None, N_KV_HEADS, tq, NBP), lambda b, s, *tab: (b, 0, tab[SI][s], 0))]
    elif mode == 'mask':
        args += [mask]
        specs += [pl.BlockSpec((None, tq, kt), lambda b, s, *tab: (b, tab[SI][s], tab[ST][s]))]
    return pl.pallas_call(
        functools.partial(_flash_kernel, mode=mode, tq=tq, kt=kt, rc=rc, qpos_base=qpos_base,
                          kpos_base=kpos_base, window=window, blk=blk, k_major=k_major, v_major=v_major),
        name='flash_' + mode + ('_win' if window else ''),
        grid_spec=pltpu.PrefetchScalarGridSpec(
            num_scalar_prefetch=len(tables), grid=(B, len(steps)), in_specs=specs,
            out_specs=pl.BlockSpec((None, None, R, KV_DIM), lambda b, s, *tab: (b, tab[SI][s], 0, 0)),
            scratch_shapes=scratch),
        out_shape=jax.ShapeDtypeStruct((B, nQ, R, KV_DIM), F32),
        compiler_params=_params(("arbitrary", "arbitrary")),
    )(*tables, *args)


def _moe_kernel(x_ref, g_ref, sc_ref, sh_ref, gate_ref, c_ref, w1_ref, w3_ref, w2_ref, o_ref, h_ref, acc_ref):
    e = pl.program_id(1)

    @pl.when(e == 0)
    def _():
        h_ref[...] = _adanorm(x_ref[...], g_ref[...], sc_ref[...], sh_ref[...]).astype(BF16)
        acc_ref[...] = jnp.zeros_like(acc_ref)

    h = h_ref[...]
    a = jnp.dot(h, w1_ref[...], preferred_element_type=F32)
    a = (a * jax.nn.sigmoid(a)) * jnp.dot(h, w3_ref[...], preferred_element_type=F32)
    y = jnp.dot(a.astype(BF16), w2_ref[...], preferred_element_type=F32)
    c = c_ref[...]
    lane = lax.broadcasted_iota(jnp.int32, c.shape, 1)
    ce = jnp.sum(jnp.where(lane == e, c, 0.0), axis=-1, keepdims=True)
    acc_ref[...] += jnp.where(ce != 0.0, ce * y, 0.0)

    @pl.when(e == pl.num_programs(1) - 1)
    def _():
        o_ref[...] = x_ref[...] + gate_ref[...] * acc_ref[...]


def moe_dense(x, coef, gain, sc, sh, gate, w1, w3, w2, *, seq_len, tm):
    N, D = x.shape
    E, _, DE = w1.shape
    tm = min(tm, N)
    row = lambda i, e: (i, 0)
    specs = [pl.BlockSpec((tm, D), row), pl.BlockSpec((1, D), lambda i, e: (0, 0))]
    args = [x, gain]
    for m in (sc, sh, gate):
        a, s = _mod_spec(m, tm, seq_len)
        args.append(a)
        imap = s.index_map
        specs.append(pl.BlockSpec(s.block_shape, lambda i, e, imap=imap: imap(i)))
    args += [coef, w1, w3, w2]
    specs += [pl.BlockSpec((tm, E), row),
              pl.BlockSpec((None, D, DE), lambda i, e: (e, 0, 0)),
              pl.BlockSpec((None, D, DE), lambda i, e: (e, 0, 0)),
              pl.BlockSpec((None, DE, D), lambda i, e: (e, 0, 0))]
    return pl.pallas_call(
        _moe_kernel, name='moe_dense', grid=(N // tm, E), in_specs=specs,
        out_specs=pl.BlockSpec((tm, D), row),
        out_shape=jax.ShapeDtypeStruct((N, D), F32),
        scratch_shapes=[pltpu.VMEM((tm, D), BF16), pltpu.VMEM((tm, D), F32)],
        compiler_params=_params(("arbitrary", "arbitrary")),
    )(*args)


def _head_norm(x, g):
    return x * lax.rsqrt(jnp.mean(x * x, axis=-1, keepdims=True) + EPS) * g


def _rope(x, pos):
    half = ROT_DIM // 2
    inv = ROPE_THETA ** (-jnp.arange(half, dtype=F32) / half)
    ang = pos.astype(F32)[:, None] * inv[None, :]
    cos, sin = jnp.cos(ang)[:, None, :], jnp.sin(ang)[:, None, :]
    x1, x2 = x[..., :half], x[..., half:ROT_DIM]
    return jnp.concatenate([x1 * cos - x2 * sin, x2 * cos + x1 * sin, x[..., ROT_DIM:]], axis=-1)


def _split(y, sizes):
    return jnp.split(y[..., :sum(sizes)], np.cumsum(sizes)[:-1].tolist(), axis=-1)


def _pad_cols(w, mult=256):
    return jnp.pad(w, ((0, 0), (0, _round_up(w.shape[1], mult) - w.shape[1])))


def _qrows(q, tq):
    B, T = q.shape[:2]
    nQ = T // tq
    q = (q * (ATTN_SCALE * LOG2E)).astype(BF16).reshape(B, nQ, tq, N_KV_HEADS, GROUP, HEAD_DIM)
    return q.transpose(0, 1, 3, 4, 2, 5).reshape(B, nQ, N_KV_HEADS, GROUP * tq, HEAD_DIM)


def _gate_rows(gates, tq):
    B, T, _, n = gates.shape
    g = gates.reshape(B, T // tq, tq, N_KV_HEADS, GROUP, n).transpose(0, 1, 4, 2, 3, 5)
    return g.reshape(B, T // tq, GROUP * tq, N_KV_HEADS * n)


def _unpad_out(o, tq):
    B, nQ = o.shape[:2]
    o = o.reshape(B, nQ, GROUP, tq, N_KV_HEADS, HEAD_DIM).transpose(0, 1, 3, 4, 2, 5)
    return o.reshape(B, nQ * tq, N_HEADS, HEAD_DIM)


def _rows_gt(a, tq):
    B, T, _ = a.shape
    a = a.reshape(B, T // tq, tq, N_KV_HEADS, GROUP).transpose(0, 1, 3, 4, 2)
    return a.reshape(B, T // tq, N_KV_HEADS, GROUP * tq, 1)


def _pad_len(a, L):
    return jnp.pad(a, ((0, 0), (0, L - a.shape[1])) + ((0, 0),) * (a.ndim - 2))


def _modulation(c, ada_w, ada_b):
    m = jax.nn.silu(c) @ ada_w + ada_b
    return [a[:, None, :] for a in jnp.split(m, 6, axis=-1)]


class _Stream:
    def __init__(self, B, T, past):
        self.B, self.T, self.past = B, T, past
        self.tq = min(T, 256)
        self.tm = min(B * T, 512)
        self.pos = past + jnp.arange(T)


def _kv_tile(L):
    return max(d for d in (512, 256, 128) if L % d == 0)


def kernel(x_prompt, x_sample, cache_fox_k, cache_fox_v, cache_fox_logf, cache_moba_k, cache_moba_v,
           cache_nsa_cmp_k, cache_nsa_cmp_v, cache_nsa_slc_k, cache_nsa_slc_v, state_nsa_win_k, state_nsa_win_v,
           cache_dsa_k, cache_dsa_v, cache_dsa_idx_k, page_table, c_prompt, c_sample,
           ada_w, ada_b, norm_mix, norm_ffn,
           fox_w_in, fox_b_f, fox_q_norm, fox_k_norm, fox_w_o,
           moba_w_in, moba_q_norm, moba_k_norm, moba_w_o,
           nsa_w_in, nsa_q_norm, nsa_kc_norm, nsa_ks_norm, nsa_kw_norm, nsa_pe_k, nsa_w1_k, nsa_w2_k,
           nsa_pe_v, nsa_w1_v, nsa_w2_v, nsa_w_o,
           dsa_w_in, dsa_q_norm, dsa_k_norm, dsa_w_o,
           router_w, router_b, moe_w1, moe_w3, moe_w2):
    D = x_prompt.shape[-1]
    page = cache_fox_k.shape[1]
    past = page_table.shape[1] * page
    streams = (_Stream(x_prompt.shape[0], x_prompt.shape[1], 0),
               _Stream(x_sample.shape[0], x_sample.shape[1], past))
    conds = (c_prompt, c_sample)
    xs = [x_prompt.reshape(-1, D), x_sample.reshape(-1, D)]
    outs = [dict(), dict()]

    w_in = {'fox': fox_w_in, 'moba': moba_w_in, 'nsa': nsa_w_in, 'dsa': dsa_w_in}
    w_out = {'fox': fox_w_o, 'moba': moba_w_o, 'nsa': nsa_w_o, 'dsa': dsa_w_o}
    w_in = {n: _pad_cols(w).astype(BF16) for n, w in w_in.items()}
    w_out = {n: w.astype(BF16) for n, w in w_out.items()}
    moe_w1b, moe_w3b, moe_w2b = moe_w1.astype(BF16), moe_w3.astype(BF16), moe_w2.astype(BF16)

    def keys_of(st, new, cache, dtype=F32):
        B, T = new.shape[:2]
        if st.past == 0:
            return new.reshape(B, T, -1).astype(dtype), T, False
        return gather_rows_t(cache, page_table, new, dtype), st.past + T, True

    def attend(st, q, K, V, L, major, **kw):
        Lk = K.shape[2] if major else K.shape[1]
        kt = _kv_tile(Lk)
        if st.tq < 16 and Lk % (2 * kt) == 0:
            kt *= 2
        return flash_attention(q, K, V, tq=st.tq, kt=kt, qpos_base=st.past, kv_len=L,
                               k_major=major, v_major=major, **kw)

    def channels_major(a, major):
        return a if major else a.transpose(0, 2, 1)

    def fox(st, y, o):
        B, T = st.B, st.T
        q, k, v, fl = _split(y, [N_HEADS * HEAD_DIM, KV_DIM, KV_DIM, N_HEADS])
        q = _head_norm(q.reshape(B, T, N_HEADS, HEAD_DIM), fox_q_norm)
        k = _head_norm(k.reshape(B, T, N_KV_HEADS, HEAD_DIM), fox_k_norm)
        v = v.reshape(B, T, N_KV_HEADS, HEAD_DIM)
        logf = jax.nn.log_sigmoid(fl.reshape(B, T, N_HEADS) + fox_b_f)
        o.update(fox_k=k, fox_v=v, fox_logf=logf)
        K, L, major = keys_of(st, k, cache_fox_k, BF16)
        V, _, _ = keys_of(st, v, cache_fox_v, BF16)
        LF, _, lf_major = keys_of(st, logf, cache_fox_logf)
        LF = channels_major(LF, lf_major)
        F = cumsum_lanes(LF.reshape(B * N_HEADS, -1)).reshape(LF.shape) * LOG2E
        fq = _rows_gt(lax.slice_in_dim(F, st.past, st.past + T, axis=2).transpose(0, 2, 1), st.tq)
        return [attend(st, _qrows(q, st.tq), K, V, L, major, mode='fox', fq=fq, fkT=F)], None

    def moba(st, y, o):
        B, T = st.B, st.T
        q, k, v = _split(y, [N_HEADS * HEAD_DIM, KV_DIM, KV_DIM])
        q = _rope(_head_norm(q.reshape(B, T, N_HEADS, HEAD_DIM), moba_q_norm), st.pos)
        k = _rope(_head_norm(k.reshape(B, T, N_KV_HEADS, HEAD_DIM), moba_k_norm), st.pos)
        v = v.reshape(B, T, N_KV_HEADS, HEAD_DIM)
        o.update(moba_k=k, moba_v=v)
        K, L, major = keys_of(st, k, cache_moba_k)
        V, _, _ = keys_of(st, v, cache_moba_v, BF16)
        Kt = channels_major(K, major)
        NBP = _round_up(Kt.shape[2] // MOBA_BLOCK, LANES)
        kbar = block_means_t(Kt, MOBA_BLOCK, NBP)
        kbt = jnp.tile(kbar.reshape(B, N_KV_HEADS, 1, HEAD_DIM, NBP), (1, 1, GROUP, 1, 1))
        kbt = kbt.reshape(B, N_KV_HEADS, KV_DIM, NBP)
        sel = moba_select(q.reshape(B, T, -1), kbt, tq=st.tq, qpos_base=st.past)
        return [attend(st, _qrows(q, st.tq), K, V, L, major, mode='blocks', sel=sel, blk=MOBA_BLOCK)], None

    def compress(st, raw_new, cache, pe, w1, w2):
        B, T = st.B, st.T
        if st.past == 0:
            lo, hi = compress_lohi(raw_new.reshape(B * T, KV_DIM), pe, w1)
            nblk = T // CMP_STRIDE
            lo, hi = lo.reshape(B, nblk, KV_DIM), hi.reshape(B, nblk, KV_DIM)
            L = T
        else:
            P = cache.shape[0]
            lo_p, hi_p = compress_lohi(pool_t(cache), pe, w1, pool_major=True)
            tail = jnp.pad(raw_new.reshape(B, T, KV_DIM), ((0, 0), (0, page - T), (0, 0)))
            lo_t, hi_t = compress_lohi(tail.reshape(B * page, KV_DIM), pe, w1)
            bpp = page // CMP_STRIDE
            lo = gather_pages(lo_p.reshape(P, bpp, KV_DIM), page_table, lo_t.reshape(B, bpp, KV_DIM))
            hi = gather_pages(hi_p.reshape(P, bpp, KV_DIM), page_table, hi_t.reshape(B, bpp, KV_DIM))
            L = st.past + T
        nc = _round_up(L, SLC_BLOCK) // CMP_STRIDE - 1
        NCP = _round_up(nc, LANES)
        lo = _pad_len(lo, NCP + 1)
        hi = _pad_len(hi, NCP + 1)
        out = compress_out(lo[:, :NCP].reshape(B * NCP, KV_DIM), hi[:, 1:NCP + 1].reshape(B * NCP, KV_DIM), w2)
        return out.reshape(B, NCP, KV_DIM), nc

    def nsa(st, y, o, win_k, win_v):
        B, T = st.B, st.T
        q, kc, vc, ks, vs, kw, vw, gl = _split(y, [N_HEADS * HEAD_DIM] + [KV_DIM] * 6 + [3 * N_HEADS])
        heads = lambda a: a.reshape(B, T, N_KV_HEADS, HEAD_DIM)
        q = _rope(_head_norm(q.reshape(B, T, N_HEADS, HEAD_DIM), nsa_q_norm), st.pos)
        ks = _rope(_head_norm(heads(ks), nsa_ks_norm), st.pos)
        kw = _rope(_head_norm(heads(kw), nsa_kw_norm), st.pos)
        kc, vc, vs, vw = heads(kc), heads(vc), heads(vs), heads(vw)
        gates = jax.nn.sigmoid(gl).reshape(B, T, N_HEADS, 3)
        o.update(nsa_cmp_k=kc, nsa_cmp_v=vc, nsa_slc_k=ks, nsa_slc_v=vs)
        qp = _qrows(q, st.tq)
        kcc, nc = compress(st, kc, cache_nsa_cmp_k, nsa_pe_k, nsa_w1_k, nsa_w2_k)
        vcc, _ = compress(st, vc, cache_nsa_cmp_v, nsa_pe_v, nsa_w1_v, nsa_w2_v)
        NCP = kcc.shape[1]
        cend = jnp.arange(NCP) * CMP_STRIDE + CMP_LEN - 1
        kcc = _rope(_head_norm(kcc.reshape(B, NCP, N_KV_HEADS, HEAD_DIM), nsa_kc_norm), cend)
        KS, L, major = keys_of(st, ks, cache_nsa_slc_k, BF16)
        VS, _, _ = keys_of(st, vs, cache_nsa_slc_v, BF16)
        nbs = _round_up(L, SLC_BLOCK) // SLC_BLOCK
        NBP = _round_up(nbs, LANES)
        ci = np.arange(NCP)[:, None]
        sj = np.arange(NBP)[None, :]
        cover = ((ci * CMP_STRIDE < (sj + 1) * SLC_BLOCK) & (ci * CMP_STRIDE + CMP_LEN > sj * SLC_BLOCK)
                 & (ci < nc) & (sj < nbs))
        oc, sel = nsa_compressed(qp, kcc.reshape(B, NCP, KV_DIM).astype(BF16), vcc.astype(BF16),
                                 jnp.asarray(cover, BF16), tq=st.tq, qpos_base=st.past, nc=nc)
        osl = attend(st, qp, KS, VS, L, major, mode='blocks', sel=sel, blk=SLC_BLOCK)
        if st.past == 0:
            KW, VW, kbase = kw.reshape(B, T, KV_DIM), vw.reshape(B, T, KV_DIM), 0
            wb = min(WINDOW, T)
            o.update(nsa_win_k=kw[:, T - wb:], nsa_win_v=vw[:, T - wb:])
        else:
            KWf = jnp.concatenate([win_k, kw], axis=1)
            VWf = jnp.concatenate([win_v, vw], axis=1)
            lw = KWf.shape[1]
            wb = min(WINDOW, lw)
            o.update(nsa_win_k=KWf[:, lw - wb:], nsa_win_v=VWf[:, lw - wb:])
            kbase = st.past - win_k.shape[1]
            Lp = _round_up(lw, LANES)
            KW = _pad_len(KWf.reshape(B, lw, KV_DIM), Lp)
            VW = _pad_len(VWf.reshape(B, lw, KV_DIM), Lp)
        ow = flash_attention(qp, KW, VW, tq=st.tq, kt=_kv_tile(KW.shape[1]), qpos_base=st.past,
                             kpos_base=kbase, kv_len=KW.shape[1], mode='plain', window=WINDOW)
        return [oc, osl, ow], gates

    def dsa(st, y, o):
        B, T, tq = st.B, st.T, st.tq
        q, k, v, qi, ki, wi = _split(y, [N_HEADS * HEAD_DIM, KV_DIM, KV_DIM, IDX_HEADS * IDX_DIM, IDX_DIM, IDX_HEADS])
        q = _rope(_head_norm(q.reshape(B, T, N_HEADS, HEAD_DIM), dsa_q_norm), st.pos)
        k = _rope(_head_norm(k.reshape(B, T, N_KV_HEADS, HEAD_DIM), dsa_k_norm), st.pos)
        v = v.reshape(B, T, N_KV_HEADS, HEAD_DIM)
        qi = _rope(qi.reshape(B, T, IDX_HEADS, IDX_DIM), st.pos)
        ki = _rope(ki.reshape(B, T, 1, IDX_DIM), st.pos)[:, :, 0]
        wi = wi * IDX_HEADS ** -0.5
        o.update(dsa_k=k, dsa_v=v, dsa_idx=ki)
        K, L, major = keys_of(st, k, cache_dsa_k, BF16)
        V, _, _ = keys_of(st, v, cache_dsa_v, BF16)
        KI, _, ki_major = keys_of(st, ki, cache_dsa_idx_k)
        KI = channels_major(KI, ki_major)
        kt = _kv_tile(KI.shape[2])
        nQ = T // tq
        qi_r = qi.astype(BF16).reshape(B, nQ, tq, IDX_HEADS, IDX_DIM).transpose(0, 1, 3, 2, 4)
        qi_r = qi_r.reshape(B, nQ, IDX_HEADS * tq, IDX_DIM)
        wi_r = wi.reshape(B, nQ, tq, IDX_HEADS)
        n_sel = min(DSA_TOPK, L // 4)
        picked = dsa_select(qi_r, wi_r, KI, tq=tq, kt=kt, qpos_base=st.past, n_sel=n_sel)
        return [attend(st, _qrows(q, tq), K, V, L, major, mode='mask', mask=picked)], None

    for layer in range(ada_w.shape[0]):
        kind = ('fox', 'moba', 'nsa', 'dsa')[layer % 4]
        for s, st in enumerate(streams):
            x = xs[s]
            sh1, sc1, g1, sh2, sc2, g2 = _modulation(conds[s], ada_w[layer], ada_b[layer])
            y = fused_matmul(x, w_in[kind], seq_len=st.T, tm=st.tm,
                             norm=(norm_mix[layer][None, :], sc1, sh1)).reshape(st.B, st.T, -1)
            if kind == 'fox':
                branches, gates = fox(st, y, outs[s])
            elif kind == 'moba':
                branches, gates = moba(st, y, outs[s])
            elif kind == 'nsa':
                branches, gates = nsa(st, y, outs[s], state_nsa_win_k, state_nsa_win_v)
            else:
                branches, gates = dsa(st, y, outs[s])
            if st.tq % 16 == 0:
                gts = None if gates is None else _gate_rows(gates, st.tq)
                x = proj_heads(branches, w_out[kind], x, g1, tq=st.tq, gts=gts)
            else:
                att = [_unpad_out(a, st.tq) for a in branches]
                if gates is not None:
                    att = [sum(gates[..., c:c + 1] * a for c, a in enumerate(att))]
                x = fused_matmul(att[0].reshape(-1, N_HEADS * HEAD_DIM), w_out[kind], seq_len=st.T, tm=st.tm,
                                 epi=(x, g1))
            ffn_norm = (norm_ffn[layer][None, :], sc2, sh2)
            coef = route(x, router_w, router_b, ffn_norm, seq_len=st.T, tm=st.tm)
            xs[s] = moe_dense(x, coef, *ffn_norm, g2, moe_w1b[layer], moe_w3b[layer], moe_w2b[layer],
                              seq_len=st.T, tm=2 * st.tm)

    names = ('fox_k', 'fox_v', 'fox_logf', 'moba_k', 'moba_v', 'nsa_cmp_k', 'nsa_cmp_v', 'nsa_slc_k', 'nsa_slc_v',
             'nsa_win_k', 'nsa_win_v', 'dsa_k', 'dsa_v', 'dsa_idx')
    res = [xs[0].reshape(x_prompt.shape), xs[1].reshape(x_sample.shape)]
    for n in names:
        res += [outs[0][n], outs[1][n]]
    return tuple(res)
```

```python
import functools

import numpy as np
import jax
import jax.numpy as jnp
from jax import lax
from jax.experimental import pallas as pl
from jax.experimental.pallas import tpu as pltpu

HEAD_DIM = 64
N_KV_HEADS = 4
GROUP = 4
N_HEADS = N_KV_HEADS * GROUP
KV_DIM = N_KV_HEADS * HEAD_DIM
ROT_DIM = HEAD_DIM // 4
ROPE_THETA = 500000.0
ATTN_SCALE = HEAD_DIM ** -0.5
LOG2E = 1.4426950408889634
MOBA_BLOCK = 256
MOBA_TOPK = 3
CMP_LEN = 32
CMP_STRIDE = 16
CMP_HIDDEN = 64
SLC_BLOCK = 64
SLC_TOPK = 16
WINDOW = 512
IDX_HEADS = 8
IDX_DIM = 64
DSA_TOPK = 256
N_EXPERTS = 16
N_GROUPS = 4
EXPERTS_PER_GROUP = N_EXPERTS // N_GROUPS
TOP_K = 2
EPS = 1e-6
NEG_INF = -1e30

LANES = 128
VMEM_LIMIT = 56 * 1024 * 1024
BF16 = jnp.bfloat16
F32 = jnp.float32
INT_MIN = -2 ** 31


def _round_up(a, m):
    return (a + m - 1) // m * m


def _params(sem):
    return pltpu.CompilerParams(dimension_semantics=sem, vmem_limit_bytes=VMEM_LIMIT)


def _gather_kernel(pt_ref, *refs, pps, n_main):
    page_refs, tail_ref, o_ref = refs[:pps], refs[pps], refs[pps + 1]
    j = pl.program_id(1)

    @pl.when(j < n_main)
    def _():
        for p in range(pps):
            o_ref[p] = page_refs[p][...]

    @pl.when(j == n_main)
    def _():
        o_ref[0] = tail_ref[...]
        for p in range(1, pps):
            o_ref[p] = jnp.zeros(o_ref.shape[1:], o_ref.dtype)


def gather_pages(pool, page_table, tail):
    _, R, C = pool.shape
    B, n_pages = page_table.shape
    pps = max(d for d in (8, 4, 2, 1) if n_pages % d == 0)
    n_main = n_pages // pps

    def page_map(b, j, pt, p):
        return (pt[b, jnp.minimum(j * pps + p, n_pages - 1)], 0, 0)

    in_specs = [pl.BlockSpec((None, R, C), functools.partial(page_map, p=p)) for p in range(pps)]
    in_specs.append(pl.BlockSpec((None, R, C), lambda b, j, pt: (b, 0, 0)))
    out = pl.pallas_call(
        functools.partial(_gather_kernel, pps=pps, n_main=n_main), name='gather_pages',
        grid_spec=pltpu.PrefetchScalarGridSpec(
            num_scalar_prefetch=1, grid=(B, n_main + 1), in_specs=in_specs,
            out_specs=pl.BlockSpec((None, pps, R, C), lambda b, j, pt: (b, j, 0, 0))),
        out_shape=jax.ShapeDtypeStruct((B, (n_main + 1) * pps, R, C), pool.dtype),
        compiler_params=_params(("arbitrary", "arbitrary")),
    )(page_table, *([pool] * pps), tail)
    return out.reshape(B, (n_main + 1) * pps * R, C)


def _gather_t_kernel(pt_ref, *refs, pps, n_main, page):
    page_refs, tail_ref, o_ref = refs[:pps], refs[pps], refs[pps + 1]
    j = pl.program_id(1)

    @pl.when(j < n_main)
    def _():
        for p in range(pps):
            o_ref[:, p * page:(p + 1) * page] = page_refs[p][...].astype(o_ref.dtype)

    @pl.when(j == n_main)
    def _():
        o_ref[:, :page] = tail_ref[...].astype(o_ref.dtype)
        o_ref[:, page:] = jnp.zeros((o_ref.shape[0], (pps - 1) * page), o_ref.dtype)


def pool_t(pool):
    P, page = pool.shape[:2]
    return jnp.moveaxis(pool.reshape(P, page, -1), 1, 2)


def gather_rows_t(pool, page_table, new_rows, dtype=None):
    dtype = pool.dtype if dtype is None else dtype
    P, page = pool.shape[:2]
    B, n_pages = page_table.shape
    T = new_rows.shape[1]
    pt = pool_t(pool)
    C = pt.shape[1]
    tail = jnp.pad(new_rows.reshape(B, T, C), ((0, 0), (0, page - T), (0, 0))).transpose(0, 2, 1)
    pps = max(d for d in (8, 4, 2, 1) if n_pages % d == 0)
    n_main = n_pages // pps

    def page_map(b, j, tab, p):
        return (tab[b, jnp.minimum(j * pps + p, n_pages - 1)], 0, 0)

    in_specs = [pl.BlockSpec((None, C, page), functools.partial(page_map, p=p)) for p in range(pps)]
    in_specs.append(pl.BlockSpec((None, C, page), lambda b, j, tab: (b, 0, 0)))
    return pl.pallas_call(
        functools.partial(_gather_t_kernel, pps=pps, n_main=n_main, page=page), name='gather_pages_t',
        grid_spec=pltpu.PrefetchScalarGridSpec(
            num_scalar_prefetch=1, grid=(B, n_main + 1), in_specs=in_specs,
            out_specs=pl.BlockSpec((None, C, pps * page), lambda b, j, tab: (b, 0, j))),
        out_shape=jax.ShapeDtypeStruct((B, C, (n_main + 1) * pps * page), dtype),
        compiler_params=_params(("arbitrary", "arbitrary")),
    )(page_table, *([pt] * pps), tail)


def _adanorm(x, g, sc, sh):
    y = x * lax.rsqrt(jnp.mean(x * x, axis=-1, keepdims=True) + EPS)
    return (y * g) * (1.0 + sc) + sh


def _matmul_kernel(*refs, has_norm, has_epi):
    it = iter(refs)
    x_ref, w_ref = next(it), next(it)
    x = x_ref[...]
    if has_norm:
        g_ref, sc_ref, sh_ref = next(it), next(it), next(it)
        x = _adanorm(x, g_ref[...], sc_ref[...], sh_ref[...])
    acc = jnp.dot(x.astype(BF16), w_ref[...], preferred_element_type=F32)
    if has_epi:
        res_ref, gate_ref = next(it), next(it)
        acc = res_ref[...] + gate_ref[...] * acc
    o_ref = next(it)
    o_ref[...] = acc


def _mod_spec(mod, tm, seq_len):
    D = mod.shape[-1]
    if seq_len % tm == 0:
        tps = seq_len // tm
        return mod, pl.BlockSpec((None, 1, D), lambda i: (i // tps, 0, 0))
    rows = jnp.broadcast_to(mod, (mod.shape[0], seq_len, D)).reshape(-1, D)
    return rows, pl.BlockSpec((tm, D), lambda i: (i, 0))


def fused_matmul(x, w, *, seq_len, tm, norm=None, epi=None):
    N, K = x.shape
    M = w.shape[1]
    tm = min(tm, N)
    args = [x, w]
    specs = [pl.BlockSpec((tm, K), lambda i: (i, 0)), pl.BlockSpec((K, M), lambda i: (0, 0))]
    if norm is not None:
        gain, sc, sh = norm
        args.append(gain)
        specs.append(pl.BlockSpec((1, K), lambda i: (0, 0)))
        for m in (sc, sh):
            a, s = _mod_spec(m, tm, seq_len)
            args.append(a)
            specs.append(s)
    if epi is not None:
        res, gate = epi
        args.append(res)
        specs.append(pl.BlockSpec((tm, M), lambda i: (i, 0)))
        a, s = _mod_spec(gate, tm, seq_len)
        args.append(a)
        specs.append(s)
    return pl.pallas_call(
        functools.partial(_matmul_kernel, has_norm=norm is not None, has_epi=epi is not None),
        name='proj_in' if norm is not None else 'proj_out',
        grid=(N // tm,), in_specs=specs,
        out_specs=pl.BlockSpec((tm, M), lambda i: (i, 0)),
        out_shape=jax.ShapeDtypeStruct((N, M), F32),
        compiler_params=_params(("arbitrary",)),
    )(*args)


def _proj_heads_kernel(*refs, n_branch, tq):
    it = iter(refs)
    o_refs = [next(it) for _ in range(n_branch)]
    gts_ref = next(it) if n_branch > 1 else None
    w_ref, res_ref, gate_ref, out_ref = next(it), next(it), next(it), next(it)
    R = GROUP * tq
    if n_branch == 1:
        o = o_refs[0][...]
    else:
        gts = gts_ref[...]
        o = None
        for c in range(n_branch):
            gfull = jnp.zeros((R, KV_DIM), F32)
            for h in range(N_KV_HEADS):
                col = h * n_branch + c
                gfull = jnp.where(_lane_head_mask(R, h), gts[:, col:col + 1], gfull)
            o = gfull * o_refs[c][...] if o is None else o + gfull * o_refs[c][...]
    ob = o.astype(BF16)
    acc = jnp.zeros(out_ref.shape, F32)
    for g in range(GROUP):
        acc += jnp.dot(ob[g * tq:(g + 1) * tq], w_ref[g], preferred_element_type=F32)
    out_ref[...] = res_ref[...] + gate_ref[...] * acc


def proj_heads(branches, w_o, res, gate, *, tq, gts=None):
    B, nQ, R, _ = branches[0].shape
    D = w_o.shape[1]
    nb = len(branches)
    w = w_o.reshape(N_KV_HEADS, GROUP, HEAD_DIM, D).transpose(1, 0, 2, 3).reshape(GROUP, KV_DIM, D)
    tile = lambda b, i: (b, i, 0, 0)
    args = list(branches)
    specs = [pl.BlockSpec((None, None, R, KV_DIM), tile)] * nb
    if nb > 1:
        args.append(gts)
        specs.append(pl.BlockSpec((None, None, R, gts.shape[-1]), tile))
    args += [w, res.reshape(B, nQ, tq, D), gate]
    specs += [pl.BlockSpec((GROUP, KV_DIM, D), lambda b, i: (0, 0, 0)),
              pl.BlockSpec((None, None, tq, D), tile), pl.BlockSpec((None, 1, D), lambda b, i: (b, 0, 0))]
    out = pl.pallas_call(
        functools.partial(_proj_heads_kernel, n_branch=nb, tq=tq), name='proj_heads',
        grid=(B, nQ), in_specs=specs,
        out_specs=pl.BlockSpec((None, None, tq, D), tile),
        out_shape=jax.ShapeDtypeStruct((B, nQ, tq, D), F32),
        compiler_params=_params(("arbitrary", "arbitrary")),
    )(*args)
    return out.reshape(B * nQ * tq, D)


ROUTE_ROWS = 8


def _first_match(vals, target):
    out, seen = [], None
    for v in vals:
        hit = v == target
        out.append(hit if seen is None else hit & jnp.logical_not(seen))
        seen = hit if seen is None else seen | hit
    return out


def _router_kernel(x_ref, g_ref, sc_ref, sh_ref, wt_ref, b_ref, o_ref):
    h = _adanorm(x_ref[...], g_ref[...], sc_ref[...], sh_ref[...]).astype(BF16)
    logit = lax.dot_general(wt_ref[...], h, (((1,), (1,)), ((), ())), preferred_element_type=F32)
    aff = jax.nn.sigmoid(logit)
    grp = aff + b_ref[...]
    n = EXPERTS_PER_GROUP
    a = [grp[k * ROUTE_ROWS:(k + 1) * ROUTE_ROWS] for k in range(n)]
    f = [aff[k * ROUTE_ROWS:(k + 1) * ROUTE_ROWS] for k in range(n)]
    top1 = functools.reduce(jnp.maximum, a)
    s1 = _first_match(a, top1)
    rest = [jnp.where(s, -jnp.inf, v) for s, v in zip(s1, a)]
    top2 = functools.reduce(jnp.maximum, rest)
    s2 = _first_match(rest, top2)
    row = lax.broadcasted_iota(jnp.int32, top1.shape, 0).astype(F32)
    score = jnp.where(row < N_GROUPS, top1 + top2, -jnp.inf)
    best = jnp.max(score, axis=0, keepdims=True)
    gsel = row == jnp.min(jnp.where(score == best, row, 1e9), axis=0, keepdims=True)
    w1 = sum(jnp.where(s, v, 0.0) for s, v in zip(s1, f))
    w2 = sum(jnp.where(s, v, 0.0) for s, v in zip(s2, f))
    den = w1 + w2
    for k in range(n):
        o_ref[k * ROUTE_ROWS:(k + 1) * ROUTE_ROWS, :] = jnp.where(
            gsel & s1[k], w1 / den, jnp.where(gsel & s2[k], w2 / den, 0.0))


def route(x, router_w, router_b, norm, *, seq_len, tm):
    assert TOP_K == 2
    N, D = x.shape
    tm = min(tm, N)
    wt = router_w.T.reshape(N_GROUPS, EXPERTS_PER_GROUP, D).transpose(1, 0, 2)
    wt = jnp.pad(wt, ((0, 0), (0, ROUTE_ROWS - N_GROUPS), (0, 0))).reshape(-1, D).astype(BF16)
    bt = router_b.reshape(N_GROUPS, EXPERTS_PER_GROUP).T
    bt = jnp.pad(bt, ((0, 0), (0, ROUTE_ROWS - N_GROUPS))).reshape(-1, 1)
    rows = wt.shape[0]
    gain, sc, sh = norm
    args, specs = [x, gain], [pl.BlockSpec((tm, D), lambda i: (i, 0)), pl.BlockSpec((1, D), lambda i: (0, 0))]
    for m in (sc, sh):
        a, s = _mod_spec(m, tm, seq_len)
        args.append(a)
        specs.append(s)
    args += [wt, bt]
    specs += [pl.BlockSpec((rows, D), lambda i: (0, 0)), pl.BlockSpec((rows, 1), lambda i: (0, 0))]
    coef_t = pl.pallas_call(
        _router_kernel, name='router', grid=(N // tm,), in_specs=specs,
        out_specs=pl.BlockSpec((rows, tm), lambda i: (0, i)),
        out_shape=jax.ShapeDtypeStruct((rows, N), F32),
        compiler_params=_params(("arbitrary",)),
    )(*args)
    coef_t = coef_t.reshape(EXPERTS_PER_GROUP, ROUTE_ROWS, N)[:, :N_GROUPS]
    return coef_t.transpose(2, 1, 0).reshape(N, N_EXPERTS)


def _cumsum_kernel(x_ref, o_ref, carry_ref, *, tc):
    @pl.when(pl.program_id(0) == 0)
    def _():
        carry_ref[...] = jnp.zeros_like(carry_ref)

    r = lax.broadcasted_iota(jnp.int32, (tc, tc), 0)
    c = lax.broadcasted_iota(jnp.int32, (tc, tc), 1)
    out = jnp.dot(x_ref[...], (r <= c).astype(F32), preferred_element_type=F32,
                  precision=lax.Precision.HIGHEST) + carry_ref[...]
    o_ref[...] = out
    carry_ref[...] = out[:, tc - 1:tc]


def cumsum_lanes(x, tc=256):
    rows, L = x.shape
    tc = min(tc, L)
    return pl.pallas_call(
        functools.partial(_cumsum_kernel, tc=tc), name='cumsum',
        grid=(L // tc,),
        in_specs=[pl.BlockSpec((rows, tc), lambda j: (0, j))],
        out_specs=pl.BlockSpec((rows, tc), lambda j: (0, j)),
        out_shape=jax.ShapeDtypeStruct((rows, L), F32),
        scratch_shapes=[pltpu.VMEM((rows, 1), F32)],
        compiler_params=_params(("arbitrary",)),
    )(x)


def _select_blocks(score, own, n_top):
    col = lax.broadcasted_iota(jnp.int32, score.shape, 1)
    colf = col.astype(F32)
    sc = jnp.where(col < own, score, -jnp.inf)
    sel = col == own
    for _ in range(n_top):
        m = jnp.max(sc, axis=-1, keepdims=True)
        idx = jnp.min(jnp.where(sc == m, colf, 1e9), axis=-1, keepdims=True)
        hit = colf == idx
        sel = sel | (hit & (m > -jnp.inf))
        sc = jnp.where(hit, -jnp.inf, sc)
    return sel


def _row_pos(tq, rows, base):
    assert tq & (tq - 1) == 0
    r = lax.broadcasted_iota(jnp.int32, (rows, 1), 0)
    return base + (r & (tq - 1))


def _moba_select_kernel(q_ref, kb_ref, o_ref, *, tq, qpos_base):
    qpos = _row_pos(tq, tq, qpos_base + pl.program_id(1) * tq)
    own = lax.shift_right_logical(qpos, int(np.log2(MOBA_BLOCK)))
    for h in range(N_KV_HEADS):
        gate = jnp.dot(q_ref[:, h * KV_DIM:(h + 1) * KV_DIM], kb_ref[h], preferred_element_type=F32)
        o_ref[h] = _select_blocks(gate, own, MOBA_TOPK).astype(F32)


def moba_select(q_nat, kbt, *, tq, qpos_base):
    B, T, D = q_nat.shape
    NBP = kbt.shape[-1]
    return pl.pallas_call(
        functools.partial(_moba_select_kernel, tq=tq, qpos_base=qpos_base), name='moba_select',
        grid=(B, T // tq),
        in_specs=[pl.BlockSpec((None, tq, D), lambda b, i: (b, i, 0)),
                  pl.BlockSpec((None, N_KV_HEADS, KV_DIM, NBP), lambda b, i: (b, 0, 0, 0))],
        out_specs=pl.BlockSpec((None, N_KV_HEADS, tq, NBP), lambda b, i: (b, 0, i, 0)),
        out_shape=jax.ShapeDtypeStruct((B, N_KV_HEADS, T, NBP), F32),
        compiler_params=_params(("arbitrary", "arbitrary")),
    )(q_nat, kbt)


def _block_mean_kernel(k_ref, o_ref, *, tl, blk):
    j = pl.program_id(1)

    @pl.when(j == 0)
    def _():
        o_ref[...] = jnp.zeros_like(o_ref)

    nbp = o_ref.shape[-1]
    key_blk = lax.shift_right_logical(j * tl + lax.broadcasted_iota(jnp.int32, (tl, nbp), 0), int(np.log2(blk)))
    avg = jnp.where(key_blk == lax.broadcasted_iota(jnp.int32, (tl, nbp), 1), 1.0 / blk, 0.0)
    o_ref[...] += jnp.dot(k_ref[...], avg, preferred_element_type=F32, precision=lax.Precision.HIGHEST)


def block_means_t(kt_, blk, nbp):
    B, C, L = kt_.shape
    tl = max(d for d in (1024, 512, 256, 128) if L % d == 0)
    return pl.pallas_call(
        functools.partial(_block_mean_kernel, tl=tl, blk=blk), name='block_means',
        grid=(B, L // tl),
        in_specs=[pl.BlockSpec((None, C, tl), lambda b, j: (b, 0, j))],
        out_specs=pl.BlockSpec((None, C, nbp), lambda b, j: (b, 0, 0)),
        out_shape=jax.ShapeDtypeStruct((B, C, nbp), F32),
        compiler_params=_params(("arbitrary", "arbitrary")),
    )(kt_)


def _compress_lohi_kernel(x_ref, pe_lo_ref, pe_hi_ref, wlo_ref, whi_ref, lo_ref, hi_ref, *scratch,
                          nblk, pool_major):
    if pool_major:
        xa_ref, xb_ref = scratch
        page = x_ref.shape[-1]
        for p in range(x_ref.shape[0]):
            xt = x_ref[p].T
            xa_ref[p * page:(p + 1) * page, :] = xt[:, :LANES]
            xb_ref[p * page:(p + 1) * page, :] = xt[:, LANES:]
    lo = jnp.zeros((nblk, KV_DIM), F32)
    hi = jnp.zeros((nblk, KV_DIM), F32)
    for t in range(CMP_STRIDE):
        if pool_major:
            x = jnp.concatenate([xa_ref[pl.ds(t, nblk, stride=CMP_STRIDE), :],
                                 xb_ref[pl.ds(t, nblk, stride=CMP_STRIDE), :]], axis=-1)
        else:
            x = jnp.concatenate([x_ref[pl.ds(2 * t, nblk, stride=2 * CMP_STRIDE), :],
                                 x_ref[pl.ds(2 * t + 1, nblk, stride=2 * CMP_STRIDE), :]], axis=-1)
        lo += jnp.dot((x + pe_lo_ref[t:t + 1, :]).astype(BF16), wlo_ref[t], preferred_element_type=F32)
        hi += jnp.dot((x + pe_hi_ref[t:t + 1, :]).astype(BF16), whi_ref[t], preferred_element_type=F32)
    lo_ref[...] = lo
    hi_ref[...] = hi


def compress_lohi(raw, pe, w1, tr=2048, pool_major=False):
    if pool_major:
        P, _, page = raw.shape
        N = P * page
    else:
        N = raw.shape[0]
    tr = max(d for d in (tr, tr // 2, tr // 4, tr // 8, LANES) if N % d == 0)
    nblk = tr // CMP_STRIDE
    if pool_major:
        x_arg = raw
        x_spec = pl.BlockSpec((tr // page, KV_DIM, page), lambda i: (i, 0, 0))
        scratch = [pltpu.VMEM((tr, LANES), F32), pltpu.VMEM((tr, LANES), F32)]
    else:
        x_arg = raw.reshape(2 * N, LANES)
        x_spec = pl.BlockSpec((2 * tr, LANES), lambda i: (i, 0))
        scratch = []
    eye = jnp.eye(N_KV_HEADS, dtype=F32)

    def blockdiag(w):
        w = w.reshape(CMP_STRIDE, HEAD_DIM, CMP_HIDDEN)
        return jnp.einsum('tdc,hk->thdkc', w, eye).reshape(CMP_STRIDE, KV_DIM, KV_DIM).astype(BF16)

    half = CMP_STRIDE * HEAD_DIM
    pe_lo = jnp.tile(pe[:CMP_STRIDE], (1, N_KV_HEADS))
    pe_hi = jnp.tile(pe[CMP_STRIDE:], (1, N_KV_HEADS))
    const = lambda i: (0, 0)
    const3 = lambda i: (0, 0, 0)
    return pl.pallas_call(
        functools.partial(_compress_lohi_kernel, nblk=nblk, pool_major=pool_major), name='compress_lohi',
        grid=(N // tr,),
        in_specs=[x_spec,
                  pl.BlockSpec((CMP_STRIDE, KV_DIM), const), pl.BlockSpec((CMP_STRIDE, KV_DIM), const),
                  pl.BlockSpec((CMP_STRIDE, KV_DIM, KV_DIM), const3),
                  pl.BlockSpec((CMP_STRIDE, KV_DIM, KV_DIM), const3)],
        out_specs=[pl.BlockSpec((nblk, KV_DIM), lambda i: (i, 0))] * 2,
        out_shape=[jax.ShapeDtypeStruct((N // CMP_STRIDE, KV_DIM), F32)] * 2,
        scratch_shapes=scratch,
        compiler_params=_params(("arbitrary",)),
    )(x_arg, pe_lo, pe_hi, blockdiag(w1[:half]), blockdiag(w1[half:]))


def _compress_out_kernel(lo_ref, hi_ref, w_ref, o_ref):
    z = lo_ref[...] + hi_ref[...]
    z = z * jax.nn.sigmoid(z)
    o_ref[...] = jnp.dot(z.astype(BF16), w_ref[...], preferred_element_type=F32)


def compress_out(lo, hi_next, w2):
    N = lo.shape[0]
    tm = max(d for d in (512, 256, 128, 64, 32, 16, 8) if N % d == 0)
    w = jnp.einsum('cd,hk->hckd', w2, jnp.eye(N_KV_HEADS, dtype=F32)).reshape(KV_DIM, KV_DIM).astype(BF16)
    return pl.pallas_call(
        _compress_out_kernel, name='compress_out', grid=(N // tm,),
        in_specs=[pl.BlockSpec((tm, KV_DIM), lambda i: (i, 0)), pl.BlockSpec((tm, KV_DIM), lambda i: (i, 0)),
                  pl.BlockSpec((KV_DIM, KV_DIM), lambda i: (0, 0))],
        out_specs=pl.BlockSpec((tm, KV_DIM), lambda i: (i, 0)),
        out_shape=jax.ShapeDtypeStruct((N, KV_DIM), F32),
        compiler_params=_params(("arbitrary",)),
    )(lo, hi_next, w)


def _lane_head_mask(rows, h):
    lane = lax.broadcasted_iota(jnp.int32, (rows, KV_DIM), 1)
    return (lane >= h * HEAD_DIM) & (lane < (h + 1) * HEAD_DIM)


def _spread_head(qh, h):
    d = lax.broadcasted_iota(jnp.int32, (HEAD_DIM, KV_DIM), 0)
    lane = lax.broadcasted_iota(jnp.int32, (HEAD_DIM, KV_DIM), 1)
    place = (lane == d + h * HEAD_DIM).astype(BF16)
    return jnp.dot(qh, place, preferred_element_type=F32).astype(BF16)


def _nsa_cmp_kernel(q_ref, kc_ref, vc_ref, cover_ref, o_ref, sel_ref, *, tq, qpos_base, nc):
    R = GROUP * tq
    NCP = kc_ref.shape[0]
    q0 = qpos_base + pl.program_id(1) * tq
    qpos = _row_pos(tq, R, q0)
    c = lax.broadcasted_iota(jnp.int32, (1, NCP), 1)
    mask = (c * CMP_STRIDE + (CMP_LEN - 1) <= qpos) & (c < nc)
    kc = kc_ref[...]
    vc = vc_ref[...]
    own = lax.shift_right_logical(_row_pos(tq, tq, q0), int(np.log2(SLC_BLOCK)))
    out = jnp.zeros((R, KV_DIM), F32)
    imps = []
    for h in range(N_KV_HEADS):
        s = lax.dot_general(_spread_head(q_ref[h], h), kc, (((1,), (1,)), ((), ())),
                            preferred_element_type=F32)
        s = jnp.where(mask, s, NEG_INF)
        m = jnp.max(s, axis=-1, keepdims=True)
        e = jnp.where(mask, jnp.exp2(s - m), 0.0)
        l = jnp.sum(e, axis=-1, keepdims=True)
        p = (e / jnp.where(l > 0.0, l, 1.0)).astype(BF16)
        out = jnp.where(_lane_head_mask(R, h), jnp.dot(p, vc, preferred_element_type=F32), out)
        imp = jnp.dot(p, cover_ref[...], preferred_element_type=F32)
        imps.append(jnp.sum(imp.reshape(GROUP, tq, imp.shape[-1]), axis=0))
    o_ref[...] = out
    sel = _select_blocks(jnp.concatenate(imps, axis=0), jnp.concatenate([own] * N_KV_HEADS, axis=0),
                         SLC_TOPK - 1).astype(F32)
    for h in range(N_KV_HEADS):
        sel_ref[h] = sel[h * tq:(h + 1) * tq]


def nsa_compressed(qpad, kcc, vcc, cover, *, tq, qpos_base, nc):
    B, nQ, _, R, _ = qpad.shape
    NCP, NBP = cover.shape
    return pl.pallas_call(
        functools.partial(_nsa_cmp_kernel, tq=tq, qpos_base=qpos_base, nc=nc), name='nsa_compressed',
        grid=(B, nQ),
        in_specs=[pl.BlockSpec((None, None, N_KV_HEADS, R, HEAD_DIM), lambda b, i: (b, i, 0, 0, 0)),
                  pl.BlockSpec((None, NCP, KV_DIM), lambda b, i: (b, 0, 0)),
                  pl.BlockSpec((None, NCP, KV_DIM), lambda b, i: (b, 0, 0)),
                  pl.BlockSpec((NCP, NBP), lambda b, i: (0, 0))],
        out_specs=[pl.BlockSpec((None, None, R, KV_DIM), lambda b, i: (b, i, 0, 0)),
                   pl.BlockSpec((None, N_KV_HEADS, tq, NBP), lambda b, i: (b, 0, i, 0))],
        out_shape=[jax.ShapeDtypeStruct((B, nQ, R, KV_DIM), F32),
                   jax.ShapeDtypeStruct((B, N_KV_HEADS, nQ * tq, NBP), F32)],
        compiler_params=_params(("arbitrary", "arbitrary")),
    )(qpad, kcc, vcc, cover)


def _index_keys(qi, wi, ki, kpos, qpos, tq):
    rel = jnp.maximum(jnp.dot(qi, ki, preferred_element_type=F32), 0.0)
    score = jnp.zeros((tq, rel.shape[1]), F32)
    for i in range(IDX_HEADS):
        score = score + wi[:, i:i + 1] * rel[i * tq:(i + 1) * tq]
    score = jnp.where(kpos <= qpos, score + 0.0, -jnp.inf)
    bits = lax.bitcast_convert_type(score, jnp.int32)
    return bits ^ (lax.shift_right_arithmetic(bits, 31) & 0x7FFFFFFF)


def _dsa_select_kernel(qi_ref, wi_ref, ki_ref, o_ref, key_ref, seen_ref, *, tq, kt, qpos_base, n_sel, nkt):
    q0 = qpos_base + pl.program_id(1) * tq
    qpos = _row_pos(tq, tq, q0)
    last = jnp.minimum((q0 + tq - 1) // kt, nkt - 1)
    lane = lax.broadcasted_iota(jnp.int32, (1, kt), 1)

    for jt in range(nkt):
        @pl.when(jt <= last)
        def _(jt=jt):
            ki = ki_ref[:, jt * kt:(jt + 1) * kt].astype(BF16)
            key_ref[jt] = _index_keys(qi_ref[...], wi_ref[...], ki, jt * kt + lane, qpos, tq)

    def count(pred):
        def body(jt, acc):
            hit = pred(key_ref[jt])
            for u in range(kt // LANES):
                acc = acc + jnp.where(hit[:, u * LANES:(u + 1) * LANES], 1.0, 0.0)
            return acc
        acc = lax.fori_loop(0, last + 1, body, jnp.zeros((tq, LANES), F32))
        return jnp.sum(acc, axis=-1, keepdims=True)

    def count_ge(t):
        return count(lambda key: key >= t)

    def bit_step(it, t_u):
        cand = t_u | lax.shift_left(jnp.int32(1), 31 - it)
        ok = count_ge(cand ^ INT_MIN) >= n_sel
        return jnp.where(ok, cand, t_u)

    t_u = lax.fori_loop(0, 32, bit_step, jnp.zeros((tq, 1), jnp.int32))
    thr = t_u ^ INT_MIN

    need = n_sel - count(lambda key: key > thr)

    r = lax.broadcasted_iota(jnp.int32, (kt, kt), 0)
    c = lax.broadcasted_iota(jnp.int32, (kt, kt), 1)
    tri = (r <= c).astype(F32)
    seen_ref[...] = jnp.zeros((tq, 1), F32)
    for jt in range(nkt):
        @pl.when(jt <= last)
        def _(jt=jt):
            key = key_ref[jt]
            eq = key == thr
            rank = seen_ref[...] + jnp.dot(eq.astype(F32), tri, preferred_element_type=F32)
            keep = ((key > thr) | (eq & (rank <= need))) & (jt * kt + lane <= qpos)
            o_ref[:, jt * kt:(jt + 1) * kt] = jnp.where(keep, 1.0, 0.0).astype(o_ref.dtype)
            seen_ref[...] = rank[:, kt - 1:kt]

        @pl.when(jt > last)
        def _(jt=jt):
            o_ref[:, jt * kt:(jt + 1) * kt] = jnp.zeros((tq, kt), o_ref.dtype)


def dsa_select(qi, wi, ki_t, *, tq, kt, qpos_base, n_sel):
    B, nQ = qi.shape[:2]
    Lk = ki_t.shape[2]
    nkt = Lk // kt
    return pl.pallas_call(
        functools.partial(_dsa_select_kernel, tq=tq, kt=kt, qpos_base=qpos_base, n_sel=n_sel, nkt=nkt),
        name='dsa_select',
        grid=(B, nQ),
        in_specs=[pl.BlockSpec((None, None, IDX_HEADS * tq, IDX_DIM), lambda b, i: (b, i, 0, 0)),
                  pl.BlockSpec((None, None, tq, IDX_HEADS), lambda b, i: (b, i, 0, 0)),
                  pl.BlockSpec((None, IDX_DIM, Lk), lambda b, i: (b, 0, 0))],
        out_specs=pl.BlockSpec((None, tq, Lk), lambda b, i: (b, i, 0)),
        out_shape=jax.ShapeDtypeStruct((B, nQ * tq, Lk), BF16 if tq % 16 == 0 else F32),
        scratch_shapes=[pltpu.VMEM((nkt, tq, kt), jnp.int32), pltpu.VMEM((tq, 1), F32)],
        compiler_params=_params(("arbitrary", "arbitrary")),
    )(qi, wi, ki_t)


class _PerHead:
    def __init__(self, refs):
        self.refs = refs

    def __getitem__(self, idx):
        idx = idx if isinstance(idx, tuple) else (idx,)
        return self.refs[idx[0]][idx[1:] if len(idx) > 1 else ...]

    def __setitem__(self, idx, value):
        idx = idx if isinstance(idx, tuple) else (idx,)
        self.refs[idx[0]][idx[1:] if len(idx) > 1 else ...] = value

    @property
    def dtype(self):
        return self.refs[0].dtype


def _flash_kernel(si_ref, st_ref, sf_ref, sl_ref, sp_ref, skc_ref, skn_ref, *refs, mode, tq, kt, rc, qpos_base,
                  kpos_base, window, blk, k_major, v_major):
    it = iter(refs)
    q_ref, kfirst_ref, k_ref, v_ref = next(it), next(it), next(it), next(it)
    if mode == 'fox':
        fq_ref, fk_ref = next(it), next(it)
    elif mode == 'blocks':
        sel_ref = next(it)
    elif mode == 'mask':
        mask_ref = next(it)
    o_ref = next(it)
    per_head = lambda: _PerHead([next(it) for _ in range(N_KV_HEADS)])
    m_ref, l_ref, acc_ref, s2_ref, p_ref, a_ref, bias_ref, qp_ref = (per_head() for _ in range(8))

    R = GROUP * tq
    step = pl.program_id(1)
    q0 = qpos_base + si_ref[step] * tq
    kpos0 = kpos_base + st_ref[step] * kt
    slot = sp_ref[step]
    s_ref = _PerHead([r.at[slot] for r in s2_ref.refs])
    s_next_ref = _PerHead([r.at[1 - slot] for r in s2_ref.refs])

    def qk(h, keys_ref):
        kb = keys_ref[...].astype(BF16)
        if k_major:
            return jnp.dot(qp_ref[h], kb, preferred_element_type=F32)
        return lax.dot_general(qp_ref[h], kb, (((1,), (1,)), ((), ())), preferred_element_type=F32)

    @pl.when(sf_ref[step] == 1)
    def _():
        for h in range(N_KV_HEADS):
            m_ref[h] = jnp.full((R, 1), NEG_INF, F32)
            l_ref[h] = jnp.zeros((R, LANES), F32)
            acc_ref[h] = jnp.zeros((R, KV_DIM), F32)
            qp_ref[h] = _spread_head(q_ref[h], h)
            s_ref[h] = qk(h, kfirst_ref)

    def tile_mask():
        kpos = kpos0 + lax.broadcasted_iota(jnp.int32, (1, kt), 1)
        qpos = _row_pos(tq, tq, q0)
        if mode == 'mask':
            return mask_ref[...].astype(F32) > 0.5
        mask = kpos <= qpos
        if window is not None:
            mask = mask & (qpos - kpos < window) & (kpos >= 0)
        return mask

    def softmax_head(h, use_bias):
        hb = h if mode == 'blocks' else 0
        for g in range(GROUP):
            if mode == 'fox':
                fk = jnp.broadcast_to(fk_ref[h * GROUP + g:h * GROUP + g + 1, :], (rc, kt))

            def logits(t0):
                u = s_ref[h, g * tq + t0:g * tq + t0 + rc, :]
                if mode == 'fox':
                    u = u - fk
                if use_bias:
                    u = u + bias_ref[hb, t0:t0 + rc, :]
                return u

            for t0 in range(0, tq, rc):
                rows = slice(g * tq + t0, g * tq + t0 + rc)
                row = jnp.max(logits(t0), axis=-1, keepdims=True)
                if mode == 'fox':
                    row = row + fq_ref[h, rows, :]
                m_prev = m_ref[h, rows, :]
                m_new = jnp.maximum(m_prev, row)
                m_ref[h, rows, :] = m_new
                a_ref[h, rows, :] = jnp.exp2(m_prev - m_new)
            for t0 in range(0, tq, rc):
                rows = slice(g * tq + t0, g * tq + t0 + rc)
                shift = -m_ref[h, rows, :]
                if mode == 'fox':
                    shift = shift + fq_ref[h, rows, :]
                p = jnp.exp2(logits(t0) + shift)
                part = p[:, 0:LANES]
                for u in range(1, kt // LANES):
                    part = part + p[:, u * LANES:(u + 1) * LANES]
                l_ref[h, rows, :] = a_ref[h, rows, :] * l_ref[h, rows, :] + part
                p_ref[h, rows, :] = p.astype(p_ref.dtype)

    def heads(use_bias):
        vb = v_ref[...].astype(BF16)
        for h in range(N_KV_HEADS):
            s_next_ref[h] = qk(h, k_ref)
        if use_bias:
            base = jnp.where(tile_mask(), 0.0, -jnp.inf)
            if mode != 'blocks':
                bias_ref[0] = base
        if mode == 'blocks':
            nbp = sel_ref.shape[-1]
            kblk = lax.shift_right_logical(kpos0 + lax.broadcasted_iota(jnp.int32, (nbp, kt), 1),
                                           int(np.log2(blk)))
            expand = (kblk == lax.broadcasted_iota(jnp.int32, (nbp, kt), 0)).astype(F32)
        for h in range(N_KV_HEADS):
            if mode == 'blocks':
                picked = jnp.dot(sel_ref[h], expand, preferred_element_type=F32) > 0.5
                bias_ref[h] = jnp.where(picked, base, -jnp.inf)
        for h in range(N_KV_HEADS):
            softmax_head(h, use_bias)
            pb = p_ref[h].astype(BF16)
            if v_major:
                pv = lax.dot_general(pb, vb, (((1,), (1,)), ((), ())), preferred_element_type=F32)
            else:
                pv = jnp.dot(pb, vb, preferred_element_type=F32)
            acc_ref[h] = a_ref[h] * acc_ref[h] + pv

    if mode in ('plain', 'fox') and window is None:
        crosses = kpos0 + kt - 1 > q0
        pl.when(crosses)(lambda: heads(True))
        pl.when(jnp.logical_not(crosses))(lambda: heads(False))
    else:
        heads(True)

    @pl.when(sl_ref[step] == 1)
    def _():
        out = jnp.zeros((R, KV_DIM), F32)
        for h in range(N_KV_HEADS):
            l = jnp.sum(l_ref[h], axis=-1, keepdims=True)
            out = jnp.where(_lane_head_mask(R, h), acc_ref[h] / jnp.where(l > 0.0, l, 1.0), out)
        o_ref[...] = out


def flash_attention(q, k, v, *, tq, kt, qpos_base, kpos_base=0, kv_len, mode='plain', window=None,
                    fq=None, fkT=None, sel=None, blk=None, mask=None, k_major=False, v_major=False):
    B, nQ, _, R, _ = q.shape
    last_tile = (kv_len - 1) // kt

    steps = []
    for i in range(nQ):
        lo = 0 if window is None else max(qpos_base + i * tq - (window - 1) - kpos_base, 0) // kt
        hi = min((qpos_base + (i + 1) * tq - 1 - kpos_base) // kt, last_tile)
        steps += [(i, t, int(t == lo), int(t == hi), (t - lo) % 2, lo, min(t + 1, hi)) for t in range(lo, hi + 1)]
    tables = [jnp.asarray(np.array(col, np.int32)) for col in zip(*steps)]
    SI, ST, SKC, SKN = 0, 1, 5, 6

    def q_map(b, s, *tab):
        return (b, tab[SI][s], 0, 0, 0)

    def kv_map(major, which):
        if major:
            return lambda b, s, *tab: (b, 0, tab[which][s])
        return lambda b, s, *tab: (b, tab[which][s], 0)

    def kv_spec(major, which):
        return pl.BlockSpec((None, KV_DIM, kt) if major else (None, kt, KV_DIM), kv_map(major, which))

    args = [q, k, k, v]
    specs = [pl.BlockSpec((None, None, N_KV_HEADS, R, HEAD_DIM), q_map), kv_spec(k_major, SKC),
             kv_spec(k_major, SKN), kv_spec(v_major, ST)]
    rc = min(tq, 16)
    scratch = []
    for shape, dtype in (((R, 1), F32), ((R, LANES), F32), ((R, KV_DIM), F32), ((2, R, kt), F32),
                         ((R, kt), BF16 if rc % 16 == 0 else F32), ((R, 1), F32), ((tq, kt), F32),
                         ((R, KV_DIM), BF16)):
        scratch += [pltpu.VMEM(shape, dtype)] * N_KV_HEADS
    if mode == 'fox':
        args += [fq, fkT]
        specs += [pl.BlockSpec((None, None, N_KV_HEADS, R, 1), q_map),
                  pl.BlockSpec((None, N_HEADS, kt), kv_map(True, ST))]
    elif mode == 'blocks':
        NBP = sel.shape[-1]
        args += [sel]
        specs += [pl.BlockSpec((None, N_KV_HEADS, tq, NBP), lambda b, s, *tab: (b, 0, tab[SI][s], 0))]
    elif mode == 'mask':
        args += [mask]
        specs += [pl.BlockSpec((None, tq, kt), lambda b, s, *tab: (b, tab[SI][s], tab[ST][s]))]
    return pl.pallas_call(
        functools.partial(_flash_kernel, mode=mode, tq=tq, kt=kt, rc=rc, qpos_base=qpos_base,
                          kpos_base=kpos_base, window=window, blk=blk, k_major=k_major, v_major=v_major),
        name='flash_' + mode + ('_win' if window else ''),
        grid_spec=pltpu.PrefetchScalarGridSpec(
            num_scalar_prefetch=len(tables), grid=(B, len(steps)), in_specs=specs,
            out_specs=pl.BlockSpec((None, None, R, KV_DIM), lambda b, s, *tab: (b, tab[SI][s], 0, 0)),
            scratch_shapes=scratch),
        out_shape=jax.ShapeDtypeStruct((B, nQ, R, KV_DIM), F32),
        compiler_params=_params(("arbitrary", "arbitrary")),
    )(*tables, *args)


def _moe_kernel(x_ref, g_ref, sc_ref, sh_ref, gate_ref, c_ref, w1_ref, w3_ref, w2_ref, o_ref, h_ref, acc_ref):
    e = pl.program_id(1)

    @pl.when(e == 0)
    def _():
        h_ref[...] = _adanorm(x_ref[...], g_ref[...], sc_ref[...], sh_ref[...]).astype(BF16)
        acc_ref[...] = jnp.zeros_like(acc_ref)

    h = h_ref[...]
    a = jnp.dot(h, w1_ref[...], preferred_element_type=F32)
    a = (a * jax.nn.sigmoid(a)) * jnp.dot(h, w3_ref[...], preferred_element_type=F32)
    y = jnp.dot(a.astype(BF16), w2_ref[...], preferred_element_type=F32)
    c = c_ref[...]
    lane = lax.broadcasted_iota(jnp.int32, c.shape, 1)
    ce = jnp.sum(jnp.where(lane == e, c, 0.0), axis=-1, keepdims=True)
    acc_ref[...] += jnp.where(ce != 0.0, ce * y, 0.0)

    @pl.when(e == pl.num_programs(1) - 1)
    def _():
        o_ref[...] = x_ref[...] + gate_ref[...] * acc_ref[...]


def moe_dense(x, coef, gain, sc, sh, gate, w1, w3, w2, *, seq_len, tm):
    N, D = x.shape
    E, _, DE = w1.shape
    tm = min(tm, N)
    row = lambda i, e: (i, 0)
    specs = [pl.BlockSpec((tm, D), row), pl.BlockSpec((1, D), lambda i, e: (0, 0))]
    args = [x, gain]
    for m in (sc, sh, gate):
        a, s = _mod_spec(m, tm, seq_len)
        args.append(a)
        imap = s.index_map
        specs.append(pl.BlockSpec(s.block_shape, lambda i, e, imap=imap: imap(i)))
    args += [coef, w1, w3, w2]
    specs += [pl.BlockSpec((tm, E), row),
              pl.BlockSpec((None, D, DE), lambda i, e: (e, 0, 0)),
              pl.BlockSpec((None, D, DE), lambda i, e: (e, 0, 0)),
              pl.BlockSpec((None, DE, D), lambda i, e: (e, 0, 0))]
    return pl.pallas_call(
        _moe_kernel, name='moe_dense', grid=(N // tm, E), in_specs=specs,
        out_specs=pl.BlockSpec((tm, D), row),
        out_shape=jax.ShapeDtypeStruct((N, D), F32),
        scratch_shapes=[pltpu.VMEM((tm, D), BF16), pltpu.VMEM((tm, D), F32)],
        compiler_params=_params(("arbitrary", "arbitrary")),
    )(*args)


def _head_norm(x, g):
    return x * lax.rsqrt(jnp.mean(x * x, axis=-1, keepdims=True) + EPS) * g


def _rope(x, pos):
    half = ROT_DIM // 2
    inv = ROPE_THETA ** (-jnp.arange(half, dtype=F32) / half)
    ang = pos.astype(F32)[:, None] * inv[None, :]
    cos, sin = jnp.cos(ang)[:, None, :], jnp.sin(ang)[:, None, :]
    x1, x2 = x[..., :half], x[..., half:ROT_DIM]
    return jnp.concatenate([x1 * cos - x2 * sin, x2 * cos + x1 * sin, x[..., ROT_DIM:]], axis=-1)


def _split(y, sizes):
    return jnp.split(y[..., :sum(sizes)], np.cumsum(sizes)[:-1].tolist(), axis=-1)


def _pad_cols(w, mult=256):
    return jnp.pad(w, ((0, 0), (0, _round_up(w.shape[1], mult) - w.shape[1])))


def _qrows(q, tq):
    B, T = q.shape[:2]
    nQ = T // tq
    q = (q * (ATTN_SCALE * LOG2E)).astype(BF16).reshape(B, nQ, tq, N_KV_HEADS, GROUP, HEAD_DIM)
    return q.transpose(0, 1, 3, 4, 2, 5).reshape(B, nQ, N_KV_HEADS, GROUP * tq, HEAD_DIM)


def _gate_rows(gates, tq):
    B, T, _, n = gates.shape
    g = gates.reshape(B, T // tq, tq, N_KV_HEADS, GROUP, n).transpose(0, 1, 4, 2, 3, 5)
    return g.reshape(B, T // tq, GROUP * tq, N_KV_HEADS * n)


def _unpad_out(o, tq):
    B, nQ = o.shape[:2]
    o = o.reshape(B, nQ, GROUP, tq, N_KV_HEADS, HEAD_DIM).transpose(0, 1, 3, 4, 2, 5)
    return o.reshape(B, nQ * tq, N_HEADS, HEAD_DIM)


def _rows_gt(a, tq):
    B, T, _ = a.shape
    a = a.reshape(B, T // tq, tq, N_KV_HEADS, GROUP).transpose(0, 1, 3, 4, 2)
    return a.reshape(B, T // tq, N_KV_HEADS, GROUP * tq, 1)


def _pad_len(a, L):
    return jnp.pad(a, ((0, 0), (0, L - a.shape[1])) + ((0, 0),) * (a.ndim - 2))


def _modulation(c, ada_w, ada_b):
    m = jax.nn.silu(c) @ ada_w + ada_b
    return [a[:, None, :] for a in jnp.split(m, 6, axis=-1)]


class _Stream:
    def __init__(self, B, T, past):
        self.B, self.T, self.past = B, T, past
        self.tq = min(T, 256)
        self.tm = min(B * T, 512)
        self.pos = past + jnp.arange(T)


def _kv_tile(L):
    return max(d for d in (512, 256, 128) if L % d == 0)


def kernel(x_prompt, x_sample, cache_fox_k, cache_fox_v, cache_fox_logf, cache_moba_k, cache_moba_v,
           cache_nsa_cmp_k, cache_nsa_cmp_v, cache_nsa_slc_k, cache_nsa_slc_v, state_nsa_win_k, state_nsa_win_v,
           cache_dsa_k, cache_dsa_v, cache_dsa_idx_k, page_table, c_prompt, c_sample,
           ada_w, ada_b, norm_mix, norm_ffn,
           fox_w_in, fox_b_f, fox_q_norm, fox_k_norm, fox_w_o,
           moba_w_in, moba_q_norm, moba_k_norm, moba_w_o,
           nsa_w_in, nsa_q_norm, nsa_kc_norm, nsa_ks_norm, nsa_kw_norm, nsa_pe_k, nsa_w1_k, nsa_w2_k,
           nsa_pe_v, nsa_w1_v, nsa_w2_v, nsa_w_o,
           dsa_w_in, dsa_q_norm, dsa_k_norm, dsa_w_o,
           router_w, router_b, moe_w1, moe_w3, moe_w2):
    D = x_prompt.shape[-1]
    page = cache_fox_k.shape[1]
    past = page_table.shape[1] * page
    streams = (_Stream(x_prompt.shape[0], x_prompt.shape[1], 0),
               _Stream(x_sample.shape[0], x_sample.shape[1], past))
    conds = (c_prompt, c_sample)
    xs = [x_prompt.reshape(-1, D), x_sample.reshape(-1, D)]
    outs = [dict(), dict()]

    w_in = {'fox': fox_w_in, 'moba': moba_w_in, 'nsa': nsa_w_in, 'dsa': dsa_w_in}
    w_out = {'fox': fox_w_o, 'moba': moba_w_o, 'nsa': nsa_w_o, 'dsa': dsa_w_o}
    w_in = {n: _pad_cols(w).astype(BF16) for n, w in w_in.items()}
    w_out = {n: w.astype(BF16) for n, w in w_out.items()}
    moe_w1b, moe_w3b, moe_w2b = moe_w1.astype(BF16), moe_w3.astype(BF16), moe_w2.astype(BF16)

    def keys_of(st, new, cache, dtype=F32):
        B, T = new.shape[:2]
        if st.past == 0:
            return new.reshape(B, T, -1).astype(dtype), T, False
        return gather_rows_t(cache, page_table, new, dtype), st.past + T, True

    def attend(st, q, K, V, L, major, **kw):
        Lk = K.shape[2] if major else K.shape[1]
        kt = _kv_tile(Lk)
        if st.tq < 16 and Lk % (2 * kt) == 0:
            kt *= 2
        return flash_attention(q, K, V, tq=st.tq, kt=kt, qpos_base=st.past, kv_len=L,
                               k_major=major, v_major=major, **kw)

    def channels_major(a, major):
        return a if major else a.transpose(0, 2, 1)

    def fox(st, y, o):
        B, T = st.B, st.T
        q, k, v, fl = _split(y, [N_HEADS * HEAD_DIM, KV_DIM, KV_DIM, N_HEADS])
        q = _head_norm(q.reshape(B, T, N_HEADS, HEAD_DIM), fox_q_norm)
        k = _head_norm(k.reshape(B, T, N_KV_HEADS, HEAD_DIM), fox_k_norm)
        v = v.reshape(B, T, N_KV_HEADS, HEAD_DIM)
        logf = jax.nn.log_sigmoid(fl.reshape(B, T, N_HEADS) + fox_b_f)
        o.update(fox_k=k, fox_v=v, fox_logf=logf)
        K, L, major = keys_of(st, k, cache_fox_k, BF16)
        V, _, _ = keys_of(st, v, cache_fox_v, BF16)
        LF, _, lf_major = keys_of(st, logf, cache_fox_logf)
        LF = channels_major(LF, lf_major)
        F = cumsum_lanes(LF.reshape(B * N_HEADS, -1)).reshape(LF.shape) * LOG2E
        fq = _rows_gt(lax.slice_in_dim(F, st.past, st.past + T, axis=2).transpose(0, 2, 1), st.tq)
        return [attend(st, _qrows(q, st.tq), K, V, L, major, mode='fox', fq=fq, fkT=F)], None

    def moba(st, y, o):
        B, T = st.B, st.T
        q, k, v = _split(y, [N_HEADS * HEAD_DIM, KV_DIM, KV_DIM])
        q = _rope(_head_norm(q.reshape(B, T, N_HEADS, HEAD_DIM), moba_q_norm), st.pos)
        k = _rope(_head_norm(k.reshape(B, T, N_KV_HEADS, HEAD_DIM), moba_k_norm), st.pos)
        v = v.reshape(B, T, N_KV_HEADS, HEAD_DIM)
        o.update(moba_k=k, moba_v=v)
        K, L, major = keys_of(st, k, cache_moba_k)
        V, _, _ = keys_of(st, v, cache_moba_v, BF16)
        Kt = channels_major(K, major)
        NBP = _round_up(Kt.shape[2] // MOBA_BLOCK, LANES)
        kbar = block_means_t(Kt, MOBA_BLOCK, NBP)
        kbt = jnp.tile(kbar.reshape(B, N_KV_HEADS, 1, HEAD_DIM, NBP), (1, 1, GROUP, 1, 1))
        kbt = kbt.reshape(B, N_KV_HEADS, KV_DIM, NBP)
        sel = moba_select(q.reshape(B, T, -1), kbt, tq=st.tq, qpos_base=st.past)
        return [attend(st, _qrows(q, st.tq), K, V, L, major, mode='blocks', sel=sel, blk=MOBA_BLOCK)], None

    def compress(st, raw_new, cache, pe, w1, w2):
        B, T = st.B, st.T
        if st.past == 0:
            lo, hi = compress_lohi(raw_new.reshape(B * T, KV_DIM), pe, w1)
            nblk = T // CMP_STRIDE
            lo, hi = lo.reshape(B, nblk, KV_DIM), hi.reshape(B, nblk, KV_DIM)
            L = T
        else:
            P = cache.shape[0]
            lo_p, hi_p = compress_lohi(pool_t(cache), pe, w1, pool_major=True)
            tail = jnp.pad(raw_new.reshape(B, T, KV_DIM), ((0, 0), (0, page - T), (0, 0)))
            lo_t, hi_t = compress_lohi(tail.reshape(B * page, KV_DIM), pe, w1)
            bpp = page // CMP_STRIDE
            lo = gather_pages(lo_p.reshape(P, bpp, KV_DIM), page_table, lo_t.reshape(B, bpp, KV_DIM))
            hi = gather_pages(hi_p.reshape(P, bpp, KV_DIM), page_table, hi_t.reshape(B, bpp, KV_DIM))
            L = st.past + T
        nc = _round_up(L, SLC_BLOCK) // CMP_STRIDE - 1
        NCP = _round_up(nc, LANES)
        lo = _pad_len(lo, NCP + 1)
        hi = _pad_len(hi, NCP + 1)
        out = compress_out(lo[:, :NCP].reshape(B * NCP, KV_DIM), hi[:, 1:NCP + 1].reshape(B * NCP, KV_DIM), w2)
        return out.reshape(B, NCP, KV_DIM), nc

    def nsa(st, y, o, win_k, win_v):
        B, T = st.B, st.T
        q, kc, vc, ks, vs, kw, vw, gl = _split(y, [N_HEADS * HEAD_DIM] + [KV_DIM] * 6 + [3 * N_HEADS])
        heads = lambda a: a.reshape(B, T, N_KV_HEADS, HEAD_DIM)
        q = _rope(_head_norm(q.reshape(B, T, N_HEADS, HEAD_DIM), nsa_q_norm), st.pos)
        ks = _rope(_head_norm(heads(ks), nsa_ks_norm), st.pos)
        kw = _rope(_head_norm(heads(kw), nsa_kw_norm), st.pos)
        kc, vc, vs, vw = heads(kc), heads(vc), heads(vs), heads(vw)
        gates = jax.nn.sigmoid(gl).reshape(B, T, N_HEADS, 3)
        o.update(nsa_cmp_k=kc, nsa_cmp_v=vc, nsa_slc_k=ks, nsa_slc_v=vs)
        qp = _qrows(q, st.tq)
        kcc, nc = compress(st, kc, cache_nsa_cmp_k, nsa_pe_k, nsa_w1_k, nsa_w2_k)
        vcc, _ = compress(st, vc, cache_nsa_cmp_v, nsa_pe_v, nsa_w1_v, nsa_w2_v)
        NCP = kcc.shape[1]
        cend = jnp.arange(NCP) * CMP_STRIDE + CMP_LEN - 1
        kcc = _rope(_head_norm(kcc.reshape(B, NCP, N_KV_HEADS, HEAD_DIM), nsa_kc_norm), cend)
        KS, L, major = keys_of(st, ks, cache_nsa_slc_k, BF16)
        VS, _, _ = keys_of(st, vs, cache_nsa_slc_v, BF16)
        nbs = _round_up(L, SLC_BLOCK) // SLC_BLOCK
        NBP = _round_up(nbs, LANES)
        ci = np.arange(NCP)[:, None]
        sj = np.arange(NBP)[None, :]
        cover = ((ci * CMP_STRIDE < (sj + 1) * SLC_BLOCK) & (ci * CMP_STRIDE + CMP_LEN > sj * SLC_BLOCK)
                 & (ci < nc) & (sj < nbs))
        oc, sel = nsa_compressed(qp, kcc.reshape(B, NCP, KV_DIM).astype(BF16), vcc.astype(BF16),
                                 jnp.asarray(cover, BF16), tq=st.tq, qpos_base=st.past, nc=nc)
        osl = attend(st, qp, KS, VS, L, major, mode='blocks', sel=sel, blk=SLC_BLOCK)
        if st.past == 0:
            KW, VW, kbase = kw.reshape(B, T, KV_DIM), vw.reshape(B, T, KV_DIM), 0
            wb = min(WINDOW, T)
            o.update(nsa_win_k=kw[:, T - wb:], nsa_win_v=vw[:, T - wb:])
        else:
            KWf = jnp.concatenate([win_k, kw], axis=1)
            VWf = jnp.concatenate([win_v, vw], axis=1)
            lw = KWf.shape[1]
            wb = min(WINDOW, lw)
            o.update(nsa_win_k=KWf[:, lw - wb:], nsa_win_v=VWf[:, lw - wb:])
            kbase = st.past - win_k.shape[1]
            Lp = _round_up(lw, LANES)
            KW = _pad_len(KWf.reshape(B, lw, KV_DIM), Lp)
            VW = _pad_len(VWf.reshape(B, lw, KV_DIM), Lp)
        ow = flash_attention(qp, KW, VW, tq=st.tq, kt=_kv_tile(KW.shape[1]), qpos_base=st.past,
                             kpos_base=kbase, kv_len=KW.shape[1], mode='plain', window=WINDOW)
        return [oc, osl, ow], gates

    def dsa(st, y, o):
        B, T, tq = st.B, st.T, st.tq
        q, k, v, qi, ki, wi = _split(y, [N_HEADS * HEAD_DIM, KV_DIM, KV_DIM, IDX_HEADS * IDX_DIM, IDX_DIM, IDX_HEADS])
        q = _rope(_head_norm(q.reshape(B, T, N_HEADS, HEAD_DIM), dsa_q_norm), st.pos)
        k = _rope(_head_norm(k.reshape(B, T, N_KV_HEADS, HEAD_DIM), dsa_k_norm), st.pos)
        v = v.reshape(B, T, N_KV_HEADS, HEAD_DIM)
        qi = _rope(qi.reshape(B, T, IDX_HEADS, IDX_DIM), st.pos)
        ki = _rope(ki.reshape(B, T, 1, IDX_DIM), st.pos)[:, :, 0]
        wi = wi * IDX_HEADS ** -0.5
        o.update(dsa_k=k, dsa_v=v, dsa_idx=ki)
        K, L, major = keys_of(st, k, cache_dsa_k, BF16)
        V, _, _ = keys_of(st, v, cache_dsa_v, BF16)
        KI, _, ki_major = keys_of(st, ki, cache_dsa_idx_k)
        KI = channels_major(KI, ki_major)
        kt = _kv_tile(KI.shape[2])
        ts = min(tq, 128)
        nS = T // ts
        qi_r = qi.astype(BF16).reshape(B, nS, ts, IDX_HEADS, IDX_DIM).transpose(0, 1, 3, 2, 4)
        qi_r = qi_r.reshape(B, nS, IDX_HEADS * ts, IDX_DIM)
        wi_r = wi.reshape(B, nS, ts, IDX_HEADS)
        n_sel = min(DSA_TOPK, L // 4)
        picked = dsa_select(qi_r, wi_r, KI, tq=ts, kt=kt, qpos_base=st.past, n_sel=n_sel)
        return [attend(st, _qrows(q, tq), K, V, L, major, mode='mask', mask=picked)], None

    for layer in range(ada_w.shape[0]):
        kind = ('fox', 'moba', 'nsa', 'dsa')[layer % 4]
        for s, st in enumerate(streams):
            x = xs[s]
            sh1, sc1, g1, sh2, sc2, g2 = _modulation(conds[s], ada_w[layer], ada_b[layer])
            y = fused_matmul(x, w_in[kind], seq_len=st.T, tm=st.tm,
                             norm=(norm_mix[layer][None, :], sc1, sh1)).reshape(st.B, st.T, -1)
            if kind == 'fox':
                branches, gates = fox(st, y, outs[s])
            elif kind == 'moba':
                branches, gates = moba(st, y, outs[s])
            elif kind == 'nsa':
                branches, gates = nsa(st, y, outs[s], state_nsa_win_k, state_nsa_win_v)
            else:
                branches, gates = dsa(st, y, outs[s])
            if st.tq % 16 == 0:
                gts = None if gates is None else _gate_rows(gates, st.tq)
                x = proj_heads(branches, w_out[kind], x, g1, tq=st.tq, gts=gts)
            else:
                att = [_unpad_out(a, st.tq) for a in branches]
                if gates is not None:
                    att = [sum(gates[..., c:c + 1] * a for c, a in enumerate(att))]
                x = fused_matmul(att[0].reshape(-1, N_HEADS * HEAD_DIM), w_out[kind], seq_len=st.T, tm=st.tm,
                                 epi=(x, g1))
            ffn_norm = (norm_ffn[layer][None, :], sc2, sh2)
            coef = route(x, router_w, router_b, ffn_norm, seq_len=st.T, tm=st.tm)
            xs[s] = moe_dense(x, coef, *ffn_norm, g2, moe_w1b[layer], moe_w3b[layer], moe_w2b[layer],
                              seq_len=st.T, tm=2 * st.tm)

    names = ('fox_k', 'fox_v', 'fox_logf', 'moba_k', 'moba_v', 'nsa_cmp_k', 'nsa_cmp_v', 'nsa_slc_k', 'nsa_slc_v',
             'nsa_win_k', 'nsa_win_v', 'dsa_k', 'dsa_v', 'dsa_idx')
    res = [xs[0].reshape(x_prompt.shape), xs[1].reshape(x_sample.shape)]
    for n in names:
        res += [outs[0][n], outs[1][n]]
    return tuple(res)
```
